```python
import jax
import jax.numpy as jnp
from jax import lax
import numpy as np


D_MODEL = 1024
BATCH = 4
SEQ = 8192
DEPTH = 4

GRID_W = 64
CTX_LEN = 256

N_HEADS_MLA = 8
Q_LORA = 384
KV_LORA = 256
QK_NOPE = 64
QK_ROPE = 32
QK_HEAD = QK_NOPE + QK_ROPE
V_HEAD = 64
W_MLA = N_HEADS_MLA * V_HEAD
ROPE_THETA = 10000.0
Q_BLOCK = 128
ATTN_SCALE = QK_HEAD ** -0.5

CONV_W = 512
CONV_K = 3

RWKV_HEADS = 8
RWKV_HEAD = 64
W_RWKV = RWKV_HEADS * RWKV_HEAD
W_LORA = 64
A_LORA = 64

N_BRANCH = 3
IN_SIZES = (Q_LORA, KV_LORA, QK_ROPE, W_MLA,
            CONV_W, CONV_W, CONV_W, CONV_W,
            W_RWKV, W_RWKV, W_RWKV,
            W_LORA, W_LORA, A_LORA, A_LORA,
            W_RWKV,
            N_BRANCH * D_MODEL)
D_IN = sum(IN_SIZES)

RMS_EPS = 1e-6
GN_EPS = 64e-5
L2_EPS = 1e-12

kernel_name = 'hybrid_mla_shortconv_rwkv7_dit'


def rms_norm(x, g):
    xf = x.astype(jnp.float32)
    y = xf * lax.rsqrt(jnp.mean(xf * xf, axis=-1, keepdims=True) + RMS_EPS)
    return (y * g.astype(jnp.float32)).astype(x.dtype)


def ada_mod(cvec, w_mod, b_mod):
    m = jax.nn.silu(cvec) @ w_mod + b_mod
    shift, scale, gate = jnp.split(m, 3, axis=-1)
    return shift[..., None, :], scale[..., None, :], gate[..., None, :]


def project_in(h_res, mod, g_pre, w_in):
    shift, scale, _ = mod
    h = rms_norm(h_res, g_pre) * (1 + scale) + shift
    return jnp.split(h @ w_in, np.cumsum(IN_SIZES)[:-1].tolist(), axis=-1)


def axial_rope_tables(length):
    n_rows = length // GRID_W
    row = jnp.repeat(jnp.arange(n_rows), GRID_W).astype(jnp.float32)
    col = jnp.tile(jnp.arange(GRID_W), n_rows).astype(jnp.float32)
    n_freq = QK_ROPE // 4
    inv = ROPE_THETA ** (-jnp.arange(n_freq, dtype=jnp.float32) / n_freq)
    ang = jnp.concatenate([row[:, None] * inv, col[:, None] * inv], axis=-1)
    return jnp.cos(ang), jnp.sin(ang)


def apply_rope(x, cos, sin):
    xf = x.astype(jnp.float32)
    x1, x2 = xf[..., 0::2], xf[..., 1::2]
    out = jnp.stack([x1 * cos - x2 * sin, x1 * sin + x2 * cos], axis=-1).reshape(x.shape)
    return out.astype(x.dtype)


def mla_queries(q_lat, g_q, w_uq, rope):
    bsz, length = q_lat.shape[:2]
    q = (rms_norm(q_lat, g_q) @ w_uq).reshape(bsz, length, N_HEADS_MLA, QK_HEAD)
    q_nope, q_rot = q[..., :QK_NOPE], q[..., QK_NOPE:]
    if rope is not None:
        q_rot = apply_rope(q_rot, rope[0][:, None, :], rope[1][:, None, :])
    return jnp.concatenate([q_nope, q_rot], axis=-1)


def mla_keys_values(kv_lat, k_rope, g_kv, w_ukv, rope):
    bsz, length = kv_lat.shape[:2]
    kv = (rms_norm(kv_lat, g_kv) @ w_ukv).reshape(bsz, length, N_HEADS_MLA, QK_NOPE + V_HEAD)
    k_nope, v = kv[..., :QK_NOPE], kv[..., QK_NOPE:]
    if rope is not None:
        k_rope = apply_rope(k_rope, rope[0], rope[1])
    k_rot = jnp.broadcast_to(k_rope[:, :, None, :], (bsz, length, N_HEADS_MLA, QK_ROPE))
    return jnp.concatenate([k_nope, k_rot], axis=-1), v


def softmax_attention(q, k, v):
    s = jnp.einsum('bqhd,bkhd->bhqk', q.astype(jnp.float32), k.astype(jnp.float32)) * ATTN_SCALE
    p = jax.nn.softmax(s, axis=-1)
    return jnp.einsum('bhqk,bkhd->bqhd', p, v.astype(jnp.float32))


def blocked_attention(q, k, v):
    bsz, length, heads, dk = q.shape
    qb = jnp.moveaxis(q.reshape(bsz, length // Q_BLOCK, Q_BLOCK, heads, dk), 1, 0)
    out = lax.map(lambda qi: softmax_attention(qi, k, v), qb)
    return jnp.moveaxis(out, 0, 1).reshape(bsz, length, heads * V_HEAD).astype(v.dtype)


def short_conv(x_in, b_gate, c_gate, conv_w, conv_b):
    u = c_gate * x_in
    up = jnp.pad(u, ((0, 0), (1, 1), (0, 0)))
    y = up[:, :-2] * conv_w[0] + up[:, 1:-1] * conv_w[1] + up[:, 2:] * conv_w[2] + conv_b
    return b_gate * y


def wkv_scan(state0, r, decay, k, v, kk, a, reverse, emit):
    xs = tuple(jnp.moveaxis(t, 1, 0) for t in (r, decay, k, v, kk, a))

    def step(S, inp):
        r_t, w_t, k_t, v_t, kk_t, a_t = inp
        s_kk = jnp.einsum('bhvk,bhk->bhv', S, kk_t)
        S = S * w_t[:, :, None, :] - s_kk[..., None] * (kk_t * a_t)[:, :, None, :] + v_t[..., None] * k_t[:, :, None, :]
        y = jnp.einsum('bhvk,bhk->bhv', S, r_t) if emit else None
        return S, y

    S, ys = lax.scan(step, state0, xs, reverse=reverse)
    return S, (jnp.moveaxis(ys, 0, 1) if emit else None)


def head_group_norm(y, g, b):
    mu = jnp.mean(y, axis=-1, keepdims=True)
    var = jnp.mean(jnp.square(y - mu), axis=-1, keepdims=True)
    yn = (y - mu) * lax.rsqrt(var + GN_EPS)
    shp = (RWKV_HEADS, RWKV_HEAD)
    return yn * g.astype(jnp.float32).reshape(shp) + b.astype(jnp.float32).reshape(shp)


def rwkv_direction(d, r, k, v, kk, wd, ad, p, state0, emit):
    f32 = jnp.float32
    shp = r.shape
    logw = -jax.nn.softplus(-(p['w0'][d] + jnp.tanh(wd) @ p['w_up'][d]).astype(f32)) - 0.5
    decay = jnp.exp(-jnp.exp(logw)).reshape(shp)
    a = jax.nn.sigmoid((p['a0'][d] + ad @ p['a_up'][d]).astype(f32)).reshape(shp)
    k_d = k * (1 + (a - 1) * p['k_a'].astype(f32).reshape(RWKV_HEADS, RWKV_HEAD))
    S, y = wkv_scan(state0, r, decay, k_d, v, kk, a, reverse=(d == 1), emit=emit)
    bonus = jnp.sum(r * k_d * p['r_k'].astype(f32), axis=-1, keepdims=True) * v if emit else None
    return S, y, bonus


def rwkv_branch(parts, p, states0, emit):
    r, k, v, wd_f, wd_b, ad_f, ad_b = parts
    bsz, length = r.shape[:2]
    shp = (bsz, length, RWKV_HEADS, RWKV_HEAD)
    r, k, v = (t.astype(jnp.float32).reshape(shp) for t in (r, k, v))
    kk = k * p['k_k'].astype(jnp.float32).reshape(RWKV_HEADS, RWKV_HEAD)
    kk = kk * lax.rsqrt(jnp.sum(kk * kk, axis=-1, keepdims=True) + L2_EPS)
    S_f, y_f, bonus_f = rwkv_direction(0, r, k, v, kk, wd_f, ad_f, p, states0[0], emit)
    S_b, y_b, bonus_b = rwkv_direction(1, r, k, v, kk, wd_b, ad_b, p, states0[1], emit)
    if not emit:
        return (S_f, S_b), None
    y = head_group_norm(y_f + y_b, p['gn_g'], p['gn_b']) + bonus_f + bonus_b
    return (S_f, S_b), y.reshape(bsz, length, W_RWKV).astype(wd_f.dtype)


def merge_branches(h_res, mod, y_mla, y_conv, y_rwkv, g_mla, g_conv, g_rwkv, gate_logits, p):
    _, _, ada_gate = mod
    br_mla = (y_mla * jax.nn.silu(g_mla)) @ p['w_br_mla']
    br_conv = (y_conv * jax.nn.silu(g_conv)) @ p['w_br_conv']
    br_rwkv = (y_rwkv * jax.nn.silu(g_rwkv)) @ p['w_br_rwkv']
    s_mla, s_conv, s_rwkv = jnp.split(jax.nn.sigmoid(gate_logits), 3, axis=-1)
    merged = s_mla * br_mla + s_conv * br_conv + s_rwkv * br_rwkv
    return h_res + ada_gate * rms_norm(merged @ p['w_out'], p['g_post'])


def trunk_layer(x, ctx, c, c_ctx, p, rope, update_ctx):
    mod_x = ada_mod(c, p['w_mod'], p['b_mod'])
    mod_c = ada_mod(c_ctx, p['w_mod'], p['b_mod'])
    (q_lat_x, kv_lat_x, kr_x, g_mla_x, cv_in_x, cv_b_x, cv_c_x, g_conv_x,
     r_x, k_x, v_x, wdf_x, wdb_x, adf_x, adb_x, g_rwkv_x, gl_x) = project_in(x, mod_x, p['g_pre'], p['w_in'])
    (q_lat_c, kv_lat_c, kr_c, g_mla_c, cv_in_c, cv_b_c, cv_c_c, g_conv_c,
     r_c, k_c, v_c, wdf_c, wdb_c, adf_c, adb_c, g_rwkv_c, gl_c) = project_in(ctx, mod_c, p['g_pre'], p['w_in'])

    key_c, val_c = mla_keys_values(kv_lat_c, kr_c, p['g_kv'], p['w_ukv'], None)
    bsz = ctx.shape[0]
    zero_state = jnp.zeros((bsz, RWKV_HEADS, RWKV_HEAD, RWKV_HEAD), jnp.float32)
    ctx_states, y_rwkv_c = rwkv_branch((r_c, k_c, v_c, wdf_c, wdb_c, adf_c, adb_c), p,
                                       (zero_state, zero_state), emit=update_ctx)

    q_x = mla_queries(q_lat_x, p['g_q'], p['w_uq'], rope)
    key_x, val_x = mla_keys_values(kv_lat_x, kr_x, p['g_kv'], p['w_ukv'], rope)
    y_mla_x = blocked_attention(q_x, jnp.concatenate([key_c, key_x], axis=1),
                                jnp.concatenate([val_c, val_x], axis=1))
    y_conv_x = short_conv(cv_in_x, cv_b_x, cv_c_x, p['conv_w'], p['conv_b'])
    _, y_rwkv_x = rwkv_branch((r_x, k_x, v_x, wdf_x, wdb_x, adf_x, adb_x), p, ctx_states, emit=True)
    x_new = merge_branches(x, mod_x, y_mla_x, y_conv_x, y_rwkv_x, g_mla_x, g_conv_x, g_rwkv_x, gl_x, p)
    if not update_ctx:
        return x_new, None

    q_c = mla_queries(q_lat_c, p['g_q'], p['w_uq'], None)
    y_mla_c = softmax_attention(q_c, key_c, val_c).reshape(bsz, ctx.shape[1], W_MLA).astype(ctx.dtype)
    y_conv_c = short_conv(cv_in_c, cv_b_c, cv_c_c, p['conv_w'], p['conv_b'])
    ctx_new = merge_branches(ctx, mod_c, y_mla_c, y_conv_c, y_rwkv_c, g_mla_c, g_conv_c, g_rwkv_c, gl_c, p)
    return x_new, ctx_new


def setup_inputs(seed: int = 0) -> dict:
    key = jax.random.key(seed)
    ks = jax.random.split(key, 32)

    def nrm(k, shape, scale):
        return jax.random.normal(k, shape, jnp.float32) * scale

    D = D_MODEL
    return {
        'x': nrm(ks[0], (BATCH, SEQ, D), 1.0),
        'c': nrm(ks[1], (BATCH, D), 1.0),
        'ctx': nrm(ks[2], (BATCH, CTX_LEN, D), 1.0),
        'c_ctx': nrm(ks[3], (D,), 1.0),
        'w_mod': nrm(ks[4], (DEPTH, D, 3 * D), D ** -0.5),
        'b_mod': nrm(ks[5], (DEPTH, 3 * D), 0.02),
        'g_pre': 1.0 + nrm(ks[6], (DEPTH, D), 0.05),
        'g_post': 1.0 + nrm(ks[7], (DEPTH, D), 0.05),
        'w_in': nrm(ks[8], (DEPTH, D, D_IN), D ** -0.5),
        'g_q': 1.0 + nrm(ks[9], (DEPTH, Q_LORA), 0.05),
        'g_kv': 1.0 + nrm(ks[10], (DEPTH, KV_LORA), 0.05),
        'w_uq': nrm(ks[11], (DEPTH, Q_LORA, N_HEADS_MLA * QK_HEAD), Q_LORA ** -0.5),
        'w_ukv': nrm(ks[12], (DEPTH, KV_LORA, N_HEADS_MLA * (QK_NOPE + V_HEAD)), KV_LORA ** -0.5),
        'conv_w': nrm(ks[13], (DEPTH, CONV_K, CONV_W), CONV_K ** -0.5),
        'conv_b': nrm(ks[14], (DEPTH, CONV_W), 0.02),
        'w0': jax.random.uniform(ks[15], (DEPTH, 2, W_RWKV), jnp.float32, minval=-4.0, maxval=0.0),
        'w_up': nrm(ks[16], (DEPTH, 2, W_LORA, W_RWKV), W_LORA ** -0.5),
        'a0': nrm(ks[17], (DEPTH, 2, W_RWKV), 0.1),
        'a_up': nrm(ks[18], (DEPTH, 2, A_LORA, W_RWKV), 0.5 * A_LORA ** -0.5),
        'k_k': 0.85 + nrm(ks[19], (DEPTH, W_RWKV), 0.1),
        'k_a': 1.0 + nrm(ks[20], (DEPTH, W_RWKV), 0.1),
        'r_k': nrm(ks[21], (DEPTH, RWKV_HEADS, RWKV_HEAD), 0.1),
        'gn_g': 1.0 + nrm(ks[22], (DEPTH, W_RWKV), 0.05),
        'gn_b': nrm(ks[23], (DEPTH, W_RWKV), 0.02),
        'w_br_mla': nrm(ks[24], (DEPTH, W_MLA, D), W_MLA ** -0.5),
        'w_br_conv': nrm(ks[25], (DEPTH, CONV_W, D), CONV_W ** -0.5),
        'w_br_rwkv': nrm(ks[26], (DEPTH, W_RWKV, D), W_RWKV ** -0.5),
        'w_out': nrm(ks[27], (DEPTH, D, D), D ** -0.5),
    }


def reference(x, c, ctx, c_ctx, w_mod, b_mod, g_pre, g_post, w_in, g_q, g_kv, w_uq, w_ukv,
              conv_w, conv_b, w0, w_up, a0, a_up, k_k, k_a, r_k, gn_g, gn_b,
              w_br_mla, w_br_conv, w_br_rwkv, w_out):
    rope = axial_rope_tables(x.shape[1])
    for i in range(DEPTH):
        p = dict(w_mod=w_mod[i], b_mod=b_mod[i], g_pre=g_pre[i], g_post=g_post[i], w_in=w_in[i],
                 g_q=g_q[i], g_kv=g_kv[i], w_uq=w_uq[i], w_ukv=w_ukv[i],
                 conv_w=conv_w[i], conv_b=conv_b[i], w0=w0[i], w_up=w_up[i], a0=a0[i], a_up=a_up[i],
                 k_k=k_k[i], k_a=k_a[i], r_k=r_k[i], gn_g=gn_g[i], gn_b=gn_b[i],
                 w_br_mla=w_br_mla[i], w_br_conv=w_br_conv[i], w_br_rwkv=w_br_rwkv[i], w_out=w_out[i])
        x, ctx = trunk_layer(x, ctx, c, c_ctx, p, rope, update_ctx=(i < DEPTH - 1))
    return x
```

```python
import functools
import math

import numpy as np
import jax
import jax.numpy as jnp
from jax import lax
from jax.experimental import pallas as pl
from jax.experimental.pallas import tpu as pltpu

F32 = jnp.float32
BF16 = jnp.bfloat16

D_MODEL = 1024
DEPTH = 4
GRID_W = 64
N_HEADS_MLA = 8
Q_LORA = 384
KV_LORA = 256
QK_NOPE = 64
QK_ROPE = 32
QK_HEAD = QK_NOPE + QK_ROPE
V_HEAD = 64
W_MLA = N_HEADS_MLA * V_HEAD
ROPE_THETA = 10000.0
ATTN_SCALE = QK_HEAD ** -0.5
CONV_W = 512
RWKV_HEADS = 8
RWKV_HEAD = 64
W_RWKV = RWKV_HEADS * RWKV_HEAD
W_LORA = 64
A_LORA = 64
RMS_EPS = 1e-6
GN_EPS = 64e-5
L2_EPS = 1e-12

LANES = 128
ROW_BLK = 256
CHUNK = 64
QUAD = 4 * RWKV_HEAD
HEAD_PAD = 128
VMEM_LIMIT = 48 * 1024 * 1024

PCOLS = 17 * 512
PB_GATE = 0
PB_RKV = 2
PB_GRWKV = 9
PB_GMLA = 10
PB_CVIN = 11
PB_CVB = 12
PB_CVC = 13
PB_GCONV = 14
PB_Q = 15
PB_KV = 16
PB_LORA128 = (16 * 512 + 256) // 128


def _cparams(sem, vmem=VMEM_LIMIT):
    return pltpu.CompilerParams(dimension_semantics=sem, vmem_limit_bytes=vmem)


def _dot(a, b):
    return jnp.dot(a, b, preferred_element_type=F32)


def _dot_nt(a, b):
    return lax.dot_general(a, b, (((1,), (1,)), ((), ())), preferred_element_type=F32)


def _dot_tn(a, b):
    return lax.dot_general(a, b, (((0,), (0,)), ((), ())), preferred_element_type=F32)


def _split2(x):
    hi = x.astype(BF16)
    lo = (x - hi.astype(F32)).astype(BF16)
    return hi, lo


def _split3(x):
    hi = x.astype(BF16)
    r1 = x - hi.astype(F32)
    mid = r1.astype(BF16)
    lo = (r1 - mid.astype(F32)).astype(BF16)
    return hi, mid, lo


def _dot_exact_rhs(x, m_bf16):
    hi, mid, lo = _split3(x)
    return _dot(hi, m_bf16) + _dot(mid, m_bf16) + _dot(lo, m_bf16)


def _dot_exact_lhs(m_bf16, x):
    hi, mid, lo = _split3(x)
    return _dot(m_bf16, hi) + _dot(m_bf16, mid) + _dot(m_bf16, lo)


def _dot_hi(a, b):
    ah, al = _split2(a)
    bh, bl = _split2(b)
    return _dot(ah, bh) + _dot(ah, bl) + _dot(al, bh)


def _sigmoid(x):
    return 1.0 / (1.0 + jnp.exp(-x))


def _silu(x):
    return x * _sigmoid(x)


def _rms(x, g):
    return x * lax.rsqrt(jnp.mean(x * x, axis=-1, keepdims=True) + RMS_EPS) * g


def _adaln_kernel(c_ref, w_ref, b_ref, o_ref):
    a = _silu(c_ref[...])
    o_ref[0] = _dot_hi(a, w_ref[0]) + b_ref[0]


def _adaln(cc, w_mod, b_mod):
    depth, d, d3 = w_mod.shape
    tn = 1024
    return pl.pallas_call(
        _adaln_kernel,
        out_shape=jax.ShapeDtypeStruct((depth, 8, d3), F32),
        grid=(depth, d3 // tn),
        in_specs=[
            pl.BlockSpec((8, d), lambda l, j: (0, 0)),
            pl.BlockSpec((1, d, tn), lambda l, j: (l, 0, j)),
            pl.BlockSpec((1, 1, tn), lambda l, j: (l, 0, j)),
        ],
        out_specs=pl.BlockSpec((1, 8, tn), lambda l, j: (l, 0, j)),
        compiler_params=_cparams(("parallel", "parallel")),
        name="adaln",
    )(cc, w_mod, b_mod.reshape(depth, 1, d3))


def _proj_kernel(z_ref, sc_ref, sh_ref, g_ref, w_ref, o_ref, h_ref, *, nsub):
    @pl.when(pl.program_id(1) == 0)
    def _():
        y = _rms(z_ref[...], g_ref[...])
        for s in range(nsub):
            rows = slice(s * ROW_BLK, (s + 1) * ROW_BLK)
            h_ref[rows, :] = (y[rows, :] * sc_ref[s] + sh_ref[s]).astype(BF16)

    o_ref[...] = _dot(h_ref[...], w_ref[...])


def _proj_in(z2, sc1, sh, g_pre, w):
    n, d = z2.shape
    tm = 1024 if n % 1024 == 0 else ROW_BLK
    tn = 512
    nsub = tm // ROW_BLK
    return pl.pallas_call(
        functools.partial(_proj_kernel, nsub=nsub),
        out_shape=jax.ShapeDtypeStruct((n, PCOLS), F32),
        grid=(n // tm, PCOLS // tn),
        in_specs=[
            pl.BlockSpec((tm, d), lambda i, j: (i, 0)),
            pl.BlockSpec((nsub, 1, d), lambda i, j: (i, 0, 0)),
            pl.BlockSpec((nsub, 1, d), lambda i, j: (i, 0, 0)),
            pl.BlockSpec((1, d), lambda i, j: (0, 0)),
            pl.BlockSpec((d, tn), lambda i, j: (0, j)),
        ],
        out_specs=pl.BlockSpec((tm, tn), lambda i, j: (i, j)),
        scratch_shapes=[pltpu.VMEM((tm, d), BF16)],
        compiler_params=_cparams(("parallel", "arbitrary")),
        name="proj_in",
    )(z2, sc1, sh, g_pre, w)


def _mla_prep_kernel(qb_ref, kb_ref, cos_ref, sin_ref, gq_ref, gkv_ref, wq_ref, wk_ref, wv_ref,
                     vone_ref, q_ref, kt_ref, v_ref):
    nh = N_HEADS_MLA
    hw = nh * HEAD_PAD
    qb = qb_ref[0]
    kb = kb_ref[0]
    cos8 = jnp.tile(cos_ref[...], (1, nh))
    sin8 = jnp.tile(sin_ref[...], (1, nh))
    qn = _rms(qb[:, :Q_LORA], gq_ref[...]).astype(BF16)
    qq = _dot(qn, wq_ref[...])
    q_ref[0] = ((qq[:, :hw] * cos8 + qq[:, hw:] * sin8) * ATTN_SCALE).astype(BF16)
    kvn = _rms(kb[:, :KV_LORA], gkv_ref[...]).astype(BF16)
    kin = jnp.concatenate([kvn, qb[:, Q_LORA:].astype(BF16)], axis=1)
    kk = _dot(kin, wk_ref[...])
    k = kk[:, :hw] * cos8 + kk[:, hw:] * sin8
    for h in range(nh):
        kt_ref[0, h] = k[:, h * HEAD_PAD:(h + 1) * HEAD_PAD].T.astype(BF16)
    v_ref[0] = (_dot(kvn, wv_ref[...]) + vone_ref[...]).astype(BF16)


def _mla_prep(p3, cos_t, sin_t, g_q, g_kv, wq, wk, wv, vone):
    b, t, _ = p3.shape
    tm = ROW_BLK
    hw = N_HEADS_MLA * HEAD_PAD
    const = lambda shape: pl.BlockSpec(shape, lambda bi, i: (0,) * len(shape))
    return pl.pallas_call(
        _mla_prep_kernel,
        out_shape=(
            jax.ShapeDtypeStruct((b, t, hw), BF16),
            jax.ShapeDtypeStruct((b, N_HEADS_MLA, HEAD_PAD, t), BF16),
            jax.ShapeDtypeStruct((b, t, hw), BF16),
        ),
        grid=(b, t // tm),
        in_specs=[
            pl.BlockSpec((1, tm, 512), lambda bi, i: (bi, i, PB_Q)),
            pl.BlockSpec((1, tm, 512), lambda bi, i: (bi, i, PB_KV)),
            pl.BlockSpec((tm, HEAD_PAD), lambda bi, i: (i, 0)),
            pl.BlockSpec((tm, HEAD_PAD), lambda bi, i: (i, 0)),
            const((1, Q_LORA)),
            const((1, KV_LORA)),
            const(wq.shape),
            const(wk.shape),
            const(wv.shape),
            const((1, hw)),
        ],
        out_specs=(
            pl.BlockSpec((1, tm, hw), lambda bi, i: (bi, i, 0)),
            pl.BlockSpec((1, N_HEADS_MLA, HEAD_PAD, tm), lambda bi, i: (bi, 0, 0, i)),
            pl.BlockSpec((1, tm, hw), lambda bi, i: (bi, i, 0)),
        ),
        compiler_params=_cparams(("parallel", "parallel")),
        name="mla_prep",
    )(p3, p3, cos_t, sin_t, g_q, g_kv, wq, wk, wv, vone)


def _attn_kernel(q_ref, kt_ref, v_ref, *rest, nk):
    o_ref, m_ref, acc_ref = rest[-3:]
    j = pl.program_id(2)

    @pl.when(j == 0)
    def _():
        m_ref[...] = jnp.full(m_ref.shape, -jnp.inf, F32)
        acc_ref[...] = jnp.zeros(acc_ref.shape, F32)

    for h in range(N_HEADS_MLA):
        lanes = slice(h * HEAD_PAD, (h + 1) * HEAD_PAD)
        s = _dot(q_ref[0, :, lanes], kt_ref[0, h])
        m_prev = m_ref[h]
        m_new = jnp.maximum(m_prev, jnp.max(s, axis=-1, keepdims=True))
        alpha = jnp.exp(m_prev - m_new)
        p = jnp.exp(s - m_new[:, :1])
        acc_ref[h] = alpha * acc_ref[h] + _dot(p.astype(BF16), v_ref[0, :, lanes])
        m_ref[h] = m_new

    @pl.when(j == nk - 1)
    def _():
        outs = []
        for h in range(N_HEADS_MLA):
            a = acc_ref[h]
            outs.append(a[:, :V_HEAD] * (1.0 / a[:, V_HEAD:V_HEAD + 1]))
        o_ref[0] = jnp.concatenate(outs, axis=1)


def _attention(q, kt, v, y_prev, *, tq, tk, nq, nk, q_off, k_off):
    b, t, hw = q.shape
    in_specs = [
        pl.BlockSpec((1, tq, hw), lambda bi, i, j: (bi, i + q_off, 0)),
        pl.BlockSpec((1, N_HEADS_MLA, HEAD_PAD, tk), lambda bi, i, j: (bi, 0, 0, j + k_off)),
        pl.BlockSpec((1, tk, hw), lambda bi, i, j: (bi, j + k_off, 0)),
    ]
    args = [q, kt, v]
    aliases = {}
    if y_prev is not None:
        in_specs.append(pl.BlockSpec(memory_space=pl.ANY))
        args.append(y_prev)
        aliases = {3: 0}
    return pl.pallas_call(
        functools.partial(_attn_kernel, nk=nk),
        out_shape=jax.ShapeDtypeStruct((b, t, W_MLA), F32),
        grid=(b, nq, nk),
        in_specs=in_specs,
        out_specs=pl.BlockSpec((1, tq, W_MLA), lambda bi, i, j: (bi, i + q_off, 0)),
        scratch_shapes=[
            pltpu.VMEM((N_HEADS_MLA, tq, LANES), F32),
            pltpu.VMEM((N_HEADS_MLA, tq, HEAD_PAD), F32),
        ],
        input_output_aliases=aliases,
        compiler_params=_cparams(("parallel", "parallel", "arbitrary")),
        name="attn_ctx" if y_prev is not None else "attn_x",
    )(*args)


def _bd4(x, m4):
    return (jnp.tile(x, (4, 1)) * m4).astype(BF16)


def _diag_blocks(full, m4):
    fm = full * m4
    return fm[0:64] + fm[64:128] + fm[128:192] + fm[192:256]


def _rwkv_prep_kernel(rkv_ref, lora_ref, w0_ref, a0_ref, wup_ref, aup_ref, kk_ref, ka_ref, rk_ref,
                      ones_ref, m4_ref, rh_ref, y0_ref, g_ref, h_ref, bonus_ref):
    d = pl.program_id(0)
    fwd = d == 0
    c = CHUNK
    w = W_RWKV
    rkv = rkv_ref[0]
    r, k, v = rkv[:, :w], rkv[:, w:2 * w], rkv[:, 2 * w:]
    lora = lora_ref[0]
    zw = w0_ref[0] + _dot(jnp.tanh(lora[:, :W_LORA]).astype(BF16), wup_ref[0])
    ell = -math.exp(-0.5) * _sigmoid(zw)
    a = _sigmoid(a0_ref[0] + _dot(lora[:, W_LORA:].astype(BF16), aup_ref[0]))
    ones_bd = ones_ref[...]
    kkr = k * kk_ref[...]
    kk = kkr * lax.rsqrt(_dot_exact_rhs(kkr * kkr, ones_bd) + L2_EPS)
    k_d = k * (1.0 + (a - 1.0) * ka_ref[...])
    bonus_ref[0, 0] = _dot_exact_rhs(r * k_d * rk_ref[...], ones_bd) * v

    sgn = jnp.where(fwd, 1, -1)
    ti = lax.broadcasted_iota(jnp.int32, (c, c), 0)
    si = lax.broadcasted_iota(jnp.int32, (c, c), 1)
    tri = jnp.where((si - ti) * sgn <= 0, 1.0, 0.0).astype(BF16)
    lc = _dot_exact_lhs(tri, ell)
    ltot = jnp.where(fwd, lc[c - 1:c, :], lc[0:1, :])
    e_neg = jnp.exp(-lc)
    e_tail = jnp.exp(ltot - lc)
    gam = jnp.exp(ltot)
    kka = kk * a
    abar = kk * jnp.exp(lc - ell)
    bbar = kka * e_neg
    kbar = k_d * e_neg
    rbar = r * jnp.exp(lc)
    btil = kka * e_tail
    ktil = k_d * e_tail

    m4 = m4_ref[...]
    tq = lax.broadcasted_iota(jnp.int32, (c, QUAD), 0)
    sq = lax.broadcasted_iota(jnp.int32, (c, QUAD), 1) % c
    before = (sq - tq) * sgn < 0
    upto = (sq - tq) * sgn <= 0
    eye = jnp.where(sq == tq, 1.0, 0.0)

    def pm(x, y):
        return _dot(x.astype(BF16), _bd4(y, m4))

    for qd in range(w // QUAD):
        ls = slice(qd * QUAD, (qd + 1) * QUAD)
        ab, bb, kb, rb = abar[:, ls], bbar[:, ls], kbar[:, ls], rbar[:, ls]
        bt, kt, vq = btil[:, ls], ktil[:, ls], v[:, ls]
        la = jnp.concatenate([ab, rb], axis=0).astype(BF16)
        nb = _dot_nt(la, _bd4(bb, m4))
        nk = _dot_nt(la, _bd4(kb, m4))
        n = jnp.where(before, nb[:c], 0.0)
        a_rb = jnp.where(upto, nb[c:], 0.0)
        a_ak = jnp.where(before, nk[:c], 0.0)
        a_rk = jnp.where(upto, nk[c:], 0.0)
        x = eye - n
        p = pm(n, n)
        for it in range(5):
            if it < 4:
                xp = pm(jnp.concatenate([x, p], axis=0), p)
                x = x + xp[:c]
                p = xp[c:]
            else:
                x = x + pm(x, p)
        py = pm(jnp.concatenate([a_ak, a_rk], axis=0), vq)
        p1, yk = py[:c], py[c:]
        ah = pm(x, ab)
        u0 = pm(x, p1)
        rh_ref[0, 0, :, ls] = (rb - pm(a_rb, ah)).astype(BF16)
        y0_ref[0, 0, :, ls] = yk - pm(a_rb, u0)
        gfull = _dot_tn(bt.astype(BF16), ah.astype(BF16))
        g_ref[0, 0, :, ls] = eye * gam[:, ls] - _diag_blocks(gfull, m4)
        hl = jnp.concatenate([kt, -bt], axis=0).astype(BF16)
        hr = jnp.concatenate([vq, u0], axis=0).astype(BF16)
        h_ref[0, 0, :, ls] = _diag_blocks(_dot_tn(hl, hr), m4)


def _rwkv_prep(p3, w0, a0, w_up, a_up, k_k, k_a, r_k, ones_bd, m4):
    b, t, _ = p3.shape
    w = W_RWKV
    c = CHUNK
    const = lambda shape: pl.BlockSpec(shape, lambda d, bi, i: (0,) * len(shape))
    perdir = lambda shape: pl.BlockSpec((1,) + shape, lambda d, bi, i: (d,) + (0,) * len(shape))
    out_spec = pl.BlockSpec((1, 1, c, w), lambda d, bi, i: (d, bi, i, 0))
    f32_out = jax.ShapeDtypeStruct((2, b, t, w), F32)
    return pl.pallas_call(
        _rwkv_prep_kernel,
        out_shape=(jax.ShapeDtypeStruct((2, b, t, w), BF16), f32_out, f32_out, f32_out, f32_out),
        grid=(2, b, t // c),
        in_specs=[
            pl.BlockSpec((1, c, 3 * w), lambda d, bi, i: (bi, i, PB_RKV)),
            pl.BlockSpec((1, c, 2 * W_LORA), lambda d, bi, i: (bi, i, PB_LORA128 + d)),
            perdir((1, w)), perdir((1, w)), perdir((W_LORA, w)), perdir((A_LORA, w)),
            const((1, w)), const((1, w)), const((1, w)),
            const((w, w)), const((QUAD, QUAD)),
        ],
        out_specs=(out_spec,) * 5,
        compiler_params=_cparams(("parallel", "parallel", "parallel")),
        name="rwkv_prep",
    )(p3, p3, w0, a0, w_up, a_up, k_k, k_a, r_k, ones_bd, m4)


def _rwkv_scan_kernel(rhf, y0f, gf, hf, rhb, y0b, gb, hb, m4_ref, yf_ref, yb_ref, st_ref, *, nb):
    @pl.when(pl.program_id(0) == 0)
    def _():
        st_ref[...] = jnp.zeros(st_ref.shape, F32)

    m4 = m4_ref[...]
    c = CHUNK
    for d, (rh, y0, g, hh, y_ref) in enumerate(((rhf, y0f, gf, hf, yf_ref), (rhb, y0b, gb, hb, yb_ref))):
        for bi in range(nb):
            for qd in range(W_RWKV // QUAD):
                ls = slice(qd * QUAD, (qd + 1) * QUAD)
                lhs = jnp.concatenate([rh[0, bi, :, ls], _bd4(g[0, bi, :, ls], m4)], axis=0)
                st = st_ref[d, bi, qd]
                res = _dot(lhs, st.astype(BF16))
                y_ref[bi, :, ls] = res[:c] + y0[0, bi, :, ls]
                st_ref[d, bi, qd] = res[c:] + jnp.tile(hh[0, bi, :, ls], (4, 1)) * m4


def _rwkv_scan(rh, y0, g, h, m4, *, n_x, n_ctx):
    _, b, t, w = rh.shape
    c = CHUNK
    n = n_x + n_ctx
    fidx = lambda j: jnp.where(j < n_ctx, n_x + j, j - n_ctx)
    bidx = lambda j: n - 1 - j
    fspec = pl.BlockSpec((1, b, c, w), lambda j: (0, 0, fidx(j), 0))
    bspec = pl.BlockSpec((1, b, c, w), lambda j: (1, 0, bidx(j), 0))
    y_shape = jax.ShapeDtypeStruct((b, t, w), F32)
    return pl.pallas_call(
        functools.partial(_rwkv_scan_kernel, nb=b),
        out_shape=(y_shape, y_shape),
        grid=(n,),
        in_specs=[fspec] * 4 + [bspec] * 4 + [pl.BlockSpec((QUAD, QUAD), lambda j: (0, 0))],
        out_specs=(
            pl.BlockSpec((b, c, w), lambda j: (0, fidx(j), 0)),
            pl.BlockSpec((b, c, w), lambda j: (0, bidx(j), 0)),
        ),
        scratch_shapes=[pltpu.VMEM((2, b, w // QUAD, QUAD, QUAD), F32)],
        compiler_params=_cparams(("arbitrary",)),
        name="rwkv_scan",
    )(rh, y0, g, h, rh, y0, g, h, m4)


def _merge_kernel(z_ref, gate_ref, gl_ref, grw_ref, gml_ref, cin_ref, cb_ref, cc_ref, gcv_ref,
                  cin_p, cc_p, cin_n, cc_n, ymla_ref, yf_ref, yb_ref, bon_ref,
                  wbm_ref, wbc_ref, wbr_ref, wout_ref, gpost_ref, cw_ref, cbias_ref, gng_ref, gnb_ref,
                  avg_ref, o_ref, *, tiles_per_batch, ctx_tiles):
    tm = z_ref.shape[0]
    d = D_MODEL
    pos = pl.program_id(0) % tiles_per_batch
    x_tiles = tiles_per_batch - ctx_tiles
    first = jnp.logical_or(pos == 0, pos == x_tiles)
    last = jnp.logical_or(pos == x_tiles - 1, pos == tiles_per_batch - 1)

    u = cc_ref[...] * cin_ref[...]
    u_halo_p = jnp.where(first, 0.0, cc_p[7:8, :] * cin_p[7:8, :])
    u_halo_n = jnp.where(last, 0.0, cc_n[0:1, :] * cin_n[0:1, :])
    row = lax.broadcasted_iota(jnp.int32, u.shape, 0)
    u_prev = jnp.where(row == 0, u_halo_p, pltpu.roll(u, 1, axis=0))
    u_next = jnp.where(row == tm - 1, u_halo_n, pltpu.roll(u, tm - 1, axis=0))
    cw = cw_ref[...]
    y_conv = cb_ref[...] * (u_prev * cw[0:1] + u * cw[1:2] + u_next * cw[2:3] + cbias_ref[...])

    avg = avg_ref[...]
    yr = yf_ref[...] + yb_ref[...]
    mu = _dot_exact_rhs(yr, avg)
    dv = yr - mu
    var = _dot_exact_rhs(dv * dv, avg)
    y_rwkv = dv * lax.rsqrt(var + GN_EPS) * gng_ref[...] + gnb_ref[...] + bon_ref[0] + bon_ref[1]

    br_mla = _dot((ymla_ref[...] * _silu(gml_ref[...])).astype(BF16), wbm_ref[...])
    br_conv = _dot((y_conv * _silu(gcv_ref[...])).astype(BF16), wbc_ref[...])
    br_rwkv = _dot((y_rwkv * _silu(grw_ref[...])).astype(BF16), wbr_ref[...])
    s = _sigmoid(gl_ref[...])
    merged = s[:, :d] * br_mla + s[:, d:2 * d] * br_conv + s[:, 2 * d:] * br_rwkv
    o = _dot(merged.astype(BF16), wout_ref[...])
    o_ref[...] = z_ref[...] + gate_ref[0] * _rms(o, gpost_ref[...])


def _merge(z2, gate_tab, p2, y_mla, yf, yb, bonus, wbm, wbc, wbr, wout, g_post, conv_w, conv_b,
           gn_g, gn_b, avg_bd, *, tiles_per_batch, ctx_tiles):
    n, d = z2.shape
    tm = ROW_BLK
    hb = tm // 8
    nb8 = n // 8
    pcol = lambda blk: pl.BlockSpec((tm, 512), lambda i: (i, blk))
    prev = lambda blk: pl.BlockSpec((8, 512), lambda i: (jnp.maximum(i * hb - 1, 0), blk))
    nxt = lambda blk: pl.BlockSpec((8, 512), lambda i: (jnp.minimum((i + 1) * hb, nb8 - 1), blk))
    const = lambda shape: pl.BlockSpec(shape, lambda i: (0,) * len(shape))
    row512 = pl.BlockSpec((tm, 512), lambda i: (i, 0))
    return pl.pallas_call(
        functools.partial(_merge_kernel, tiles_per_batch=tiles_per_batch, ctx_tiles=ctx_tiles),
        out_shape=jax.ShapeDtypeStruct((n, d), F32),
        grid=(n // tm,),
        in_specs=[
            pl.BlockSpec((tm, d), lambda i: (i, 0)),
            pl.BlockSpec((1, 1, d), lambda i: (i, 0, 0)),
            pl.BlockSpec((tm, 3 * d), lambda i: (i, PB_GATE)),
            pcol(PB_GRWKV), pcol(PB_GMLA), pcol(PB_CVIN), pcol(PB_CVB), pcol(PB_CVC), pcol(PB_GCONV),
            prev(PB_CVIN), prev(PB_CVC), nxt(PB_CVIN), nxt(PB_CVC),
            row512, row512, row512,
            pl.BlockSpec((2, tm, 512), lambda i: (0, i, 0)),
            const(wbm.shape), const(wbc.shape), const(wbr.shape), const(wout.shape),
            const((1, d)), const(conv_w.shape), const((1, 512)), const((1, 512)), const((1, 512)),
            const(avg_bd.shape),
        ],
        out_specs=pl.BlockSpec((tm, d), lambda i: (i, 0)),
        compiler_params=_cparams(("parallel",)),
        name="merge",
    )(z2, gate_tab, p2, p2, p2, p2, p2, p2, p2, p2, p2, p2, p2, y_mla, yf, yb, bonus,
      wbm, wbc, wbr, wout, g_post, conv_w, conv_b, gn_g, gn_b, avg_bd)


def _pair_swap(w):
    s = w.shape
    return w.reshape(s[:-1] + (s[-1] // 2, 2))[..., ::-1].reshape(s)


def _layout_w_in(w_in):
    sizes = (Q_LORA, KV_LORA, QK_ROPE, W_MLA, CONV_W, CONV_W, CONV_W, CONV_W, W_RWKV, W_RWKV, W_RWKV,
             W_LORA, W_LORA, A_LORA, A_LORA, W_RWKV, 3 * D_MODEL)
    offs = np.concatenate([[0], np.cumsum(sizes)])
    names = ("q_lat", "kv_lat", "kr", "g_mla", "cv_in", "cv_b", "cv_c", "g_conv", "r", "k", "v",
             "wd_f", "wd_b", "ad_f", "ad_b", "g_rwkv", "gl")
    col = {nm: w_in[..., offs[i]:offs[i + 1]] for i, nm in enumerate(names)}
    zeros = jnp.zeros(w_in.shape[:-1] + (512 - Q_LORA - 2 * QK_ROPE,), w_in.dtype)
    parts = [col["gl"], col["r"], col["k"], col["v"], col["g_rwkv"], col["g_mla"], col["cv_in"],
             col["cv_b"], col["cv_c"], col["g_conv"],
             col["q_lat"], col["kr"], _pair_swap(col["kr"]), zeros,
             col["kv_lat"], col["wd_f"], col["ad_f"], col["wd_b"], col["ad_b"]]
    out = jnp.concatenate(parts, axis=-1).astype(BF16)
    assert out.shape[-1] == PCOLS
    return out


def _layout_mla_weights(w_uq, w_ukv):
    depth = w_uq.shape[0]
    nh = N_HEADS_MLA
    wq = w_uq.reshape(depth, Q_LORA, nh, QK_HEAD)
    q_nope, q_rope = wq[..., :QK_NOPE], wq[..., QK_NOPE:]
    zq = jnp.zeros((depth, Q_LORA, nh, HEAD_PAD - QK_HEAD), w_uq.dtype)
    wq_a = jnp.concatenate([q_nope, q_rope, zq], axis=-1)
    wq_b = jnp.concatenate([jnp.zeros_like(q_nope), _pair_swap(q_rope), zq], axis=-1)
    wq_all = jnp.concatenate([wq_a.reshape(depth, Q_LORA, nh * HEAD_PAD),
                              wq_b.reshape(depth, Q_LORA, nh * HEAD_PAD)], axis=-1).astype(BF16)

    wkv = w_ukv.reshape(depth, KV_LORA, nh, QK_NOPE + V_HEAD)
    k_nope, v_w = wkv[..., :QK_NOPE], wkv[..., QK_NOPE:]
    zk = jnp.zeros((depth, KV_LORA, nh, HEAD_PAD - QK_NOPE), w_ukv.dtype)
    wk_top = jnp.concatenate([k_nope, zk], axis=-1).reshape(depth, KV_LORA, nh * HEAD_PAD)
    wv = jnp.concatenate([v_w, zk], axis=-1).reshape(depth, KV_LORA, nh * HEAD_PAD).astype(BF16)
    place = np.zeros((HEAD_PAD, HEAD_PAD), np.float32)
    place[np.arange(QK_ROPE), QK_NOPE + np.arange(QK_ROPE)] = 1.0
    e_a = np.tile(place, (1, nh))
    place_b = np.zeros((HEAD_PAD, HEAD_PAD), np.float32)
    place_b[QK_ROPE + np.arange(QK_ROPE), QK_NOPE + np.arange(QK_ROPE)] = 1.0
    e_b = np.tile(place_b, (1, nh))
    top = jnp.concatenate([wk_top, jnp.zeros_like(wk_top)], axis=-1)
    bot = jnp.broadcast_to(jnp.asarray(np.concatenate([e_a, e_b], axis=1)), (depth, HEAD_PAD, 2 * nh * HEAD_PAD))
    wk_all = jnp.concatenate([top, bot.astype(top.dtype)], axis=1).astype(BF16)
    return wq_all, wk_all, wv


def _rope_tables(seq, ctx_len):
    n_freq = QK_ROPE // 4
    pos = np.arange(seq)
    inv = ROPE_THETA ** (-np.arange(n_freq, dtype=np.float32) / n_freq)
    row = (pos // GRID_W).astype(np.float32)
    colp = (pos % GRID_W).astype(np.float32)
    ang = jnp.concatenate([jnp.asarray(row)[:, None] * jnp.asarray(inv), jnp.asarray(colp)[:, None] * jnp.asarray(inv)], axis=-1)
    cos, sin = jnp.cos(ang), jnp.sin(ang)
    cos2 = jnp.repeat(cos, 2, axis=-1)
    sin2 = jnp.stack([-sin, sin], axis=-1).reshape(seq, QK_ROPE)
    ones = jnp.ones((seq, QK_NOPE), F32)
    pad = jnp.zeros((seq, HEAD_PAD - QK_HEAD), F32)
    cos_x = jnp.concatenate([ones, cos2, pad], axis=-1)
    sin_x = jnp.concatenate([jnp.zeros_like(ones), sin2, pad], axis=-1)
    cos_c = jnp.concatenate([jnp.ones((ctx_len, QK_HEAD), F32), jnp.zeros((ctx_len, HEAD_PAD - QK_HEAD), F32)], axis=-1)
    sin_c = jnp.zeros((ctx_len, HEAD_PAD), F32)
    return jnp.concatenate([cos_x, cos_c], axis=0), jnp.concatenate([sin_x, sin_c], axis=0)


def _block_diag_const(n, blk, value):
    i = np.arange(n)
    return np.where((i[:, None] // blk) == (i[None, :] // blk), value, 0.0).astype(np.float32)


def _pick(n, candidates):
    for cand in candidates:
        if n % cand == 0:
            return cand
    raise ValueError(f"no tile for {n}")


def kernel(x, c, ctx, c_ctx, w_mod, b_mod, g_pre, g_post, w_in, g_q, g_kv, w_uq, w_ukv, conv_w, conv_b,
           w0, w_up, a0, a_up, k_k, k_a, r_k, gn_g, gn_b, w_br_mla, w_br_conv, w_br_rwkv, w_out):
    bsz, seq, d = x.shape
    ctx_len = ctx.shape[1]
    depth = w_mod.shape[0]
    assert d == D_MODEL and ctx_len == ROW_BLK and seq % ROW_BLK == 0
    t = seq + ctx_len
    n = bsz * t
    tiles_per_batch = t // ROW_BLK

    cc = jnp.zeros((8, d), F32).at[0].set(c_ctx).at[1:1 + bsz].set(c)
    mods = _adaln(cc, w_mod, b_mod)
    seg = np.array([0 if p == tiles_per_batch - 1 else 1 + b for b in range(bsz) for p in range(tiles_per_batch)])
    shift_tab = mods[:, seg, None, :d]
    scale1_tab = 1.0 + mods[:, seg, None, d:2 * d]
    gate_tab = mods[:, seg, None, 2 * d:]

    w_in_p = _layout_w_in(w_in)
    wq_all, wk_all, wv_all = _layout_mla_weights(w_uq, w_ukv)
    cos_t, sin_t = _rope_tables(seq, ctx_len)
    vone = np.zeros((1, N_HEADS_MLA * HEAD_PAD), np.float32)
    vone[0, V_HEAD::HEAD_PAD] = 1.0
    vone = jnp.asarray(vone)
    ones_bd = jnp.asarray(_block_diag_const(W_RWKV, RWKV_HEAD, 1.0), BF16)
    avg_bd = jnp.asarray(_block_diag_const(W_RWKV, RWKV_HEAD, 1.0 / RWKV_HEAD), BF16)
    m4 = jnp.asarray(_block_diag_const(QUAD, RWKV_HEAD, 1.0), F32)

    tq = _pick(seq, (512, 256))
    tk = _pick(t, (768, 512, 256))

    z = jnp.concatenate([x, ctx], axis=1)
    for l in range(depth):
        z2 = z.reshape(n, d)
        p2 = _proj_in(z2, scale1_tab[l], shift_tab[l], g_pre[l][None], w_in_p[l])
        p3 = p2.reshape(bsz, t, PCOLS)

        q, kt, v = _mla_prep(p3, cos_t, sin_t, g_q[l][None], g_kv[l][None], wq_all[l], wk_all[l], wv_all[l], vone)
        y_mla = _attention(q, kt, v, None, tq=tq, tk=tk, nq=seq // tq, nk=t // tk, q_off=0, k_off=0)
        y_mla = _attention(q, kt, v, y_mla, tq=ctx_len, tk=ctx_len, nq=1, nk=1,
                           q_off=seq // ctx_len, k_off=seq // ctx_len)

        rh, y0, g, h, bonus = _rwkv_prep(
            p3, w0[l][:, None], a0[l][:, None], w_up[l].astype(BF16), a_up[l].astype(BF16),
            k_k[l][None], k_a[l][None], r_k[l].reshape(1, W_RWKV), ones_bd, m4)
        yf, yb = _rwkv_scan(rh, y0, g, h, m4, n_x=seq // CHUNK, n_ctx=ctx_len // CHUNK)

        z2 = _merge(z2, gate_tab[l], p2, y_mla.reshape(n, W_MLA), yf.reshape(n, W_RWKV), yb.reshape(n, W_RWKV),
                    bonus.reshape(2, n, W_RWKV),
                    w_br_mla[l].astype(BF16), w_br_conv[l].astype(BF16), w_br_rwkv[l].astype(BF16),
                    w_out[l].astype(BF16), g_post[l][None], conv_w[l], conv_b[l][None],
                    gn_g[l][None], gn_b[l][None], avg_bd,
                    tiles_per_batch=tiles_per_batch, ctx_tiles=ctx_len // ROW_BLK)
        z = z2.reshape(bsz, t, d)
    return z[:, :seq]
```

```python
import functools
import math

import numpy as np
import jax
import jax.numpy as jnp
from jax import lax
from jax.experimental import pallas as pl
from jax.experimental.pallas import tpu as pltpu

F32 = jnp.float32
BF16 = jnp.bfloat16

D_MODEL = 1024
DEPTH = 4
GRID_W = 64
N_HEADS_MLA = 8
Q_LORA = 384
KV_LORA = 256
QK_NOPE = 64
QK_ROPE = 32
QK_HEAD = QK_NOPE + QK_ROPE
V_HEAD = 64
W_MLA = N_HEADS_MLA * V_HEAD
ROPE_THETA = 10000.0
ATTN_SCALE = QK_HEAD ** -0.5
CONV_W = 512
RWKV_HEADS = 8
RWKV_HEAD = 64
W_RWKV = RWKV_HEADS * RWKV_HEAD
W_LORA = 64
A_LORA = 64
RMS_EPS = 1e-6
GN_EPS = 64e-5
L2_EPS = 1e-12
LOG2E = math.log2(math.e)

LANES = 128
ROW_BLK = 256
CHUNK = 64
QUAD = 4 * RWKV_HEAD
PREP_ROWS = 256
HEAD_PAD = 128
VMEM_LIMIT = 48 * 1024 * 1024

PCOLS = 17 * 512
PB_GATE = 0
PB_RKV = 2
PB_GRWKV = 9
PB_GMLA = 10
PB_CVIN = 11
PB_CVB = 12
PB_CVC = 13
PB_GCONV = 14
PB_Q = 15
PB_KV = 16
PB_LORA128 = (16 * 512 + 256) // 128


def _cparams(sem, vmem=VMEM_LIMIT):
    return pltpu.CompilerParams(dimension_semantics=sem, vmem_limit_bytes=vmem)


def _dot(a, b):
    return jnp.dot(a, b, preferred_element_type=F32)


def _dot_nt(a, b):
    return lax.dot_general(a, b, (((1,), (1,)), ((), ())), preferred_element_type=F32)


def _dot_tn(a, b):
    return lax.dot_general(a, b, (((0,), (0,)), ((), ())), preferred_element_type=F32)


def _split2(x):
    hi = x.astype(BF16)
    lo = (x - hi.astype(F32)).astype(BF16)
    return hi, lo


def _split3(x):
    hi = x.astype(BF16)
    r1 = x - hi.astype(F32)
    mid = r1.astype(BF16)
    lo = (r1 - mid.astype(F32)).astype(BF16)
    return hi, mid, lo


def _dot_exact_rhs(x, m_bf16):
    hi, mid, lo = _split3(x)
    return _dot(hi, m_bf16) + _dot(mid, m_bf16) + _dot(lo, m_bf16)


def _dot_hi(a, b):
    ah, al = _split2(a)
    bh, bl = _split2(b)
    return _dot(ah, bh) + _dot(ah, bl) + _dot(al, bh)


def _sigmoid(x):
    return 1.0 / (1.0 + jnp.exp(-x))


def _silu(x):
    return x * _sigmoid(x)


def _rms(x, g):
    return x * lax.rsqrt(jnp.mean(x * x, axis=-1, keepdims=True) + RMS_EPS) * g


def _adaln_kernel(c_ref, w_ref, b_ref, o_ref):
    a = _silu(c_ref[...])
    o_ref[0] = _dot_hi(a, w_ref[0]) + b_ref[0]


def _adaln(cc, w_mod, b_mod):
    depth, d, d3 = w_mod.shape
    tn = 1024
    return pl.pallas_call(
        _adaln_kernel,
        out_shape=jax.ShapeDtypeStruct((depth, 8, d3), F32),
        grid=(depth, d3 // tn),
        in_specs=[
            pl.BlockSpec((8, d), lambda l, j: (0, 0)),
            pl.BlockSpec((1, d, tn), lambda l, j: (l, 0, j)),
            pl.BlockSpec((1, 1, tn), lambda l, j: (l, 0, j)),
        ],
        out_specs=pl.BlockSpec((1, 8, tn), lambda l, j: (l, 0, j)),
        compiler_params=_cparams(("parallel", "parallel")),
        name="adaln",
    )(cc, w_mod, b_mod.reshape(depth, 1, d3))


def _seg_row(blk, tiles_per_batch):
    return jnp.where(blk % tiles_per_batch == tiles_per_batch - 1, 0, 1 + blk // tiles_per_batch)


def _proj_kernel(z_ref, *refs, nsub):
    mod_refs = refs[:2 * nsub]
    g_ref, w_ref, o_ref, h_ref = refs[2 * nsub:]

    @pl.when(pl.program_id(1) == 0)
    def _():
        y = _rms(z_ref[...], g_ref[...])
        for s in range(nsub):
            rows = slice(s * ROW_BLK, (s + 1) * ROW_BLK)
            shift, scale = mod_refs[2 * s][0], mod_refs[2 * s + 1][0]
            h_ref[rows, :] = (y[rows, :] * (1.0 + scale) + shift).astype(BF16)

    o_ref[...] = _dot(h_ref[...], w_ref[...])


def _proj_in(z2, mods, g_pre, w, *, tiles_per_batch):
    n, d = z2.shape
    tm = 1024 if n % 1024 == 0 else ROW_BLK
    tn = 512
    nsub = tm // ROW_BLK
    mod_specs = []
    for s in range(nsub):
        for col in (0, 1):
            mod_specs.append(pl.BlockSpec(
                (1, 1, d), lambda i, j, s=s, col=col: (_seg_row(i * nsub + s, tiles_per_batch), 0, col)))
    return pl.pallas_call(
        functools.partial(_proj_kernel, nsub=nsub),
        out_shape=jax.ShapeDtypeStruct((n, PCOLS), F32),
        grid=(n // tm, PCOLS // tn),
        in_specs=[pl.BlockSpec((tm, d), lambda i, j: (i, 0))] + mod_specs + [
            pl.BlockSpec((1, d), lambda i, j: (0, 0)),
            pl.BlockSpec((d, tn), lambda i, j: (0, j)),
        ],
        out_specs=pl.BlockSpec((tm, tn), lambda i, j: (i, j)),
        scratch_shapes=[pltpu.VMEM((tm, d), BF16)],
        compiler_params=_cparams(("parallel", "arbitrary")),
        name="proj_in",
    )(z2, *([mods] * (2 * nsub)), g_pre, w)


def _mla_prep_kernel(qb_ref, kb_ref, cos_ref, sin_ref, gq_ref, gkv_ref, wq_ref, wk_ref, wv_ref,
                     vone_ref, q_ref, kt_ref, v_ref):
    nh = N_HEADS_MLA
    hw = nh * HEAD_PAD
    qb = qb_ref[0]
    kb = kb_ref[0]
    cos8 = jnp.tile(cos_ref[...], (1, nh))
    sin8 = jnp.tile(sin_ref[...], (1, nh))
    qn = _rms(qb[:, :Q_LORA], gq_ref[...]).astype(BF16)
    qq = _dot(qn, wq_ref[...])
    q_ref[0] = ((qq[:, :hw] * cos8 + qq[:, hw:] * sin8) * (ATTN_SCALE * LOG2E)).astype(BF16)
    kvn = _rms(kb[:, :KV_LORA], gkv_ref[...]).astype(BF16)
    kin = jnp.concatenate([kvn, qb[:, Q_LORA:].astype(BF16)], axis=1)
    kk = _dot(kin, wk_ref[...])
    k = kk[:, :hw] * cos8 + kk[:, hw:] * sin8
    for h in range(nh):
        kt_ref[0, h] = k[:, h * HEAD_PAD:(h + 1) * HEAD_PAD].T.astype(BF16)
    v_ref[0] = (_dot(kvn, wv_ref[...]) + vone_ref[...]).astype(BF16)


def _mla_prep(p3, cos_t, sin_t, g_q, g_kv, wq, wk, wv, vone):
    b, t, _ = p3.shape
    tm = ROW_BLK
    hw = N_HEADS_MLA * HEAD_PAD
    const = lambda shape: pl.BlockSpec(shape, lambda bi, i: (0,) * len(shape))
    return pl.pallas_call(
        _mla_prep_kernel,
        out_shape=(
            jax.ShapeDtypeStruct((b, t, hw), BF16),
            jax.ShapeDtypeStruct((b, N_HEADS_MLA, HEAD_PAD, t), BF16),
            jax.ShapeDtypeStruct((b, t, hw), BF16),
        ),
        grid=(b, t // tm),
        in_specs=[
            pl.BlockSpec((1, tm, 512), lambda bi, i: (bi, i, PB_Q)),
            pl.BlockSpec((1, tm, 512), lambda bi, i: (bi, i, PB_KV)),
            pl.BlockSpec((tm, HEAD_PAD), lambda bi, i: (i, 0)),
            pl.BlockSpec((tm, HEAD_PAD), lambda bi, i: (i, 0)),
            const((1, Q_LORA)),
            const((1, KV_LORA)),
            const(wq.shape),
            const(wk.shape),
            const(wv.shape),
            const((1, hw)),
        ],
        out_specs=(
            pl.BlockSpec((1, tm, hw), lambda bi, i: (bi, i, 0)),
            pl.BlockSpec((1, N_HEADS_MLA, HEAD_PAD, tm), lambda bi, i: (bi, 0, 0, i)),
            pl.BlockSpec((1, tm, hw), lambda bi, i: (bi, i, 0)),
        ),
        compiler_params=_cparams(("parallel", "parallel")),
        name="mla_prep",
    )(p3, p3, cos_t, sin_t, g_q, g_kv, wq, wk, wv, vone)


def _attn_kernel(q_ref, kt_ref, v_ref, *rest, nk):
    o_ref, m_ref, acc_ref = rest[-3:]
    j = pl.program_id(2)

    @pl.when(j == 0)
    def _():
        m_ref[...] = jnp.full(m_ref.shape, -jnp.inf, F32)
        acc_ref[...] = jnp.zeros(acc_ref.shape, F32)

    def scores(h):
        return _dot(q_ref[0, :, h * HEAD_PAD:(h + 1) * HEAD_PAD], kt_ref[0, h])

    s_next = scores(0)
    for h in range(N_HEADS_MLA):
        lanes = slice(h * HEAD_PAD, (h + 1) * HEAD_PAD)
        s = s_next
        if h + 1 < N_HEADS_MLA:
            s_next = scores(h + 1)
        m_prev = m_ref[h]
        m_new = jnp.maximum(m_prev, jnp.max(s, axis=-1, keepdims=True))
        alpha = jnp.exp2(m_prev - m_new)
        p = jnp.exp2(s - m_new[:, :1])
        acc_ref[h] = alpha * acc_ref[h] + _dot(p.astype(BF16), v_ref[0, :, lanes])
        m_ref[h] = m_new

    @pl.when(j == nk - 1)
    def _():
        outs = []
        for h in range(N_HEADS_MLA):
            a = acc_ref[h]
            outs.append(a[:, :V_HEAD] * (1.0 / a[:, V_HEAD:V_HEAD + 1]))
        o_ref[0] = jnp.concatenate(outs, axis=1)


def _attention(q, kt, v, y_prev, *, tq, tk, nq, nk, q_off, k_off):
    b, t, hw = q.shape
    in_specs = [
        pl.BlockSpec((1, tq, hw), lambda bi, i, j: (bi, i + q_off, 0)),
        pl.BlockSpec((1, N_HEADS_MLA, HEAD_PAD, tk), lambda bi, i, j: (bi, 0, 0, j + k_off)),
        pl.BlockSpec((1, tk, hw), lambda bi, i, j: (bi, j + k_off, 0)),
    ]
    args = [q, kt, v]
    aliases = {}
    if y_prev is not None:
        in_specs.append(pl.BlockSpec(memory_space=pl.ANY))
        args.append(y_prev)
        aliases = {3: 0}
    return pl.pallas_call(
        functools.partial(_attn_kernel, nk=nk),
        out_shape=jax.ShapeDtypeStruct((b, t, W_MLA), F32),
        grid=(b, nq, nk),
        in_specs=in_specs,
        out_specs=pl.BlockSpec((1, tq, W_MLA), lambda bi, i, j: (bi, i + q_off, 0)),
        scratch_shapes=[
            pltpu.VMEM((N_HEADS_MLA, tq, LANES), F32),
            pltpu.VMEM((N_HEADS_MLA, tq, HEAD_PAD), F32),
        ],
        input_output_aliases=aliases,
        compiler_params=_cparams(("parallel", "parallel", "arbitrary")),
        name="attn_ctx" if y_prev is not None else "attn_x",
    )(*args)


def _half_lane_masks():
    lane = lax.broadcasted_iota(jnp.int32, (CHUNK, LANES), 1)
    first = lane < RWKV_HEAD
    return first, jnp.where(first, 1.0, 0.0).astype(BF16), jnp.where(first, 0.0, 1.0).astype(BF16)


def _bd4(x, lo, hi):
    xb = x.astype(BF16)
    xl, xr = xb[:, :LANES], xb[:, LANES:]
    z = jnp.zeros((2 * CHUNK, LANES), BF16)
    c0 = jnp.concatenate([xl * lo, xl * hi, z], axis=0)
    c1 = jnp.concatenate([z, xr * lo, xr * hi], axis=0)
    return jnp.concatenate([c0, c1], axis=1)


def _bd4_f32(x, first):
    xl, xr = x[:, :LANES], x[:, LANES:]
    z = jnp.zeros((2 * CHUNK, LANES), F32)
    c0 = jnp.concatenate([jnp.where(first, xl, 0.0), jnp.where(first, 0.0, xl), z], axis=0)
    c1 = jnp.concatenate([z, jnp.where(first, xr, 0.0), jnp.where(first, 0.0, xr)], axis=0)
    return jnp.concatenate([c0, c1], axis=1)


def _diag_blocks(full, first):
    c = CHUNK
    left = jnp.where(first, full[0:c, :LANES], full[c:2 * c, :LANES])
    right = jnp.where(first, full[2 * c:3 * c, LANES:], full[3 * c:4 * c, LANES:])
    return jnp.concatenate([left, right], axis=1)


def _rwkv_prep_kernel(rkv_ref, lora_ref, w0_ref, a0_ref, wup_ref, aup_ref, kk_ref, ka_ref, rk_ref,
                      ones_ref, rh_ref, y0_ref, g_ref, h_ref, bonus_ref):
    fwd = pl.program_id(0) == 0
    c = CHUNK
    w = W_RWKV
    rows = rkv_ref.shape[1]
    rkv = rkv_ref[0]
    r, k, v = rkv[:, :w], rkv[:, w:2 * w], rkv[:, 2 * w:]
    lora = lora_ref[0]
    zw = w0_ref[0] + _dot(jnp.tanh(lora[:, :W_LORA]).astype(BF16), wup_ref[0])
    ell = -math.exp(-0.5) * _sigmoid(zw)
    a = _sigmoid(a0_ref[0] + _dot(lora[:, W_LORA:].astype(BF16), aup_ref[0]))
    ones4 = ones_ref[...]

    def head_sum(x):
        hi, lo = _split2(x)
        parts = []
        for qd in range(w // QUAD):
            ls = slice(qd * QUAD, (qd + 1) * QUAD)
            parts.append(_dot(hi[:, ls], ones4) + _dot(lo[:, ls], ones4))
        return jnp.concatenate(parts, axis=1)

    kkr = k * kk_ref[...]
    kk = kkr * lax.rsqrt(head_sum(kkr * kkr) + L2_EPS)
    k_d = k * (1.0 + (a - 1.0) * ka_ref[...])
    bonus_ref[0, 0] = head_sum(r * k_d * rk_ref[...]) * v

    sgn = jnp.where(fwd, 1, -1)
    ti = lax.broadcasted_iota(jnp.int32, (rows, rows), 0)
    si = lax.broadcasted_iota(jnp.int32, (rows, rows), 1)
    same = jnp.where((ti // c) == (si // c), 1.0, 0.0)
    tri = jnp.where((si - ti) * sgn <= 0, same, 0.0).astype(BF16)
    same = same.astype(BF16)
    ell_hi, ell_lo = _split2(ell)
    lc = _dot(tri, ell_hi) + _dot(tri, ell_lo)
    ltot = _dot(same, ell_hi) + _dot(same, ell_lo)
    e_neg = jnp.exp(-lc)
    e_tail = jnp.exp(ltot - lc)
    gam = jnp.exp(ltot)
    kka = kk * a
    abar = kk * jnp.exp(lc - ell)
    bbar = kka * e_neg
    kbar = k_d * e_neg
    rbar = r * jnp.exp(lc)
    btil = kka * e_tail
    ktil = k_d * e_tail

    first, lo, hi = _half_lane_masks()
    tq = lax.broadcasted_iota(jnp.int32, (c, QUAD), 0)
    sq = lax.broadcasted_iota(jnp.int32, (c, QUAD), 1) % c
    before = (sq - tq) * sgn < 0
    upto = (sq - tq) * sgn <= 0
    eye = jnp.where(sq == tq, 1.0, 0.0)

    def pm(x, y):
        return _dot(x.astype(BF16), _bd4(y, lo, hi))

    sl = [(slice(ch * c, (ch + 1) * c), slice(qd * QUAD, (qd + 1) * QUAD))
          for ch in range(rows // c) for qd in range(w // QUAD)]
    la = [jnp.concatenate([abar[s], rbar[s]], axis=0).astype(BF16) for s in sl]
    nb = [_dot_nt(la_i, _bd4(bbar[s], lo, hi)) for la_i, s in zip(la, sl)]
    nk = [_dot_nt(la_i, _bd4(kbar[s], lo, hi)) for la_i, s in zip(la, sl)]
    n = [jnp.where(before, t[:c], 0.0) for t in nb]
    a_rb = [jnp.where(upto, t[c:], 0.0) for t in nb]
    a_ak = [jnp.where(before, t[:c], 0.0) for t in nk]
    a_rk = [jnp.where(upto, t[c:], 0.0) for t in nk]
    py = [pm(jnp.concatenate([u, r_], axis=0), v[s]) for u, r_, s in zip(a_ak, a_rk, sl)]
    x = [eye - t for t in n]
    p = [pm(t, t) for t in n]
    for it in range(5):
        if it < 4:
            xp = [pm(jnp.concatenate([x_i, p_i], axis=0), p_i) for x_i, p_i in zip(x, p)]
            x = [x_i + t[:c] for x_i, t in zip(x, xp)]
            p = [t[c:] for t in xp]
        else:
            x = [x_i + pm(x_i, p_i) for x_i, p_i in zip(x, p)]
    ah = [pm(x_i, abar[s]) for x_i, s in zip(x, sl)]
    u0 = [pm(x_i, t[:c]) for x_i, t in zip(x, py)]
    ra = [pm(m_i, t) for m_i, t in zip(a_rb, ah)]
    ru = [pm(m_i, t) for m_i, t in zip(a_rb, u0)]
    gfull = [_dot_tn(btil[s].astype(BF16), t.astype(BF16)) for s, t in zip(sl, ah)]
    hfull = [_dot_tn(jnp.concatenate([ktil[s], -btil[s]], axis=0).astype(BF16),
                     jnp.concatenate([v[s], t], axis=0).astype(BF16)) for s, t in zip(sl, u0)]
    for i, (rs, ls) in enumerate(sl):
        rh_ref[0, 0, rs, ls] = (rbar[rs, ls] - ra[i]).astype(BF16)
        y0_ref[0, 0, rs, ls] = py[i][c:] - ru[i]
        g_ref[0, 0, rs, ls] = eye * gam[rs, ls] - _diag_blocks(gfull[i], first)
        h_ref[0, 0, rs, ls] = _diag_blocks(hfull[i], first)


def _rwkv_prep(p3, w0, a0, w_up, a_up, k_k, k_a, r_k, ones4):
    b, t, _ = p3.shape
    w = W_RWKV
    rows = PREP_ROWS
    const = lambda shape: pl.BlockSpec(shape, lambda d, bi, i: (0,) * len(shape))
    perdir = lambda shape: pl.BlockSpec((1,) + shape, lambda d, bi, i: (d,) + (0,) * len(shape))
    out_spec = pl.BlockSpec((1, 1, rows, w), lambda d, bi, i: (d, bi, i, 0))
    f32_out = jax.ShapeDtypeStruct((2, b, t, w), F32)
    return pl.pallas_call(
        _rwkv_prep_kernel,
        out_shape=(jax.ShapeDtypeStruct((2, b, t, w), BF16), f32_out, f32_out, f32_out, f32_out),
        grid=(2, b, t // rows),
        in_specs=[
            pl.BlockSpec((1, rows, 3 * w), lambda d, bi, i: (bi, i, PB_RKV)),
            pl.BlockSpec((1, rows, 2 * W_LORA), lambda d, bi, i: (bi, i, PB_LORA128 + d)),
            perdir((1, w)), perdir((1, w)), perdir((W_LORA, w)), perdir((A_LORA, w)),
            const((1, w)), const((1, w)), const((1, w)),
            const((QUAD, QUAD)),
        ],
        out_specs=(out_spec,) * 5,
        compiler_params=_cparams(("parallel", "parallel", "parallel")),
        name="rwkv_prep",
    )(p3, p3, w0, a0, w_up, a_up, k_k, k_a, r_k, ones4)


def _rwkv_scan_kernel(rhf, y0f, gf, hf, rhb, y0b, gb, hb, yf_ref, yb_ref, st_ref, *, nb):
    @pl.when(pl.program_id(0) == 0)
    def _():
        st_ref[...] = jnp.zeros(st_ref.shape, F32)

    first, lo, hi = _half_lane_masks()
    c = CHUNK
    for d, (rh, y0, g, hh, y_ref) in enumerate(((rhf, y0f, gf, hf, yf_ref), (rhb, y0b, gb, hb, yb_ref))):
        for bi in range(nb):
            for qd in range(W_RWKV // QUAD):
                ls = slice(qd * QUAD, (qd + 1) * QUAD)
                lhs = jnp.concatenate([rh[0, bi, :, ls], _bd4(g[0, bi, :, ls], lo, hi)], axis=0)
                res = _dot(lhs, st_ref[d, bi, qd].astype(BF16))
                y_ref[bi, :, ls] = res[:c] + y0[0, bi, :, ls]
                st_ref[d, bi, qd] = res[c:] + _bd4_f32(hh[0, bi, :, ls], first)


def _rwkv_scan(rh, y0, g, h, *, n_x, n_ctx):
    _, b, t, w = rh.shape
    c = CHUNK
    n = n_x + n_ctx
    fidx = lambda j: jnp.where(j < n_ctx, n_x + j, j - n_ctx)
    bidx = lambda j: n - 1 - j
    fspec = pl.BlockSpec((1, b, c, w), lambda j: (0, 0, fidx(j), 0))
    bspec = pl.BlockSpec((1, b, c, w), lambda j: (1, 0, bidx(j), 0))
    y_shape = jax.ShapeDtypeStruct((b, t, w), F32)
    return pl.pallas_call(
        functools.partial(_rwkv_scan_kernel, nb=b),
        out_shape=(y_shape, y_shape),
        grid=(n,),
        in_specs=[fspec] * 4 + [bspec] * 4,
        out_specs=(
            pl.BlockSpec((b, c, w), lambda j: (0, fidx(j), 0)),
            pl.BlockSpec((b, c, w), lambda j: (0, bidx(j), 0)),
        ),
        scratch_shapes=[pltpu.VMEM((2, b, w // QUAD, QUAD, QUAD), F32)],
        compiler_params=_cparams(("arbitrary",)),
        name="rwkv_scan",
    )(rh, y0, g, h, rh, y0, g, h)


def _merge_kernel(z_ref, gate_ref, gl_ref, grw_ref, gml_ref, cin_ref, cb_ref, cc_ref, gcv_ref,
                  cin_p, cc_p, cin_n, cc_n, ymla_ref, yf_ref, yb_ref, bon_ref,
                  wbm_ref, wbc_ref, wbr_ref, wout_ref, gpost_ref, cw_ref, cbias_ref, gng_ref, gnb_ref,
                  avg_ref, o_ref, *, tiles_per_batch, ctx_tiles):
    tm = z_ref.shape[0]
    d = D_MODEL
    pos = pl.program_id(0) % tiles_per_batch
    x_tiles = tiles_per_batch - ctx_tiles
    first = jnp.logical_or(pos == 0, pos == x_tiles)
    last = jnp.logical_or(pos == x_tiles - 1, pos == tiles_per_batch - 1)

    u = cc_ref[...] * cin_ref[...]
    u_halo_p = jnp.where(first, 0.0, cc_p[7:8, :] * cin_p[7:8, :])
    u_halo_n = jnp.where(last, 0.0, cc_n[0:1, :] * cin_n[0:1, :])
    row = lax.broadcasted_iota(jnp.int32, u.shape, 0)
    u_prev = jnp.where(row == 0, u_halo_p, pltpu.roll(u, 1, axis=0))
    u_next = jnp.where(row == tm - 1, u_halo_n, pltpu.roll(u, tm - 1, axis=0))
    cw = cw_ref[...]
    y_conv = cb_ref[...] * (u_prev * cw[0:1] + u * cw[1:2] + u_next * cw[2:3] + cbias_ref[...])

    avg = avg_ref[...]
    yr = yf_ref[...] + yb_ref[...]
    mu = _dot_exact_rhs(yr, avg)
    dv = yr - mu
    var = _dot_exact_rhs(dv * dv, avg)
    y_rwkv = dv * lax.rsqrt(var + GN_EPS) * gng_ref[...] + gnb_ref[...] + bon_ref[0] + bon_ref[1]

    br_mla = _dot((ymla_ref[...] * _silu(gml_ref[...])).astype(BF16), wbm_ref[...])
    br_conv = _dot((y_conv * _silu(gcv_ref[...])).astype(BF16), wbc_ref[...])
    br_rwkv = _dot((y_rwkv * _silu(grw_ref[...])).astype(BF16), wbr_ref[...])
    s = _sigmoid(gl_ref[...])
    merged = s[:, :d] * br_mla + s[:, d:2 * d] * br_conv + s[:, 2 * d:] * br_rwkv
    o = _dot(merged.astype(BF16), wout_ref[...])
    o_ref[...] = z_ref[...] + gate_ref[0] * _rms(o, gpost_ref[...])


def _merge(z2, mods, p2, y_mla, yf, yb, bonus, wbm, wbc, wbr, wout, g_post, conv_w, conv_b,
           gn_g, gn_b, avg_bd, *, tiles_per_batch, ctx_tiles):
    n, d = z2.shape
    tm = ROW_BLK
    hb = tm // 8
    nb8 = n // 8
    pcol = lambda blk: pl.BlockSpec((tm, 512), lambda i: (i, blk))
    prev = lambda blk: pl.BlockSpec((8, 512), lambda i: (jnp.maximum(i * hb - 1, 0), blk))
    nxt = lambda blk: pl.BlockSpec((8, 512), lambda i: (jnp.minimum((i + 1) * hb, nb8 - 1), blk))
    const = lambda shape: pl.BlockSpec(shape, lambda i: (0,) * len(shape))
    row512 = pl.BlockSpec((tm, 512), lambda i: (i, 0))
    return pl.pallas_call(
        functools.partial(_merge_kernel, tiles_per_batch=tiles_per_batch, ctx_tiles=ctx_tiles),
        out_shape=jax.ShapeDtypeStruct((n, d), F32),
        grid=(n // tm,),
        in_specs=[
            pl.BlockSpec((tm, d), lambda i: (i, 0)),
            pl.BlockSpec((1, 1, d), lambda i: (_seg_row(i, tiles_per_batch), 0, 2)),
            pl.BlockSpec((tm, 3 * d), lambda i: (i, PB_GATE)),
            pcol(PB_GRWKV), pcol(PB_GMLA), pcol(PB_CVIN), pcol(PB_CVB), pcol(PB_CVC), pcol(PB_GCONV),
            prev(PB_CVIN), prev(PB_CVC), nxt(PB_CVIN), nxt(PB_CVC),
            row512, row512, row512,
            pl.BlockSpec((2, tm, 512), lambda i: (0, i, 0)),
            const(wbm.shape), const(wbc.shape), const(wbr.shape), const(wout.shape),
            const((1, d)), const(conv_w.shape), const((1, 512)), const((1, 512)), const((1, 512)),
            const(avg_bd.shape),
        ],
        out_specs=pl.BlockSpec((tm, d), lambda i: (i, 0)),
        compiler_params=_cparams(("parallel",)),
        name="merge",
    )(z2, mods, p2, p2, p2, p2, p2, p2, p2, p2, p2, p2, p2, y_mla, yf, yb, bonus,
      wbm, wbc, wbr, wout, g_post, conv_w, conv_b, gn_g, gn_b, avg_bd)


def _pair_swap(w):
    s = w.shape
    return w.reshape(s[:-1] + (s[-1] // 2, 2))[..., ::-1].reshape(s)


def _layout_w_in(w_in):
    sizes = (Q_LORA, KV_LORA, QK_ROPE, W_MLA, CONV_W, CONV_W, CONV_W, CONV_W, W_RWKV, W_RWKV, W_RWKV,
             W_LORA, W_LORA, A_LORA, A_LORA, W_RWKV, 3 * D_MODEL)
    offs = np.concatenate([[0], np.cumsum(sizes)])
    names = ("q_lat", "kv_lat", "kr", "g_mla", "cv_in", "cv_b", "cv_c", "g_conv", "r", "k", "v",
             "wd_f", "wd_b", "ad_f", "ad_b", "g_rwkv", "gl")
    col = {nm: w_in[..., offs[i]:offs[i + 1]] for i, nm in enumerate(names)}
    zeros = jnp.zeros(w_in.shape[:-1] + (512 - Q_LORA - 2 * QK_ROPE,), w_in.dtype)
    parts = [col["gl"], col["r"], col["k"], col["v"], col["g_rwkv"], col["g_mla"], col["cv_in"],
             col["cv_b"], col["cv_c"], col["g_conv"],
             col["q_lat"], col["kr"], _pair_swap(col["kr"]), zeros,
             col["kv_lat"], col["wd_f"], col["ad_f"], col["wd_b"], col["ad_b"]]
    out = jnp.concatenate(parts, axis=-1).astype(BF16)
    assert out.shape[-1] == PCOLS
    return out


def _layout_mla_weights(w_uq, w_ukv):
    depth = w_uq.shape[0]
    nh = N_HEADS_MLA
    wq = w_uq.reshape(depth, Q_LORA, nh, QK_HEAD)
    q_nope, q_rope = wq[..., :QK_NOPE], wq[..., QK_NOPE:]
    zq = jnp.zeros((depth, Q_LORA, nh, HEAD_PAD - QK_HEAD), w_uq.dtype)
    wq_a = jnp.concatenate([q_nope, q_rope, zq], axis=-1)
    wq_b = jnp.concatenate([jnp.zeros_like(q_nope), _pair_swap(q_rope), zq], axis=-1)
    wq_all = jnp.concatenate([wq_a.reshape(depth, Q_LORA, nh * HEAD_PAD),
                              wq_b.reshape(depth, Q_LORA, nh * HEAD_PAD)], axis=-1).astype(BF16)

    wkv = w_ukv.reshape(depth, KV_LORA, nh, QK_NOPE + V_HEAD)
    k_nope, v_w = wkv[..., :QK_NOPE], wkv[..., QK_NOPE:]
    zk = jnp.zeros((depth, KV_LORA, nh, HEAD_PAD - QK_NOPE), w_ukv.dtype)
    wk_top = jnp.concatenate([k_nope, zk], axis=-1).reshape(depth, KV_LORA, nh * HEAD_PAD)
    wv = jnp.concatenate([v_w, zk], axis=-1).reshape(depth, KV_LORA, nh * HEAD_PAD).astype(BF16)
    place = np.zeros((HEAD_PAD, HEAD_PAD), np.float32)
    place[np.arange(QK_ROPE), QK_NOPE + np.arange(QK_ROPE)] = 1.0
    e_a = np.tile(place, (1, nh))
    place_b = np.zeros((HEAD_PAD, HEAD_PAD), np.float32)
    place_b[QK_ROPE + np.arange(QK_ROPE), QK_NOPE + np.arange(QK_ROPE)] = 1.0
    e_b = np.tile(place_b, (1, nh))
    top = jnp.concatenate([wk_top, jnp.zeros_like(wk_top)], axis=-1)
    bot = jnp.broadcast_to(jnp.asarray(np.concatenate([e_a, e_b], axis=1)), (depth, HEAD_PAD, 2 * nh * HEAD_PAD))
    wk_all = jnp.concatenate([top, bot.astype(top.dtype)], axis=1).astype(BF16)
    return wq_all, wk_all, wv


def _rope_tables(seq, ctx_len):
    n_freq = QK_ROPE // 4
    pos = np.arange(seq)
    inv = ROPE_THETA ** (-np.arange(n_freq, dtype=np.float32) / n_freq)
    row = (pos // GRID_W).astype(np.float32)
    colp = (pos % GRID_W).astype(np.float32)
    ang = jnp.concatenate([jnp.asarray(row)[:, None] * jnp.asarray(inv), jnp.asarray(colp)[:, None] * jnp.asarray(inv)], axis=-1)
    cos, sin = jnp.cos(ang), jnp.sin(ang)
    cos2 = jnp.repeat(cos, 2, axis=-1)
    sin2 = jnp.stack([-sin, sin], axis=-1).reshape(seq, QK_ROPE)
    ones = jnp.ones((seq, QK_NOPE), F32)
    pad = jnp.zeros((seq, HEAD_PAD - QK_HEAD), F32)
    cos_x = jnp.concatenate([ones, cos2, pad], axis=-1)
    sin_x = jnp.concatenate([jnp.zeros_like(ones), sin2, pad], axis=-1)
    cos_c = jnp.concatenate([jnp.ones((ctx_len, QK_HEAD), F32), jnp.zeros((ctx_len, HEAD_PAD - QK_HEAD), F32)], axis=-1)
    sin_c = jnp.zeros((ctx_len, HEAD_PAD), F32)
    return jnp.concatenate([cos_x, cos_c], axis=0), jnp.concatenate([sin_x, sin_c], axis=0)


def _block_diag_const(n, blk, value):
    i = np.arange(n)
    return np.where((i[:, None] // blk) == (i[None, :] // blk), value, 0.0).astype(np.float32)


def _pick(n, candidates):
    for cand in candidates:
        if n % cand == 0:
            return cand
    raise ValueError(f"no tile for {n}")


def kernel(x, c, ctx, c_ctx, w_mod, b_mod, g_pre, g_post, w_in, g_q, g_kv, w_uq, w_ukv, conv_w, conv_b,
           w0, w_up, a0, a_up, k_k, k_a, r_k, gn_g, gn_b, w_br_mla, w_br_conv, w_br_rwkv, w_out):
    bsz, seq, d = x.shape
    ctx_len = ctx.shape[1]
    depth = w_mod.shape[0]
    assert d == D_MODEL and ctx_len == ROW_BLK and seq % ROW_BLK == 0
    t = seq + ctx_len
    n = bsz * t
    tiles_per_batch = t // ROW_BLK

    cc = jnp.zeros((8, d), F32).at[0].set(c_ctx).at[1:1 + bsz].set(c)
    mods = _adaln(cc, w_mod, b_mod).reshape(depth, 8, 1, 3 * d)

    w_in_p = _layout_w_in(w_in)
    wq_all, wk_all, wv_all = _layout_mla_weights(w_uq, w_ukv)
    cos_t, sin_t = _rope_tables(seq, ctx_len)
    vone = np.zeros((1, N_HEADS_MLA * HEAD_PAD), np.float32)
    vone[0, V_HEAD::HEAD_PAD] = 1.0
    vone = jnp.asarray(vone)
    ones4 = jnp.asarray(_block_diag_const(QUAD, RWKV_HEAD, 1.0), BF16)
    avg_bd = jnp.asarray(_block_diag_const(W_RWKV, RWKV_HEAD, 1.0 / RWKV_HEAD), BF16)

    tq = _pick(seq, (1024, 512, 256))
    tk = _pick(t, (768, 512, 256))

    z = jnp.concatenate([x, ctx], axis=1)
    for l in range(depth):
        z2 = z.reshape(n, d)
        p2 = _proj_in(z2, mods[l], g_pre[l][None], w_in_p[l], tiles_per_batch=tiles_per_batch)
        p3 = p2.reshape(bsz, t, PCOLS)

        q, kt, v = _mla_prep(p3, cos_t, sin_t, g_q[l][None], g_kv[l][None], wq_all[l], wk_all[l], wv_all[l], vone)
        y_mla = _attention(q, kt, v, None, tq=tq, tk=tk, nq=seq // tq, nk=t // tk, q_off=0, k_off=0)
        y_mla = _attention(q, kt, v, y_mla, tq=ctx_len, tk=ctx_len, nq=1, nk=1,
                           q_off=seq // ctx_len, k_off=seq // ctx_len)

        rh, y0, g, h, bonus = _rwkv_prep(
            p3, w0[l][:, None], a0[l][:, None], w_up[l].astype(BF16), a_up[l].astype(BF16),
            k_k[l][None], k_a[l][None], r_k[l].reshape(1, W_RWKV), ones4)
        yf, yb = _rwkv_scan(rh, y0, g, h, n_x=seq // CHUNK, n_ctx=ctx_len // CHUNK)

        z2 = _merge(z2, mods[l], p2, y_mla.reshape(n, W_MLA), yf.reshape(n, W_RWKV), yb.reshape(n, W_RWKV),
                    bonus.reshape(2, n, W_RWKV),
                    w_br_mla[l].astype(BF16), w_br_conv[l].astype(BF16), w_br_rwkv[l].astype(BF16),
                    w_out[l].astype(BF16), g_post[l][None], conv_w[l], conv_b[l][None],
                    gn_g[l][None], gn_b[l][None], avg_bd,
                    tiles_per_batch=tiles_per_batch, ctx_tiles=ctx_len // ROW_BLK)
        z = z2.reshape(bsz, t, d)
    return z[:, :seq]
```

```python
import functools
import math

import numpy as np
import jax
import jax.numpy as jnp
from jax import lax
from jax.experimental import pallas as pl
from jax.experimental.pallas import tpu as pltpu

F32 = jnp.float32
BF16 = jnp.bfloat16

D_MODEL = 1024
DEPTH = 4
GRID_W = 64
N_HEADS_MLA = 8
Q_LORA = 384
KV_LORA = 256
QK_NOPE = 64
QK_ROPE = 32
QK_HEAD = QK_NOPE + QK_ROPE
V_HEAD = 64
W_MLA = N_HEADS_MLA * V_HEAD
ROPE_THETA = 10000.0
ATTN_SCALE = QK_HEAD ** -0.5
CONV_W = 512
RWKV_HEADS = 8
RWKV_HEAD = 64
W_RWKV = RWKV_HEADS * RWKV_HEAD
W_LORA = 64
A_LORA = 64
RMS_EPS = 1e-6
GN_EPS = 64e-5
L2_EPS = 1e-12
LOG2E = math.log2(math.e)

LANES = 128
ROW_BLK = 256
CHUNK = 64
QUAD = 4 * RWKV_HEAD
PREP_ROWS = 256
HEAD_PAD = 128
VT_ROWS = 80
VMEM_LIMIT = 48 * 1024 * 1024

PCOLS = 17 * 512
PB_GATE = 0
PB_RKV = 2
PB_GRWKV = 9
PB_GMLA = 10
PB_CVIN = 11
PB_CVB = 12
PB_CVC = 13
PB_GCONV = 14
PB_Q = 15
PB_KV = 16
PB_LORA128 = (16 * 512 + 256) // 128


def _cparams(sem, vmem=VMEM_LIMIT):
    return pltpu.CompilerParams(dimension_semantics=sem, vmem_limit_bytes=vmem)


def _dot(a, b):
    return jnp.dot(a, b, preferred_element_type=F32)


def _dot_nt(a, b):
    return lax.dot_general(a, b, (((1,), (1,)), ((), ())), preferred_element_type=F32)


def _dot_tn(a, b):
    return lax.dot_general(a, b, (((0,), (0,)), ((), ())), preferred_element_type=F32)


def _split2(x):
    hi = x.astype(BF16)
    lo = (x - hi.astype(F32)).astype(BF16)
    return hi, lo


def _split3(x):
    hi = x.astype(BF16)
    r1 = x - hi.astype(F32)
    mid = r1.astype(BF16)
    lo = (r1 - mid.astype(F32)).astype(BF16)
    return hi, mid, lo


def _dot_exact_rhs(x, m_bf16):
    hi, mid, lo = _split3(x)
    return _dot(hi, m_bf16) + _dot(mid, m_bf16) + _dot(lo, m_bf16)


def _dot_hi(a, b):
    ah, al = _split2(a)
    bh, bl = _split2(b)
    return _dot(ah, bh) + _dot(ah, bl) + _dot(al, bh)


def _sigmoid(x):
    return 1.0 / (1.0 + jnp.exp(-x))


def _silu(x):
    return x * _sigmoid(x)


def _rms(x, g):
    return x * lax.rsqrt(jnp.mean(x * x, axis=-1, keepdims=True) + RMS_EPS) * g


def _adaln_kernel(c_ref, w_ref, b_ref, o_ref):
    a = _silu(c_ref[...])
    o_ref[0] = _dot_hi(a, w_ref[0]) + b_ref[0]


def _adaln(cc, w_mod, b_mod):
    depth, d, d3 = w_mod.shape
    tn = 1024
    return pl.pallas_call(
        _adaln_kernel,
        out_shape=jax.ShapeDtypeStruct((depth, 8, d3), F32),
        grid=(depth, d3 // tn),
        in_specs=[
            pl.BlockSpec((8, d), lambda l, j: (0, 0)),
            pl.BlockSpec((1, d, tn), lambda l, j: (l, 0, j)),
            pl.BlockSpec((1, 1, tn), lambda l, j: (l, 0, j)),
        ],
        out_specs=pl.BlockSpec((1, 8, tn), lambda l, j: (l, 0, j)),
        compiler_params=_cparams(("parallel", "parallel")),
        name="adaln",
    )(cc, w_mod, b_mod.reshape(depth, 1, d3))


def _seg_row(blk, tiles_per_batch):
    return jnp.where(blk % tiles_per_batch == tiles_per_batch - 1, 0, 1 + blk // tiles_per_batch)


def _proj_kernel(z_ref, *refs, nsub):
    mod_refs = refs[:2 * nsub]
    g_ref, w_ref, o_ref, h_ref = refs[2 * nsub:]

    @pl.when(pl.program_id(1) == 0)
    def _():
        y = _rms(z_ref[...], g_ref[...])
        for s in range(nsub):
            rows = slice(s * ROW_BLK, (s + 1) * ROW_BLK)
            shift, scale = mod_refs[2 * s][0], mod_refs[2 * s + 1][0]
            h_ref[rows, :] = (y[rows, :] * (1.0 + scale) + shift).astype(BF16)

    o_ref[...] = _dot(h_ref[...], w_ref[...])


def _proj_in(z2, mods, g_pre, w, *, tiles_per_batch):
    n, d = z2.shape
    tm = 1024 if n % 1024 == 0 else ROW_BLK
    tn = PCOLS // 4
    nsub = tm // ROW_BLK
    mod_specs = []
    for s in range(nsub):
        for col in (0, 1):
            mod_specs.append(pl.BlockSpec(
                (1, 1, d), lambda i, j, s=s, col=col: (_seg_row(i * nsub + s, tiles_per_batch), 0, col)))
    return pl.pallas_call(
        functools.partial(_proj_kernel, nsub=nsub),
        out_shape=jax.ShapeDtypeStruct((n, PCOLS), F32),
        grid=(n // tm, PCOLS // tn),
        in_specs=[pl.BlockSpec((tm, d), lambda i, j: (i, 0))] + mod_specs + [
            pl.BlockSpec((1, d), lambda i, j: (0, 0)),
            pl.BlockSpec((d, tn), lambda i, j: (0, j)),
        ],
        out_specs=pl.BlockSpec((tm, tn), lambda i, j: (i, j)),
        scratch_shapes=[pltpu.VMEM((tm, d), BF16)],
        compiler_params=_cparams(("parallel", "arbitrary")),
        name="proj_in",
    )(z2, *([mods] * (2 * nsub)), g_pre, w)


def _mla_prep_kernel(qb_ref, kb_ref, cos_ref, sin_ref, gq_ref, gkv_ref, wq_ref, wk_ref, wv_ref,
                     vone_ref, qt_ref, k_ref, vt_ref):
    nh = N_HEADS_MLA
    hw = nh * HEAD_PAD
    qb = qb_ref[0]
    kb = kb_ref[0]
    cos8 = jnp.tile(cos_ref[...], (1, nh))
    sin8 = jnp.tile(sin_ref[...], (1, nh))
    qn = _rms(qb[:, :Q_LORA], gq_ref[...]).astype(BF16)
    qq = _dot(qn, wq_ref[...])
    q = (qq[:, :hw] * cos8 + qq[:, hw:] * sin8) * (ATTN_SCALE * LOG2E)
    kvn = _rms(kb[:, :KV_LORA], gkv_ref[...]).astype(BF16)
    kin = jnp.concatenate([kvn, qb[:, Q_LORA:].astype(BF16)], axis=1)
    kk = _dot(kin, wk_ref[...])
    k_ref[0] = (kk[:, :hw] * cos8 + kk[:, hw:] * sin8).astype(BF16)
    v = _dot(kvn, wv_ref[...]) + vone_ref[...]
    for h in range(nh):
        lanes = slice(h * HEAD_PAD, (h + 1) * HEAD_PAD)
        qt_ref[0, h] = q[:, lanes].T.astype(BF16)
        vt_ref[0, h] = v[:, lanes].T[:VT_ROWS].astype(BF16)


def _mla_prep(p3, cos_t, sin_t, g_q, g_kv, wq, wk, wv, vone):
    b, t, _ = p3.shape
    tm = ROW_BLK
    hw = N_HEADS_MLA * HEAD_PAD
    const = lambda shape: pl.BlockSpec(shape, lambda bi, i: (0,) * len(shape))
    return pl.pallas_call(
        _mla_prep_kernel,
        out_shape=(
            jax.ShapeDtypeStruct((b, N_HEADS_MLA, HEAD_PAD, t), BF16),
            jax.ShapeDtypeStruct((b, t, hw), BF16),
            jax.ShapeDtypeStruct((b, N_HEADS_MLA, VT_ROWS, t), BF16),
        ),
        grid=(b, t // tm),
        in_specs=[
            pl.BlockSpec((1, tm, 512), lambda bi, i: (bi, i, PB_Q)),
            pl.BlockSpec((1, tm, 512), lambda bi, i: (bi, i, PB_KV)),
            pl.BlockSpec((tm, HEAD_PAD), lambda bi, i: (i, 0)),
            pl.BlockSpec((tm, HEAD_PAD), lambda bi, i: (i, 0)),
            const((1, Q_LORA)),
            const((1, KV_LORA)),
            const(wq.shape),
            const(wk.shape),
            const(wv.shape),
            const((1, hw)),
        ],
        out_specs=(
            pl.BlockSpec((1, N_HEADS_MLA, HEAD_PAD, tm), lambda bi, i: (bi, 0, 0, i)),
            pl.BlockSpec((1, tm, hw), lambda bi, i: (bi, i, 0)),
            pl.BlockSpec((1, N_HEADS_MLA, VT_ROWS, tm), lambda bi, i: (bi, 0, 0, i)),
        ),
        compiler_params=_cparams(("parallel", "parallel")),
        name="mla_prep",
    )(p3, p3, cos_t, sin_t, g_q, g_kv, wq, wk, wv, vone)


def _attn_kernel(qt_ref, k_ref, vt_ref, *rest, nk):
    o_ref, m_ref, acc_ref = rest[-3:]
    j = pl.program_id(2)

    @pl.when(j == 0)
    def _():
        m_ref[...] = jnp.full(m_ref.shape, -jnp.inf, F32)
        acc_ref[...] = jnp.zeros(acc_ref.shape, F32)

    def scores(h):
        return _dot(k_ref[0, :, h * HEAD_PAD:(h + 1) * HEAD_PAD], qt_ref[0, h])

    s_next = scores(0)
    for h in range(N_HEADS_MLA):
        s = s_next
        if h + 1 < N_HEADS_MLA:
            s_next = scores(h + 1)
        m_prev = m_ref[h]
        m_new = jnp.maximum(m_prev, jnp.max(s, axis=0, keepdims=True))
        alpha = jnp.exp2(m_prev - m_new)
        p = jnp.exp2(s - m_new[0:1, :]).astype(BF16)
        acc_ref[h] = alpha[0:1, :] * acc_ref[h] + _dot(vt_ref[0, h], p)
        m_ref[h] = m_new

    @pl.when(j == nk - 1)
    def _():
        outs = []
        for h in range(N_HEADS_MLA):
            a = acc_ref[h]
            outs.append(a[:V_HEAD] * (1.0 / a[V_HEAD:V_HEAD + 1]))
        o_ref[0] = jnp.concatenate(outs, axis=0).T


def _attention(qt, k, vt, y_prev, *, tq, tk, nq, nk, q_off, k_off):
    b, t, hw = k.shape
    in_specs = [
        pl.BlockSpec((1, N_HEADS_MLA, HEAD_PAD, tq), lambda bi, i, j: (bi, 0, 0, i + q_off)),
        pl.BlockSpec((1, tk, hw), lambda bi, i, j: (bi, j + k_off, 0)),
        pl.BlockSpec((1, N_HEADS_MLA, VT_ROWS, tk), lambda bi, i, j: (bi, 0, 0, j + k_off)),
    ]
    args = [qt, k, vt]
    aliases = {}
    if y_prev is not None:
        in_specs.append(pl.BlockSpec(memory_space=pl.ANY))
        args.append(y_prev)
        aliases = {3: 0}
    return pl.pallas_call(
        functools.partial(_attn_kernel, nk=nk),
        out_shape=jax.ShapeDtypeStruct((b, t, W_MLA), F32),
        grid=(b, nq, nk),
        in_specs=in_specs,
        out_specs=pl.BlockSpec((1, tq, W_MLA), lambda bi, i, j: (bi, i + q_off, 0)),
        scratch_shapes=[
            pltpu.VMEM((N_HEADS_MLA, 8, tq), F32),
            pltpu.VMEM((N_HEADS_MLA, VT_ROWS, tq), F32),
        ],
        input_output_aliases=aliases,
        compiler_params=_cparams(("parallel", "parallel", "arbitrary")),
        name="attn_ctx" if y_prev is not None else "attn_x",
    )(*args)


def _half_lane_masks():
    lane = lax.broadcasted_iota(jnp.int32, (CHUNK, LANES), 1)
    first = lane < RWKV_HEAD
    return first, jnp.where(first, 1.0, 0.0).astype(BF16), jnp.where(first, 0.0, 1.0).astype(BF16)


def _bd4(x, lo, hi):
    xb = x.astype(BF16)
    xl, xr = xb[:, :LANES], xb[:, LANES:]
    z = jnp.zeros((2 * CHUNK, LANES), BF16)
    c0 = jnp.concatenate([xl * lo, xl * hi, z], axis=0)
    c1 = jnp.concatenate([z, xr * lo, xr * hi], axis=0)
    return jnp.concatenate([c0, c1], axis=1)


def _bd4_f32(x, first):
    xl, xr = x[:, :LANES], x[:, LANES:]
    z = jnp.zeros((2 * CHUNK, LANES), F32)
    c0 = jnp.concatenate([jnp.where(first, xl, 0.0), jnp.where(first, 0.0, xl), z], axis=0)
    c1 = jnp.concatenate([z, jnp.where(first, xr, 0.0), jnp.where(first, 0.0, xr)], axis=0)
    return jnp.concatenate([c0, c1], axis=1)


def _diag_blocks(full, first):
    c = CHUNK
    left = jnp.where(first, full[0:c, :LANES], full[c:2 * c, :LANES])
    right = jnp.where(first, full[2 * c:3 * c, LANES:], full[3 * c:4 * c, LANES:])
    return jnp.concatenate([left, right], axis=1)


def _rwkv_prep_kernel(rkv_ref, lora_ref, w0_ref, a0_ref, wup_ref, aup_ref, kk_ref, ka_ref, rk_ref,
                      ones_ref, rh_ref, y0_ref, g_ref, h_ref, bonus_ref):
    fwd = pl.program_id(0) == 0
    c = CHUNK
    w = W_RWKV
    rows = rkv_ref.shape[1]
    rkv = rkv_ref[0]
    r, k, v = rkv[:, :w], rkv[:, w:2 * w], rkv[:, 2 * w:]
    lora = lora_ref[0]
    zw = w0_ref[0] + _dot(jnp.tanh(lora[:, :W_LORA]).astype(BF16), wup_ref[0])
    ell = -math.exp(-0.5) * _sigmoid(zw)
    a = _sigmoid(a0_ref[0] + _dot(lora[:, W_LORA:].astype(BF16), aup_ref[0]))
    ones4 = ones_ref[...]

    def head_sum(x):
        hi, lo = _split2(x)
        parts = []
        for qd in range(w // QUAD):
            ls = slice(qd * QUAD, (qd + 1) * QUAD)
            parts.append(_dot(hi[:, ls], ones4) + _dot(lo[:, ls], ones4))
        return jnp.concatenate(parts, axis=1)

    kkr = k * kk_ref[...]
    kk = kkr * lax.rsqrt(head_sum(kkr * kkr) + L2_EPS)
    k_d = k * (1.0 + (a - 1.0) * ka_ref[...])
    bonus_ref[0, 0] = head_sum(r * k_d * rk_ref[...]) * v

    sgn = jnp.where(fwd, 1, -1)
    ti = lax.broadcasted_iota(jnp.int32, (rows, rows), 0)
    si = lax.broadcasted_iota(jnp.int32, (rows, rows), 1)
    same = jnp.where((ti // c) == (si // c), 1.0, 0.0)
    tri = jnp.where((si - ti) * sgn <= 0, same, 0.0).astype(BF16)
    same = same.astype(BF16)
    ell_hi, ell_lo = _split2(ell)
    lc = _dot(tri, ell_hi) + _dot(tri, ell_lo)
    ltot = _dot(same, ell_hi) + _dot(same, ell_lo)
    e_neg = jnp.exp(-lc)
    e_tail = jnp.exp(ltot - lc)
    gam = jnp.exp(ltot)
    kka = kk * a
    abar = kk * jnp.exp(lc - ell)
    bbar = kka * e_neg
    kbar = k_d * e_neg
    rbar = r * jnp.exp(lc)
    btil = kka * e_tail
    ktil = k_d * e_tail

    first, lo, hi = _half_lane_masks()
    tq = lax.broadcasted_iota(jnp.int32, (c, QUAD), 0)
    sq = lax.broadcasted_iota(jnp.int32, (c, QUAD), 1) % c
    before = (sq - tq) * sgn < 0
    upto = (sq - tq) * sgn <= 0
    eye = jnp.where(sq == tq, 1.0, 0.0)

    def pm(x, y):
        return _dot(x.astype(BF16), _bd4(y, lo, hi))

    sl = [(slice(ch * c, (ch + 1) * c), slice(qd * QUAD, (qd + 1) * QUAD))
          for ch in range(rows // c) for qd in range(w // QUAD)]
    la = [jnp.concatenate([abar[s], rbar[s]], axis=0).astype(BF16) for s in sl]
    nb = [_dot_nt(la_i, _bd4(bbar[s], lo, hi)) for la_i, s in zip(la, sl)]
    nk = [_dot_nt(la_i, _bd4(kbar[s], lo, hi)) for la_i, s in zip(la, sl)]
    n = [jnp.where(before, t[:c], 0.0) for t in nb]
    a_rb = [jnp.where(upto, t[c:], 0.0) for t in nb]
    a_ak = [jnp.where(before, t[:c], 0.0) for t in nk]
    a_rk = [jnp.where(upto, t[c:], 0.0) for t in nk]
    py = [pm(jnp.concatenate([u, r_], axis=0), v[s]) for u, r_, s in zip(a_ak, a_rk, sl)]
    x = [eye - t for t in n]
    p = [pm(t, t) for t in n]
    for it in range(5):
        if it < 4:
            xp = [pm(jnp.concatenate([x_i, p_i], axis=0), p_i) for x_i, p_i in zip(x, p)]
            x = [x_i + t[:c] for x_i, t in zip(x, xp)]
            p = [t[c:] for t in xp]
        else:
            x = [x_i + pm(x_i, p_i) for x_i, p_i in zip(x, p)]
    ah = [pm(x_i, abar[s]) for x_i, s in zip(x, sl)]
    u0 = [pm(x_i, t[:c]) for x_i, t in zip(x, py)]
    ra = [pm(m_i, t) for m_i, t in zip(a_rb, ah)]
    ru = [pm(m_i, t) for m_i, t in zip(a_rb, u0)]
    gfull = [_dot_tn(btil[s].astype(BF16), t.astype(BF16)) for s, t in zip(sl, ah)]
    hfull = [_dot_tn(jnp.concatenate([ktil[s], -btil[s]], axis=0).astype(BF16),
                     jnp.concatenate([v[s], t], axis=0).astype(BF16)) for s, t in zip(sl, u0)]
    for i, (rs, ls) in enumerate(sl):
        rh_ref[0, 0, rs, ls] = (rbar[rs, ls] - ra[i]).astype(BF16)
        y0_ref[0, 0, rs, ls] = py[i][c:] - ru[i]
        g_ref[0, 0, rs, ls] = eye * gam[rs, ls] - _diag_blocks(gfull[i], first)
        h_ref[0, 0, rs, ls] = _diag_blocks(hfull[i], first)


def _rwkv_prep(p3, w0, a0, w_up, a_up, k_k, k_a, r_k, ones4):
    b, t, _ = p3.shape
    w = W_RWKV
    rows = PREP_ROWS
    const = lambda shape: pl.BlockSpec(shape, lambda d, bi, i: (0,) * len(shape))
    perdir = lambda shape: pl.BlockSpec((1,) + shape, lambda d, bi, i: (d,) + (0,) * len(shape))
    out_spec = pl.BlockSpec((1, 1, rows, w), lambda d, bi, i: (d, bi, i, 0))
    f32_out = jax.ShapeDtypeStruct((2, b, t, w), F32)
    return pl.pallas_call(
        _rwkv_prep_kernel,
        out_shape=(jax.ShapeDtypeStruct((2, b, t, w), BF16), f32_out, f32_out, f32_out, f32_out),
        grid=(2, b, t // rows),
        in_specs=[
            pl.BlockSpec((1, rows, 3 * w), lambda d, bi, i: (bi, i, PB_RKV)),
            pl.BlockSpec((1, rows, 2 * W_LORA), lambda d, bi, i: (bi, i, PB_LORA128 + d)),
            perdir((1, w)), perdir((1, w)), perdir((W_LORA, w)), perdir((A_LORA, w)),
            const((1, w)), const((1, w)), const((1, w)),
            const((QUAD, QUAD)),
        ],
        out_specs=(out_spec,) * 5,
        compiler_params=_cparams(("parallel", "parallel", "parallel")),
        name="rwkv_prep",
    )(p3, p3, w0, a0, w_up, a_up, k_k, k_a, r_k, ones4)


def _rwkv_scan_kernel(rhf, y0f, gf, hf, rhb, y0b, gb, hb, yf_ref, yb_ref, st_ref, *, nb):
    @pl.when(pl.program_id(0) == 0)
    def _():
        st_ref[...] = jnp.zeros(st_ref.shape, F32)

    first, lo, hi = _half_lane_masks()
    c = CHUNK
    for d, (rh, y0, g, hh, y_ref) in enumerate(((rhf, y0f, gf, hf, yf_ref), (rhb, y0b, gb, hb, yb_ref))):
        for bi in range(nb):
            for qd in range(W_RWKV // QUAD):
                ls = slice(qd * QUAD, (qd + 1) * QUAD)
                lhs = jnp.concatenate([rh[0, bi, :, ls], _bd4(g[0, bi, :, ls], lo, hi)], axis=0)
                res = _dot(lhs, st_ref[d, bi, qd].astype(BF16))
                y_ref[bi, :, ls] = res[:c] + y0[0, bi, :, ls]
                st_ref[d, bi, qd] = res[c:] + _bd4_f32(hh[0, bi, :, ls], first)


def _rwkv_scan(rh, y0, g, h, *, n_x, n_ctx):
    _, b, t, w = rh.shape
    c = CHUNK
    n = n_x + n_ctx
    fidx = lambda j: jnp.where(j < n_ctx, n_x + j, j - n_ctx)
    bidx = lambda j: n - 1 - j
    fspec = pl.BlockSpec((1, b, c, w), lambda j: (0, 0, fidx(j), 0))
    bspec = pl.BlockSpec((1, b, c, w), lambda j: (1, 0, bidx(j), 0))
    y_shape = jax.ShapeDtypeStruct((b, t, w), F32)
    return pl.pallas_call(
        functools.partial(_rwkv_scan_kernel, nb=b),
        out_shape=(y_shape, y_shape),
        grid=(n,),
        in_specs=[fspec] * 4 + [bspec] * 4,
        out_specs=(
            pl.BlockSpec((b, c, w), lambda j: (0, fidx(j), 0)),
            pl.BlockSpec((b, c, w), lambda j: (0, bidx(j), 0)),
        ),
        scratch_shapes=[pltpu.VMEM((2, b, w // QUAD, QUAD, QUAD), F32)],
        compiler_params=_cparams(("arbitrary",)),
        name="rwkv_scan",
    )(rh, y0, g, h, rh, y0, g, h)


def _merge_kernel(z_ref, gate_ref, gl_ref, grw_ref, gml_ref, cin_ref, cb_ref, cc_ref, gcv_ref,
                  cin_p, cc_p, cin_n, cc_n, ymla_ref, yf_ref, yb_ref, bon_ref,
                  wbm_ref, wbc_ref, wbr_ref, wout_ref, gpost_ref, cw_ref, cbias_ref, gng_ref, gnb_ref,
                  avg_ref, o_ref, *, tiles_per_batch, ctx_tiles):
    tm = z_ref.shape[0]
    d = D_MODEL
    pos = pl.program_id(0) % tiles_per_batch
    x_tiles = tiles_per_batch - ctx_tiles
    first = jnp.logical_or(pos == 0, pos == x_tiles)
    last = jnp.logical_or(pos == x_tiles - 1, pos == tiles_per_batch - 1)

    u = cc_ref[...] * cin_ref[...]
    u_halo_p = jnp.where(first, 0.0, cc_p[7:8, :] * cin_p[7:8, :])
    u_halo_n = jnp.where(last, 0.0, cc_n[0:1, :] * cin_n[0:1, :])
    row = lax.broadcasted_iota(jnp.int32, u.shape, 0)
    u_prev = jnp.where(row == 0, u_halo_p, pltpu.roll(u, 1, axis=0))
    u_next = jnp.where(row == tm - 1, u_halo_n, pltpu.roll(u, tm - 1, axis=0))
    cw = cw_ref[...]
    y_conv = cb_ref[...] * (u_prev * cw[0:1] + u * cw[1:2] + u_next * cw[2:3] + cbias_ref[...])

    avg = avg_ref[...]
    yr = yf_ref[...] + yb_ref[...]
    mu = _dot_exact_rhs(yr, avg)
    dv = yr - mu
    var = _dot_exact_rhs(dv * dv, avg)
    y_rwkv = dv * lax.rsqrt(var + GN_EPS) * gng_ref[...] + gnb_ref[...] + bon_ref[0] + bon_ref[1]

    br_mla = _dot((ymla_ref[...] * _silu(gml_ref[...])).astype(BF16), wbm_ref[...])
    br_conv = _dot((y_conv * _silu(gcv_ref[...])).astype(BF16), wbc_ref[...])
    br_rwkv = _dot((y_rwkv * _silu(grw_ref[...])).astype(BF16), wbr_ref[...])
    s = _sigmoid(gl_ref[...])
    merged = s[:, :d] * br_mla + s[:, d:2 * d] * br_conv + s[:, 2 * d:] * br_rwkv
    o = _dot(merged.astype(BF16), wout_ref[...])
    o_ref[...] = z_ref[...] + gate_ref[0] * _rms(o, gpost_ref[...])


def _merge(z2, mods, p2, y_mla, yf, yb, bonus, wbm, wbc, wbr, wout, g_post, conv_w, conv_b,
           gn_g, gn_b, avg_bd, *, tiles_per_batch, ctx_tiles):
    n, d = z2.shape
    tm = ROW_BLK
    hb = tm // 8
    nb8 = n // 8
    pcol = lambda blk: pl.BlockSpec((tm, 512), lambda i: (i, blk))
    prev = lambda blk: pl.BlockSpec((8, 512), lambda i: (jnp.maximum(i * hb - 1, 0), blk))
    nxt = lambda blk: pl.BlockSpec((8, 512), lambda i: (jnp.minimum((i + 1) * hb, nb8 - 1), blk))
    const = lambda shape: pl.BlockSpec(shape, lambda i: (0,) * len(shape))
    row512 = pl.BlockSpec((tm, 512), lambda i: (i, 0))
    return pl.pallas_call(
        functools.partial(_merge_kernel, tiles_per_batch=tiles_per_batch, ctx_tiles=ctx_tiles),
        out_shape=jax.ShapeDtypeStruct((n, d), F32),
        grid=(n // tm,),
        in_specs=[
            pl.BlockSpec((tm, d), lambda i: (i, 0)),
            pl.BlockSpec((1, 1, d), lambda i: (_seg_row(i, tiles_per_batch), 0, 2)),
            pl.BlockSpec((tm, 3 * d), lambda i: (i, PB_GATE)),
            pcol(PB_GRWKV), pcol(PB_GMLA), pcol(PB_CVIN), pcol(PB_CVB), pcol(PB_CVC), pcol(PB_GCONV),
            prev(PB_CVIN), prev(PB_CVC), nxt(PB_CVIN), nxt(PB_CVC),
            row512, row512, row512,
            pl.BlockSpec((2, tm, 512), lambda i: (0, i, 0)),
            const(wbm.shape), const(wbc.shape), const(wbr.shape), const(wout.shape),
            const((1, d)), const(conv_w.shape), const((1, 512)), const((1, 512)), const((1, 512)),
            const(avg_bd.shape),
        ],
        out_specs=pl.BlockSpec((tm, d), lambda i: (i, 0)),
        compiler_params=_cparams(("parallel",)),
        name="merge",
    )(z2, mods, p2, p2, p2, p2, p2, p2, p2, p2, p2, p2, p2, y_mla, yf, yb, bonus,
      wbm, wbc, wbr, wout, g_post, conv_w, conv_b, gn_g, gn_b, avg_bd)


def _pair_swap(w):
    s = w.shape
    return w.reshape(s[:-1] + (s[-1] // 2, 2))[..., ::-1].reshape(s)


def _layout_w_in(w_in):
    sizes = (Q_LORA, KV_LORA, QK_ROPE, W_MLA, CONV_W, CONV_W, CONV_W, CONV_W, W_RWKV, W_RWKV, W_RWKV,
             W_LORA, W_LORA, A_LORA, A_LORA, W_RWKV, 3 * D_MODEL)
    offs = np.concatenate([[0], np.cumsum(sizes)])
    names = ("q_lat", "kv_lat", "kr", "g_mla", "cv_in", "cv_b", "cv_c", "g_conv", "r", "k", "v",
             "wd_f", "wd_b", "ad_f", "ad_b", "g_rwkv", "gl")
    col = {nm: w_in[..., offs[i]:offs[i + 1]] for i, nm in enumerate(names)}
    zeros = jnp.zeros(w_in.shape[:-1] + (512 - Q_LORA - 2 * QK_ROPE,), w_in.dtype)
    parts = [col["gl"], col["r"], col["k"], col["v"], col["g_rwkv"], col["g_mla"], col["cv_in"],
             col["cv_b"], col["cv_c"], col["g_conv"],
             col["q_lat"], col["kr"], _pair_swap(col["kr"]), zeros,
             col["kv_lat"], col["wd_f"], col["ad_f"], col["wd_b"], col["ad_b"]]
    out = jnp.concatenate(parts, axis=-1).astype(BF16)
    assert out.shape[-1] == PCOLS
    return out


def _layout_mla_weights(w_uq, w_ukv):
    depth = w_uq.shape[0]
    nh = N_HEADS_MLA
    wq = w_uq.reshape(depth, Q_LORA, nh, QK_HEAD)
    q_nope, q_rope = wq[..., :QK_NOPE], wq[..., QK_NOPE:]
    zq = jnp.zeros((depth, Q_LORA, nh, HEAD_PAD - QK_HEAD), w_uq.dtype)
    wq_a = jnp.concatenate([q_nope, q_rope, zq], axis=-1)
    wq_b = jnp.concatenate([jnp.zeros_like(q_nope), _pair_swap(q_rope), zq], axis=-1)
    wq_all = jnp.concatenate([wq_a.reshape(depth, Q_LORA, nh * HEAD_PAD),
                              wq_b.reshape(depth, Q_LORA, nh * HEAD_PAD)], axis=-1).astype(BF16)

    wkv = w_ukv.reshape(depth, KV_LORA, nh, QK_NOPE + V_HEAD)
    k_nope, v_w = wkv[..., :QK_NOPE], wkv[..., QK_NOPE:]
    zk = jnp.zeros((depth, KV_LORA, nh, HEAD_PAD - QK_NOPE), w_ukv.dtype)
    wk_top = jnp.concatenate([k_nope, zk], axis=-1).reshape(depth, KV_LORA, nh * HEAD_PAD)
    wv = jnp.concatenate([v_w, zk], axis=-1).reshape(depth, KV_LORA, nh * HEAD_PAD).astype(BF16)
    place = np.zeros((HEAD_PAD, HEAD_PAD), np.float32)
    place[np.arange(QK_ROPE), QK_NOPE + np.arange(QK_ROPE)] = 1.0
    e_a = np.tile(place, (1, nh))
    place_b = np.zeros((HEAD_PAD, HEAD_PAD), np.float32)
    place_b[QK_ROPE + np.arange(QK_ROPE), QK_NOPE + np.arange(QK_ROPE)] = 1.0
    e_b = np.tile(place_b, (1, nh))
    top = jnp.concatenate([wk_top, jnp.zeros_like(wk_top)], axis=-1)
    bot = jnp.broadcast_to(jnp.asarray(np.concatenate([e_a, e_b], axis=1)), (depth, HEAD_PAD, 2 * nh * HEAD_PAD))
    wk_all = jnp.concatenate([top, bot.astype(top.dtype)], axis=1).astype(BF16)
    return wq_all, wk_all, wv


def _rope_tables(seq, ctx_len):
    n_freq = QK_ROPE // 4
    pos = np.arange(seq)
    inv = ROPE_THETA ** (-np.arange(n_freq, dtype=np.float32) / n_freq)
    row = (pos // GRID_W).astype(np.float32)
    colp = (pos % GRID_W).astype(np.float32)
    ang = jnp.concatenate([jnp.asarray(row)[:, None] * jnp.asarray(inv), jnp.asarray(colp)[:, None] * jnp.asarray(inv)], axis=-1)
    cos, sin = jnp.cos(ang), jnp.sin(ang)
    cos2 = jnp.repeat(cos, 2, axis=-1)
    sin2 = jnp.stack([-sin, sin], axis=-1).reshape(seq, QK_ROPE)
    ones = jnp.ones((seq, QK_NOPE), F32)
    pad = jnp.zeros((seq, HEAD_PAD - QK_HEAD), F32)
    cos_x = jnp.concatenate([ones, cos2, pad], axis=-1)
    sin_x = jnp.concatenate([jnp.zeros_like(ones), sin2, pad], axis=-1)
    cos_c = jnp.concatenate([jnp.ones((ctx_len, QK_HEAD), F32), jnp.zeros((ctx_len, HEAD_PAD - QK_HEAD), F32)], axis=-1)
    sin_c = jnp.zeros((ctx_len, HEAD_PAD), F32)
    return jnp.concatenate([cos_x, cos_c], axis=0), jnp.concatenate([sin_x, sin_c], axis=0)


def _block_diag_const(n, blk, value):
    i = np.arange(n)
    return np.where((i[:, None] // blk) == (i[None, :] // blk), value, 0.0).astype(np.float32)


def _pick(n, candidates):
    for cand in candidates:
        if n % cand == 0:
            return cand
    raise ValueError(f"no tile for {n}")


def kernel(x, c, ctx, c_ctx, w_mod, b_mod, g_pre, g_post, w_in, g_q, g_kv, w_uq, w_ukv, conv_w, conv_b,
           w0, w_up, a0, a_up, k_k, k_a, r_k, gn_g, gn_b, w_br_mla, w_br_conv, w_br_rwkv, w_out):
    bsz, seq, d = x.shape
    ctx_len = ctx.shape[1]
    depth = w_mod.shape[0]
    assert d == D_MODEL and ctx_len == ROW_BLK and seq % ROW_BLK == 0
    t = seq + ctx_len
    n = bsz * t
    tiles_per_batch = t // ROW_BLK

    cc = jnp.zeros((8, d), F32).at[0].set(c_ctx).at[1:1 + bsz].set(c)
    mods = _adaln(cc, w_mod, b_mod).reshape(depth, 8, 1, 3 * d)

    w_in_p = _layout_w_in(w_in)
    wq_all, wk_all, wv_all = _layout_mla_weights(w_uq, w_ukv)
    cos_t, sin_t = _rope_tables(seq, ctx_len)
    vone = np.zeros((1, N_HEADS_MLA * HEAD_PAD), np.float32)
    vone[0, V_HEAD::HEAD_PAD] = 1.0
    vone = jnp.asarray(vone)
    ones4 = jnp.asarray(_block_diag_const(QUAD, RWKV_HEAD, 1.0), BF16)
    avg_bd = jnp.asarray(_block_diag_const(W_RWKV, RWKV_HEAD, 1.0 / RWKV_HEAD), BF16)

    tq = _pick(seq, (1024, 512, 256))
    tk = _pick(t, (768, 512, 256))

    z = jnp.concatenate([x, ctx], axis=1)
    for l in range(depth):
        z2 = z.reshape(n, d)
        p2 = _proj_in(z2, mods[l], g_pre[l][None], w_in_p[l], tiles_per_batch=tiles_per_batch)
        p3 = p2.reshape(bsz, t, PCOLS)

        qt, k, vt = _mla_prep(p3, cos_t, sin_t, g_q[l][None], g_kv[l][None], wq_all[l], wk_all[l], wv_all[l], vone)
        y_mla = _attention(qt, k, vt, None, tq=tq, tk=tk, nq=seq // tq, nk=t // tk, q_off=0, k_off=0)
        y_mla = _attention(qt, k, vt, y_mla, tq=ctx_len, tk=ctx_len, nq=1, nk=1,
                           q_off=seq // ctx_len, k_off=seq // ctx_len)

        rh, y0, g, h, bonus = _rwkv_prep(
            p3, w0[l][:, None], a0[l][:, None], w_up[l].astype(BF16), a_up[l].astype(BF16),
            k_k[l][None], k_a[l][None], r_k[l].reshape(1, W_RWKV), ones4)
        yf, yb = _rwkv_scan(rh, y0, g, h, n_x=seq // CHUNK, n_ctx=ctx_len // CHUNK)

        z2 = _merge(z2, mods[l], p2, y_mla.reshape(n, W_MLA), yf.reshape(n, W_RWKV), yb.reshape(n, W_RWKV),
                    bonus.reshape(2, n, W_RWKV),
                    w_br_mla[l].astype(BF16), w_br_conv[l].astype(BF16), w_br_rwkv[l].astype(BF16),
                    w_out[l].astype(BF16), g_post[l][None], conv_w[l], conv_b[l][None],
                    gn_g[l][None], gn_b[l][None], avg_bd,
                    tiles_per_batch=tiles_per_batch, ctx_tiles=ctx_len // ROW_BLK)
        z = z2.reshape(bsz, t, d)
    return z[:, :seq]
```

```python
import functools
import math

import numpy as np
import jax
import jax.numpy as jnp
from jax import lax
from jax.experimental import pallas as pl
from jax.experimental.pallas import tpu as pltpu

F32 = jnp.float32
BF16 = jnp.bfloat16
ACT = BF16

D_MODEL = 1024
DEPTH = 4
GRID_W = 64
N_HEADS_MLA = 8
Q_LORA = 384
KV_LORA = 256
QK_NOPE = 64
QK_ROPE = 32
QK_HEAD = QK_NOPE + QK_ROPE
V_HEAD = 64
W_MLA = N_HEADS_MLA * V_HEAD
ROPE_THETA = 10000.0
ATTN_SCALE = QK_HEAD ** -0.5
CONV_W = 512
RWKV_HEADS = 8
RWKV_HEAD = 64
W_RWKV = RWKV_HEADS * RWKV_HEAD
W_LORA = 64
A_LORA = 64
RMS_EPS = 1e-6
GN_EPS = 64e-5
L2_EPS = 1e-12
LOG2E = math.log2(math.e)

LANES = 128
ROW_BLK = 256
CHUNK = 64
QUAD = 4 * RWKV_HEAD
PREP_ROWS = 256
HEAD_PAD = 128
HALO_ROWS = 16
VT_ROWS = 80
VMEM_LIMIT = 48 * 1024 * 1024

PCOLS = 17 * 512
PB_GATE = 0
PB_RKV = 2
PB_GRWKV = 9
PB_GMLA = 10
PB_CVIN = 11
PB_CVB = 12
PB_CVC = 13
PB_GCONV = 14
PB_Q = 15
PB_KV = 16
PB_LORA128 = (16 * 512 + 256) // 128


def _cparams(sem, vmem=VMEM_LIMIT):
    return pltpu.CompilerParams(dimension_semantics=sem, vmem_limit_bytes=vmem)


def _dot(a, b):
    return jnp.dot(a, b, preferred_element_type=F32)


def _dot_nt(a, b):
    return lax.dot_general(a, b, (((1,), (1,)), ((), ())), preferred_element_type=F32)


def _dot_tn(a, b):
    return lax.dot_general(a, b, (((0,), (0,)), ((), ())), preferred_element_type=F32)


def _split2(x):
    hi = x.astype(BF16)
    lo = (x - hi.astype(F32)).astype(BF16)
    return hi, lo


def _split3(x):
    hi = x.astype(BF16)
    r1 = x - hi.astype(F32)
    mid = r1.astype(BF16)
    lo = (r1 - mid.astype(F32)).astype(BF16)
    return hi, mid, lo


def _dot_exact_rhs(x, m_bf16):
    hi, mid, lo = _split3(x)
    return _dot(hi, m_bf16) + _dot(mid, m_bf16) + _dot(lo, m_bf16)


def _dot_hi(a, b):
    ah, al = _split2(a)
    bh, bl = _split2(b)
    return _dot(ah, bh) + _dot(ah, bl) + _dot(al, bh)


def _sigmoid(x):
    return 1.0 / (1.0 + jnp.exp(-x))


def _silu(x):
    return x * _sigmoid(x)


def _rms(x, g):
    return x * lax.rsqrt(jnp.mean(x * x, axis=-1, keepdims=True) + RMS_EPS) * g


def _adaln_kernel(c_ref, w_ref, b_ref, o_ref):
    a = _silu(c_ref[...])
    o_ref[0] = _dot_hi(a, w_ref[0]) + b_ref[0]


def _adaln(cc, w_mod, b_mod):
    depth, d, d3 = w_mod.shape
    tn = 1024
    return pl.pallas_call(
        _adaln_kernel,
        out_shape=jax.ShapeDtypeStruct((depth, 8, d3), F32),
        grid=(depth, d3 // tn),
        in_specs=[
            pl.BlockSpec((8, d), lambda l, j: (0, 0)),
            pl.BlockSpec((1, d, tn), lambda l, j: (l, 0, j)),
            pl.BlockSpec((1, 1, tn), lambda l, j: (l, 0, j)),
        ],
        out_specs=pl.BlockSpec((1, 8, tn), lambda l, j: (l, 0, j)),
        compiler_params=_cparams(("parallel", "parallel")),
        name="adaln",
    )(cc, w_mod, b_mod.reshape(depth, 1, d3))


def _seg_row(blk, tiles_per_batch):
    return jnp.where(blk % tiles_per_batch == tiles_per_batch - 1, 0, 1 + blk // tiles_per_batch)


def _proj_kernel(z_ref, *refs, nsub):
    mod_refs = refs[:2 * nsub]
    g_ref, w_ref, o_ref, h_ref = refs[2 * nsub:]

    @pl.when(pl.program_id(1) == 0)
    def _():
        y = _rms(z_ref[...], g_ref[...])
        for s in range(nsub):
            rows = slice(s * ROW_BLK, (s + 1) * ROW_BLK)
            shift, scale = mod_refs[2 * s][0], mod_refs[2 * s + 1][0]
            h_ref[rows, :] = (y[rows, :] * (1.0 + scale) + shift).astype(BF16)

    o_ref[...] = _dot(h_ref[...], w_ref[...]).astype(o_ref.dtype)


def _proj_in(z2, mods, g_pre, w, *, tiles_per_batch):
    n, d = z2.shape
    tm = 1024 if n % 1024 == 0 else ROW_BLK
    tn = PCOLS // 4
    nsub = tm // ROW_BLK
    mod_specs = []
    for s in range(nsub):
        for col in (0, 1):
            mod_specs.append(pl.BlockSpec(
                (1, 1, d), lambda i, j, s=s, col=col: (_seg_row(i * nsub + s, tiles_per_batch), 0, col)))
    return pl.pallas_call(
        functools.partial(_proj_kernel, nsub=nsub),
        out_shape=jax.ShapeDtypeStruct((n, PCOLS), ACT),
        grid=(n // tm, PCOLS // tn),
        in_specs=[pl.BlockSpec((tm, d), lambda i, j: (i, 0))] + mod_specs + [
            pl.BlockSpec((1, d), lambda i, j: (0, 0)),
            pl.BlockSpec((d, tn), lambda i, j: (0, j)),
        ],
        out_specs=pl.BlockSpec((tm, tn), lambda i, j: (i, j)),
        scratch_shapes=[pltpu.VMEM((tm, d), BF16)],
        compiler_params=_cparams(("parallel", "arbitrary")),
        name="proj_in",
    )(z2, *([mods] * (2 * nsub)), g_pre, w)


def _mla_prep_kernel(qb_ref, kb_ref, cos_ref, sin_ref, gq_ref, gkv_ref, wq_ref, wk_ref, wv_ref,
                     vone_ref, qt_ref, k_ref, vt_ref):
    nh = N_HEADS_MLA
    hw = nh * HEAD_PAD
    qb = qb_ref[0].astype(F32)
    kb = kb_ref[0].astype(F32)
    cos8 = jnp.tile(cos_ref[...], (1, nh))
    sin8 = jnp.tile(sin_ref[...], (1, nh))
    qn = _rms(qb[:, :Q_LORA], gq_ref[...]).astype(BF16)
    qq = _dot(qn, wq_ref[...])
    q = (qq[:, :hw] * cos8 + qq[:, hw:] * sin8) * (ATTN_SCALE * LOG2E)
    kvn = _rms(kb[:, :KV_LORA], gkv_ref[...]).astype(BF16)
    kin = jnp.concatenate([kvn, qb[:, Q_LORA:].astype(BF16)], axis=1)
    kk = _dot(kin, wk_ref[...])
    k_ref[0] = (kk[:, :hw] * cos8 + kk[:, hw:] * sin8).astype(BF16)
    v = _dot(kvn, wv_ref[...]) + vone_ref[...]
    for h in range(nh):
        lanes = slice(h * HEAD_PAD, (h + 1) * HEAD_PAD)
        qt_ref[0, h] = q[:, lanes].T.astype(BF16)
        vt_ref[0, h] = v[:, lanes].T[:VT_ROWS].astype(BF16)


def _mla_prep(p3, cos_t, sin_t, g_q, g_kv, wq, wk, wv, vone):
    b, t, _ = p3.shape
    tm = ROW_BLK
    hw = N_HEADS_MLA * HEAD_PAD
    const = lambda shape: pl.BlockSpec(shape, lambda bi, i: (0,) * len(shape))
    return pl.pallas_call(
        _mla_prep_kernel,
        out_shape=(
            jax.ShapeDtypeStruct((b, N_HEADS_MLA, HEAD_PAD, t), BF16),
            jax.ShapeDtypeStruct((b, t, hw), BF16),
            jax.ShapeDtypeStruct((b, N_HEADS_MLA, VT_ROWS, t), BF16),
        ),
        grid=(b, t // tm),
        in_specs=[
            pl.BlockSpec((1, tm, 512), lambda bi, i: (bi, i, PB_Q)),
            pl.BlockSpec((1, tm, 512), lambda bi, i: (bi, i, PB_KV)),
            pl.BlockSpec((tm, HEAD_PAD), lambda bi, i: (i, 0)),
            pl.BlockSpec((tm, HEAD_PAD), lambda bi, i: (i, 0)),
            const((1, Q_LORA)),
            const((1, KV_LORA)),
            const(wq.shape),
            const(wk.shape),
            const(wv.shape),
            const((1, hw)),
        ],
        out_specs=(
            pl.BlockSpec((1, N_HEADS_MLA, HEAD_PAD, tm), lambda bi, i: (bi, 0, 0, i)),
            pl.BlockSpec((1, tm, hw), lambda bi, i: (bi, i, 0)),
            pl.BlockSpec((1, N_HEADS_MLA, VT_ROWS, tm), lambda bi, i: (bi, 0, 0, i)),
        ),
        compiler_params=_cparams(("parallel", "parallel")),
        name="mla_prep",
    )(p3, p3, cos_t, sin_t, g_q, g_kv, wq, wk, wv, vone)


def _attn_kernel(qt_ref, k_ref, vt_ref, *rest, nk):
    o_ref, m_ref, acc_ref = rest[-3:]
    j = pl.program_id(2)

    @pl.when(j == 0)
    def _():
        m_ref[...] = jnp.full(m_ref.shape, -jnp.inf, F32)
        acc_ref[...] = jnp.zeros(acc_ref.shape, F32)

    def scores(h):
        return _dot(k_ref[0, :, h * HEAD_PAD:(h + 1) * HEAD_PAD], qt_ref[0, h])

    s_next = scores(0)
    for h in range(N_HEADS_MLA):
        s = s_next
        if h + 1 < N_HEADS_MLA:
            s_next = scores(h + 1)
        m_prev = m_ref[h]
        m_new = jnp.maximum(m_prev, jnp.max(s, axis=0, keepdims=True))
        alpha = jnp.exp2(m_prev - m_new)
        p = jnp.exp2(s - m_new[0:1, :]).astype(BF16)
        acc_ref[h] = alpha[0:1, :] * acc_ref[h] + _dot(vt_ref[0, h], p)
        m_ref[h] = m_new

    @pl.when(j == nk - 1)
    def _():
        outs = []
        for h in range(N_HEADS_MLA):
            a = acc_ref[h]
            outs.append(a[:V_HEAD] * (1.0 / a[V_HEAD:V_HEAD + 1]))
        o_ref[0] = jnp.concatenate(outs, axis=0).T.astype(o_ref.dtype)


def _attention(qt, k, vt, y_prev, *, tq, tk, nq, nk, q_off, k_off):
    b, t, hw = k.shape
    in_specs = [
        pl.BlockSpec((1, N_HEADS_MLA, HEAD_PAD, tq), lambda bi, i, j: (bi, 0, 0, i + q_off)),
        pl.BlockSpec((1, tk, hw), lambda bi, i, j: (bi, j + k_off, 0)),
        pl.BlockSpec((1, N_HEADS_MLA, VT_ROWS, tk), lambda bi, i, j: (bi, 0, 0, j + k_off)),
    ]
    args = [qt, k, vt]
    aliases = {}
    if y_prev is not None:
        in_specs.append(pl.BlockSpec(memory_space=pl.ANY))
        args.append(y_prev)
        aliases = {3: 0}
    return pl.pallas_call(
        functools.partial(_attn_kernel, nk=nk),
        out_shape=jax.ShapeDtypeStruct((b, t, W_MLA), ACT),
        grid=(b, nq, nk),
        in_specs=in_specs,
        out_specs=pl.BlockSpec((1, tq, W_MLA), lambda bi, i, j: (bi, i + q_off, 0)),
        scratch_shapes=[
            pltpu.VMEM((N_HEADS_MLA, 8, tq), F32),
            pltpu.VMEM((N_HEADS_MLA, VT_ROWS, tq), F32),
        ],
        input_output_aliases=aliases,
        compiler_params=_cparams(("parallel", "parallel", "arbitrary")),
        name="attn_ctx" if y_prev is not None else "attn_x",
    )(*args)


def _half_lane_masks():
    lane = lax.broadcasted_iota(jnp.int32, (CHUNK, LANES), 1)
    first = lane < RWKV_HEAD
    return first, jnp.where(first, 1.0, 0.0).astype(BF16), jnp.where(first, 0.0, 1.0).astype(BF16)


def _bd4(x, lo, hi):
    xb = x.astype(BF16)
    xl, xr = xb[:, :LANES], xb[:, LANES:]
    z = jnp.zeros((2 * CHUNK, LANES), BF16)
    c0 = jnp.concatenate([xl * lo, xl * hi, z], axis=0)
    c1 = jnp.concatenate([z, xr * lo, xr * hi], axis=0)
    return jnp.concatenate([c0, c1], axis=1)


def _bd4_f32(x, first):
    xl, xr = x[:, :LANES], x[:, LANES:]
    z = jnp.zeros((2 * CHUNK, LANES), F32)
    c0 = jnp.concatenate([jnp.where(first, xl, 0.0), jnp.where(first, 0.0, xl), z], axis=0)
    c1 = jnp.concatenate([z, jnp.where(first, xr, 0.0), jnp.where(first, 0.0, xr)], axis=0)
    return jnp.concatenate([c0, c1], axis=1)


def _diag_blocks(full, first):
    c = CHUNK
    left = jnp.where(first, full[0:c, :LANES], full[c:2 * c, :LANES])
    right = jnp.where(first, full[2 * c:3 * c, LANES:], full[3 * c:4 * c, LANES:])
    return jnp.concatenate([left, right], axis=1)


def _rwkv_prep_kernel(rkv_ref, lora_ref, w0_ref, a0_ref, wup_ref, aup_ref, kk_ref, ka_ref, rk_ref,
                      ones_ref, rh_ref, y0_ref, g_ref, h_ref, bonus_ref):
    fwd = pl.program_id(0) == 0
    c = CHUNK
    w = W_RWKV
    rows = rkv_ref.shape[1]
    rkv = rkv_ref[0].astype(F32)
    r, k, v = rkv[:, :w], rkv[:, w:2 * w], rkv[:, 2 * w:]
    lora = lora_ref[0].astype(F32)
    zw = w0_ref[0] + _dot(jnp.tanh(lora[:, :W_LORA]).astype(BF16), wup_ref[0])
    ell = -math.exp(-0.5) * _sigmoid(zw)
    a = _sigmoid(a0_ref[0] + _dot(lora[:, W_LORA:].astype(BF16), aup_ref[0]))
    ones4 = ones_ref[...]

    def head_sum(x):
        hi, lo = _split2(x)
        parts = []
        for qd in range(w // QUAD):
            ls = slice(qd * QUAD, (qd + 1) * QUAD)
            parts.append(_dot(hi[:, ls], ones4) + _dot(lo[:, ls], ones4))
        return jnp.concatenate(parts, axis=1)

    kkr = k * kk_ref[...]
    kk = kkr * lax.rsqrt(head_sum(kkr * kkr) + L2_EPS)
    k_d = k * (1.0 + (a - 1.0) * ka_ref[...])
    bonus_ref[0, 0] = (head_sum(r * k_d * rk_ref[...]) * v).astype(bonus_ref.dtype)

    sgn = jnp.where(fwd, 1, -1)
    ti = lax.broadcasted_iota(jnp.int32, (rows, rows), 0)
    si = lax.broadcasted_iota(jnp.int32, (rows, rows), 1)
    same = jnp.where((ti // c) == (si // c), 1.0, 0.0)
    tri = jnp.where((si - ti) * sgn <= 0, same, 0.0).astype(BF16)
    same = same.astype(BF16)
    ell_hi, ell_lo = _split2(ell)
    lc = _dot(tri, ell_hi) + _dot(tri, ell_lo)
    ltot = _dot(same, ell_hi) + _dot(same, ell_lo)
    e_neg = jnp.exp(-lc)
    e_tail = jnp.exp(ltot - lc)
    gam = jnp.exp(ltot)
    kka = kk * a
    abar = kk * jnp.exp(lc - ell)
    bbar = kka * e_neg
    kbar = k_d * e_neg
    rbar = r * jnp.exp(lc)
    btil = kka * e_tail
    ktil = k_d * e_tail

    first, lo, hi = _half_lane_masks()
    tq = lax.broadcasted_iota(jnp.int32, (c, QUAD), 0)
    sq = lax.broadcasted_iota(jnp.int32, (c, QUAD), 1) % c
    before = (sq - tq) * sgn < 0
    upto = (sq - tq) * sgn <= 0
    eye = jnp.where(sq == tq, 1.0, 0.0)

    def pm(x, y):
        return _dot(x.astype(BF16), _bd4(y, lo, hi))

    sl = [(slice(ch * c, (ch + 1) * c), slice(qd * QUAD, (qd + 1) * QUAD))
          for ch in range(rows // c) for qd in range(w // QUAD)]
    la = [jnp.concatenate([abar[s], rbar[s]], axis=0).astype(BF16) for s in sl]
    nb = [_dot_nt(la_i, _bd4(bbar[s], lo, hi)) for la_i, s in zip(la, sl)]
    nk = [_dot_nt(la_i, _bd4(kbar[s], lo, hi)) for la_i, s in zip(la, sl)]
    n = [jnp.where(before, t[:c], 0.0) for t in nb]
    a_rb = [jnp.where(upto, t[c:], 0.0) for t in nb]
    a_ak = [jnp.where(before, t[:c], 0.0) for t in nk]
    a_rk = [jnp.where(upto, t[c:], 0.0) for t in nk]
    py = [pm(jnp.concatenate([u, r_], axis=0), v[s]) for u, r_, s in zip(a_ak, a_rk, sl)]
    x = [eye - t for t in n]
    p = [pm(t, t) for t in n]
    for it in range(5):
        if it < 4:
            xp = [pm(jnp.concatenate([x_i, p_i], axis=0), p_i) for x_i, p_i in zip(x, p)]
            x = [x_i + t[:c] for x_i, t in zip(x, xp)]
            p = [t[c:] for t in xp]
        else:
            x = [x_i + pm(x_i, p_i) for x_i, p_i in zip(x, p)]
    ah = [pm(x_i, abar[s]) for x_i, s in zip(x, sl)]
    u0 = [pm(x_i, t[:c]) for x_i, t in zip(x, py)]
    ra = [pm(m_i, t) for m_i, t in zip(a_rb, ah)]
    ru = [pm(m_i, t) for m_i, t in zip(a_rb, u0)]
    gfull = [_dot_tn(btil[s].astype(BF16), t.astype(BF16)) for s, t in zip(sl, ah)]
    hfull = [_dot_tn(jnp.concatenate([ktil[s], -btil[s]], axis=0).astype(BF16),
                     jnp.concatenate([v[s], t], axis=0).astype(BF16)) for s, t in zip(sl, u0)]
    for i, (rs, ls) in enumerate(sl):
        rh_ref[0, 0, rs, ls] = (rbar[rs, ls] - ra[i]).astype(BF16)
        y0_ref[0, 0, rs, ls] = py[i][c:] - ru[i]
        g_ref[0, 0, rs, ls] = (eye * gam[rs, ls] - _diag_blocks(gfull[i], first)).astype(g_ref.dtype)
        h_ref[0, 0, rs, ls] = _diag_blocks(hfull[i], first)


def _rwkv_prep(p3, w0, a0, w_up, a_up, k_k, k_a, r_k, ones4):
    b, t, _ = p3.shape
    w = W_RWKV
    rows = PREP_ROWS
    const = lambda shape: pl.BlockSpec(shape, lambda d, bi, i: (0,) * len(shape))
    perdir = lambda shape: pl.BlockSpec((1,) + shape, lambda d, bi, i: (d,) + (0,) * len(shape))
    out_spec = pl.BlockSpec((1, 1, rows, w), lambda d, bi, i: (d, bi, i, 0))
    f32_out = jax.ShapeDtypeStruct((2, b, t, w), F32)
    act_out = jax.ShapeDtypeStruct((2, b, t, w), ACT)
    return pl.pallas_call(
        _rwkv_prep_kernel,
        out_shape=(act_out, f32_out, act_out, f32_out, act_out),
        grid=(2, b, t // rows),
        in_specs=[
            pl.BlockSpec((1, rows, 3 * w), lambda d, bi, i: (bi, i, PB_RKV)),
            pl.BlockSpec((1, rows, 2 * W_LORA), lambda d, bi, i: (bi, i, PB_LORA128 + d)),
            perdir((1, w)), perdir((1, w)), perdir((W_LORA, w)), perdir((A_LORA, w)),
            const((1, w)), const((1, w)), const((1, w)),
            const((QUAD, QUAD)),
        ],
        out_specs=(out_spec,) * 5,
        compiler_params=_cparams(("parallel", "parallel", "parallel")),
        name="rwkv_prep",
    )(p3, p3, w0, a0, w_up, a_up, k_k, k_a, r_k, ones4)


def _rwkv_scan_kernel(rhf, y0f, gf, hf, rhb, y0b, gb, hb, yf_ref, yb_ref, st_ref, *, nb):
    @pl.when(pl.program_id(0) == 0)
    def _():
        st_ref[...] = jnp.zeros(st_ref.shape, F32)

    first, lo, hi = _half_lane_masks()
    c = CHUNK
    for d, (rh, y0, g, hh, y_ref) in enumerate(((rhf, y0f, gf, hf, yf_ref), (rhb, y0b, gb, hb, yb_ref))):
        for bi in range(nb):
            for qd in range(W_RWKV // QUAD):
                ls = slice(qd * QUAD, (qd + 1) * QUAD)
                lhs = jnp.concatenate([rh[0, bi, :, ls], _bd4(g[0, bi, :, ls], lo, hi)], axis=0)
                res = _dot(lhs, st_ref[d, bi, qd].astype(BF16))
                y_ref[bi, :, ls] = (res[:c] + y0[0, bi, :, ls]).astype(y_ref.dtype)
                st_ref[d, bi, qd] = res[c:] + _bd4_f32(hh[0, bi, :, ls], first)


def _rwkv_scan(rh, y0, g, h, *, n_x, n_ctx):
    _, b, t, w = rh.shape
    c = CHUNK
    n = n_x + n_ctx
    fidx = lambda j: jnp.where(j < n_ctx, n_x + j, j - n_ctx)
    bidx = lambda j: n - 1 - j
    fspec = pl.BlockSpec((1, b, c, w), lambda j: (0, 0, fidx(j), 0))
    bspec = pl.BlockSpec((1, b, c, w), lambda j: (1, 0, bidx(j), 0))
    y_shape = jax.ShapeDtypeStruct((b, t, w), ACT)
    return pl.pallas_call(
        functools.partial(_rwkv_scan_kernel, nb=b),
        out_shape=(y_shape, y_shape),
        grid=(n,),
        in_specs=[fspec] * 4 + [bspec] * 4,
        out_specs=(
            pl.BlockSpec((b, c, w), lambda j: (0, fidx(j), 0)),
            pl.BlockSpec((b, c, w), lambda j: (0, bidx(j), 0)),
        ),
        scratch_shapes=[pltpu.VMEM((2, b, w // QUAD, QUAD, QUAD), F32)],
        compiler_params=_cparams(("arbitrary",)),
        name="rwkv_scan",
    )(rh, y0, g, h, rh, y0, g, h)


def _merge_kernel(z_ref, gate_ref, gl_ref, grw_ref, gml_ref, cin_ref, cb_ref, cc_ref, gcv_ref,
                  cin_p, cc_p, cin_n, cc_n, ymla_ref, yf_ref, yb_ref, bon_ref,
                  wbm_ref, wbc_ref, wbr_ref, wout_ref, gpost_ref, cw_ref, cbias_ref, gng_ref, gnb_ref,
                  avg_ref, o_ref, *, tiles_per_batch, ctx_tiles):
    tm = z_ref.shape[0]
    d = D_MODEL
    pos = pl.program_id(0) % tiles_per_batch
    x_tiles = tiles_per_batch - ctx_tiles
    first = jnp.logical_or(pos == 0, pos == x_tiles)
    last = jnp.logical_or(pos == x_tiles - 1, pos == tiles_per_batch - 1)

    def f32(ref, idx=slice(None)):
        return ref[idx].astype(F32)

    u = f32(cc_ref) * f32(cin_ref)
    hl = HALO_ROWS - 1
    u_halo_p = jnp.where(first, 0.0, f32(cc_p, slice(hl, hl + 1)) * f32(cin_p, slice(hl, hl + 1)))
    u_halo_n = jnp.where(last, 0.0, f32(cc_n, slice(0, 1)) * f32(cin_n, slice(0, 1)))
    row = lax.broadcasted_iota(jnp.int32, u.shape, 0)
    u_prev = jnp.where(row == 0, u_halo_p, pltpu.roll(u, 1, axis=0))
    u_next = jnp.where(row == tm - 1, u_halo_n, pltpu.roll(u, tm - 1, axis=0))
    cw = cw_ref[...]
    y_conv = f32(cb_ref) * (u_prev * cw[0:1] + u * cw[1:2] + u_next * cw[2:3] + cbias_ref[...])

    avg = avg_ref[...]
    yr = f32(yf_ref) + f32(yb_ref)
    mu = _dot_exact_rhs(yr, avg)
    dv = yr - mu
    var = _dot_exact_rhs(dv * dv, avg)
    y_rwkv = dv * lax.rsqrt(var + GN_EPS) * gng_ref[...] + gnb_ref[...] + f32(bon_ref, 0) + f32(bon_ref, 1)

    br_mla = _dot((f32(ymla_ref) * _silu(f32(gml_ref))).astype(BF16), wbm_ref[...])
    br_conv = _dot((y_conv * _silu(f32(gcv_ref))).astype(BF16), wbc_ref[...])
    br_rwkv = _dot((y_rwkv * _silu(f32(grw_ref))).astype(BF16), wbr_ref[...])
    s = _sigmoid(f32(gl_ref))
    merged = s[:, :d] * br_mla + s[:, d:2 * d] * br_conv + s[:, 2 * d:] * br_rwkv
    o = _dot(merged.astype(BF16), wout_ref[...])
    o_ref[...] = z_ref[...] + gate_ref[0] * _rms(o, gpost_ref[...])


def _merge(z2, mods, p2, y_mla, yf, yb, bonus, wbm, wbc, wbr, wout, g_post, conv_w, conv_b,
           gn_g, gn_b, avg_bd, *, tiles_per_batch, ctx_tiles):
    n, d = z2.shape
    tm = ROW_BLK
    hb = tm // HALO_ROWS
    nhb = n // HALO_ROWS
    pcol = lambda blk: pl.BlockSpec((tm, 512), lambda i: (i, blk))
    prev = lambda blk: pl.BlockSpec((HALO_ROWS, 512), lambda i: (jnp.maximum(i * hb - 1, 0), blk))
    nxt = lambda blk: pl.BlockSpec((HALO_ROWS, 512), lambda i: (jnp.minimum((i + 1) * hb, nhb - 1), blk))
    const = lambda shape: pl.BlockSpec(shape, lambda i: (0,) * len(shape))
    row512 = pl.BlockSpec((tm, 512), lambda i: (i, 0))
    return pl.pallas_call(
        functools.partial(_merge_kernel, tiles_per_batch=tiles_per_batch, ctx_tiles=ctx_tiles),
        out_shape=jax.ShapeDtypeStruct((n, d), F32),
        grid=(n // tm,),
        in_specs=[
            pl.BlockSpec((tm, d), lambda i: (i, 0)),
            pl.BlockSpec((1, 1, d), lambda i: (_seg_row(i, tiles_per_batch), 0, 2)),
            pl.BlockSpec((tm, 3 * d), lambda i: (i, PB_GATE)),
            pcol(PB_GRWKV), pcol(PB_GMLA), pcol(PB_CVIN), pcol(PB_CVB), pcol(PB_CVC), pcol(PB_GCONV),
            prev(PB_CVIN), prev(PB_CVC), nxt(PB_CVIN), nxt(PB_CVC),
            row512, row512, row512,
            pl.BlockSpec((2, tm, 512), lambda i: (0, i, 0)),
            const(wbm.shape), const(wbc.shape), const(wbr.shape), const(wout.shape),
            const((1, d)), const(conv_w.shape), const((1, 512)), const((1, 512)), const((1, 512)),
            const(avg_bd.shape),
        ],
        out_specs=pl.BlockSpec((tm, d), lambda i: (i, 0)),
        compiler_params=_cparams(("parallel",)),
        name="merge",
    )(z2, mods, p2, p2, p2, p2, p2, p2, p2, p2, p2, p2, p2, y_mla, yf, yb, bonus,
      wbm, wbc, wbr, wout, g_post, conv_w, conv_b, gn_g, gn_b, avg_bd)


def _pair_swap(w):
    s = w.shape
    return w.reshape(s[:-1] + (s[-1] // 2, 2))[..., ::-1].reshape(s)


def _layout_w_in(w_in):
    sizes = (Q_LORA, KV_LORA, QK_ROPE, W_MLA, CONV_W, CONV_W, CONV_W, CONV_W, W_RWKV, W_RWKV, W_RWKV,
             W_LORA, W_LORA, A_LORA, A_LORA, W_RWKV, 3 * D_MODEL)
    offs = np.concatenate([[0], np.cumsum(sizes)])
    names = ("q_lat", "kv_lat", "kr", "g_mla", "cv_in", "cv_b", "cv_c", "g_conv", "r", "k", "v",
             "wd_f", "wd_b", "ad_f", "ad_b", "g_rwkv", "gl")
    col = {nm: w_in[..., offs[i]:offs[i + 1]] for i, nm in enumerate(names)}
    zeros = jnp.zeros(w_in.shape[:-1] + (512 - Q_LORA - 2 * QK_ROPE,), w_in.dtype)
    parts = [col["gl"], col["r"], col["k"], col["v"], col["g_rwkv"], col["g_mla"], col["cv_in"],
             col["cv_b"], col["cv_c"], col["g_conv"],
             col["q_lat"], col["kr"], _pair_swap(col["kr"]), zeros,
             col["kv_lat"], col["wd_f"], col["ad_f"], col["wd_b"], col["ad_b"]]
    out = jnp.concatenate(parts, axis=-1).astype(BF16)
    assert out.shape[-1] == PCOLS
    return out


def _layout_mla_weights(w_uq, w_ukv):
    depth = w_uq.shape[0]
    nh = N_HEADS_MLA
    wq = w_uq.reshape(depth, Q_LORA, nh, QK_HEAD)
    q_nope, q_rope = wq[..., :QK_NOPE], wq[..., QK_NOPE:]
    zq = jnp.zeros((depth, Q_LORA, nh, HEAD_PAD - QK_HEAD), w_uq.dtype)
    wq_a = jnp.concatenate([q_nope, q_rope, zq], axis=-1)
    wq_b = jnp.concatenate([jnp.zeros_like(q_nope), _pair_swap(q_rope), zq], axis=-1)
    wq_all = jnp.concatenate([wq_a.reshape(depth, Q_LORA, nh * HEAD_PAD),
                              wq_b.reshape(depth, Q_LORA, nh * HEAD_PAD)], axis=-1).astype(BF16)

    wkv = w_ukv.reshape(depth, KV_LORA, nh, QK_NOPE + V_HEAD)
    k_nope, v_w = wkv[..., :QK_NOPE], wkv[..., QK_NOPE:]
    zk = jnp.zeros((depth, KV_LORA, nh, HEAD_PAD - QK_NOPE), w_ukv.dtype)
    wk_top = jnp.concatenate([k_nope, zk], axis=-1).reshape(depth, KV_LORA, nh * HEAD_PAD)
    wv = jnp.concatenate([v_w, zk], axis=-1).reshape(depth, KV_LORA, nh * HEAD_PAD).astype(BF16)
    place = np.zeros((HEAD_PAD, HEAD_PAD), np.float32)
    place[np.arange(QK_ROPE), QK_NOPE + np.arange(QK_ROPE)] = 1.0
    e_a = np.tile(place, (1, nh))
    place_b = np.zeros((HEAD_PAD, HEAD_PAD), np.float32)
    place_b[QK_ROPE + np.arange(QK_ROPE), QK_NOPE + np.arange(QK_ROPE)] = 1.0
    e_b = np.tile(place_b, (1, nh))
    top = jnp.concatenate([wk_top, jnp.zeros_like(wk_top)], axis=-1)
    bot = jnp.broadcast_to(jnp.asarray(np.concatenate([e_a, e_b], axis=1)), (depth, HEAD_PAD, 2 * nh * HEAD_PAD))
    wk_all = jnp.concatenate([top, bot.astype(top.dtype)], axis=1).astype(BF16)
    return wq_all, wk_all, wv


def _rope_tables(seq, ctx_len):
    n_freq = QK_ROPE // 4
    pos = np.arange(seq)
    inv = ROPE_THETA ** (-np.arange(n_freq, dtype=np.float32) / n_freq)
    row = (pos // GRID_W).astype(np.float32)
    colp = (pos % GRID_W).astype(np.float32)
    ang = jnp.concatenate([jnp.asarray(row)[:, None] * jnp.asarray(inv), jnp.asarray(colp)[:, None] * jnp.asarray(inv)], axis=-1)
    cos, sin = jnp.cos(ang), jnp.sin(ang)
    cos2 = jnp.repeat(cos, 2, axis=-1)
    sin2 = jnp.stack([-sin, sin], axis=-1).reshape(seq, QK_ROPE)
    ones = jnp.ones((seq, QK_NOPE), F32)
    pad = jnp.zeros((seq, HEAD_PAD - QK_HEAD), F32)
    cos_x = jnp.concatenate([ones, cos2, pad], axis=-1)
    sin_x = jnp.concatenate([jnp.zeros_like(ones), sin2, pad], axis=-1)
    cos_c = jnp.concatenate([jnp.ones((ctx_len, QK_HEAD), F32), jnp.zeros((ctx_len, HEAD_PAD - QK_HEAD), F32)], axis=-1)
    sin_c = jnp.zeros((ctx_len, HEAD_PAD), F32)
    return jnp.concatenate([cos_x, cos_c], axis=0), jnp.concatenate([sin_x, sin_c], axis=0)


def _block_diag_const(n, blk, value):
    i = np.arange(n)
    return np.where((i[:, None] // blk) == (i[None, :] // blk), value, 0.0).astype(np.float32)


def _pick(n, candidates):
    for cand in candidates:
        if n % cand == 0:
            return cand
    raise ValueError(f"no tile for {n}")


def kernel(x, c, ctx, c_ctx, w_mod, b_mod, g_pre, g_post, w_in, g_q, g_kv, w_uq, w_ukv, conv_w, conv_b,
           w0, w_up, a0, a_up, k_k, k_a, r_k, gn_g, gn_b, w_br_mla, w_br_conv, w_br_rwkv, w_out):
    bsz, seq, d = x.shape
    ctx_len = ctx.shape[1]
    depth = w_mod.shape[0]
    assert d == D_MODEL and ctx_len == ROW_BLK and seq % ROW_BLK == 0
    t = seq + ctx_len
    n = bsz * t
    tiles_per_batch = t // ROW_BLK

    cc = jnp.zeros((8, d), F32).at[0].set(c_ctx).at[1:1 + bsz].set(c)
    mods = _adaln(cc, w_mod, b_mod).reshape(depth, 8, 1, 3 * d)

    w_in_p = _layout_w_in(w_in)
    wq_all, wk_all, wv_all = _layout_mla_weights(w_uq, w_ukv)
    cos_t, sin_t = _rope_tables(seq, ctx_len)
    vone = np.zeros((1, N_HEADS_MLA * HEAD_PAD), np.float32)
    vone[0, V_HEAD::HEAD_PAD] = 1.0
    vone = jnp.asarray(vone)
    ones4 = jnp.asarray(_block_diag_const(QUAD, RWKV_HEAD, 1.0), BF16)
    avg_bd = jnp.asarray(_block_diag_const(W_RWKV, RWKV_HEAD, 1.0 / RWKV_HEAD), BF16)

    tq = _pick(seq, (1024, 512, 256))
    tk = _pick(t, (768, 512, 256))

    z = jnp.concatenate([x, ctx], axis=1)
    for l in range(depth):
        z2 = z.reshape(n, d)
        p2 = _proj_in(z2, mods[l], g_pre[l][None], w_in_p[l], tiles_per_batch=tiles_per_batch)
        p3 = p2.reshape(bsz, t, PCOLS)

        qt, k, vt = _mla_prep(p3, cos_t, sin_t, g_q[l][None], g_kv[l][None], wq_all[l], wk_all[l], wv_all[l], vone)
        y_mla = _attention(qt, k, vt, None, tq=tq, tk=tk, nq=seq // tq, nk=t // tk, q_off=0, k_off=0)
        y_mla = _attention(qt, k, vt, y_mla, tq=ctx_len, tk=ctx_len, nq=1, nk=1,
                           q_off=seq // ctx_len, k_off=seq // ctx_len)

        rh, y0, g, h, bonus = _rwkv_prep(
            p3, w0[l][:, None], a0[l][:, None], w_up[l].astype(BF16), a_up[l].astype(BF16),
            k_k[l][None], k_a[l][None], r_k[l].reshape(1, W_RWKV), ones4)
        yf, yb = _rwkv_scan(rh, y0, g, h, n_x=seq // CHUNK, n_ctx=ctx_len // CHUNK)

        z2 = _merge(z2, mods[l], p2, y_mla.reshape(n, W_MLA), yf.reshape(n, W_RWKV), yb.reshape(n, W_RWKV),
                    bonus.reshape(2, n, W_RWKV),
                    w_br_mla[l].astype(BF16), w_br_conv[l].astype(BF16), w_br_rwkv[l].astype(BF16),
                    w_out[l].astype(BF16), g_post[l][None], conv_w[l], conv_b[l][None],
                    gn_g[l][None], gn_b[l][None], avg_bd,
                    tiles_per_batch=tiles_per_batch, ctx_tiles=ctx_len // ROW_BLK)
        z = z2.reshape(bsz, t, d)
    return z[:, :seq]
```

```python
import functools
import math

import numpy as np
import jax
import jax.numpy as jnp
from jax import lax
from jax.experimental import pallas as pl
from jax.experimental.pallas import tpu as pltpu

F32 = jnp.float32
BF16 = jnp.bfloat16
ACT = BF16

D_MODEL = 1024
DEPTH = 4
GRID_W = 64
N_HEADS_MLA = 8
Q_LORA = 384
KV_LORA = 256
QK_NOPE = 64
QK_ROPE = 32
QK_HEAD = QK_NOPE + QK_ROPE
V_HEAD = 64
W_MLA = N_HEADS_MLA * V_HEAD
ROPE_THETA = 10000.0
ATTN_SCALE = QK_HEAD ** -0.5
CONV_W = 512
RWKV_HEADS = 8
RWKV_HEAD = 64
W_RWKV = RWKV_HEADS * RWKV_HEAD
W_LORA = 64
A_LORA = 64
RMS_EPS = 1e-6
GN_EPS = 64e-5
L2_EPS = 1e-12
LOG2E = math.log2(math.e)

LANES = 128
ROW_BLK = 256
CHUNK = 64
QUAD = 4 * RWKV_HEAD
PREP_GROUP = 256
PREP_ROWS = 512
HEAD_PAD = 128
HALO_ROWS = 16
VT_ROWS = 80
VMEM_LIMIT = 48 * 1024 * 1024

PCOLS = 17 * 512
PB_GATE = 0
PB_RKV = 2
PB_GRWKV = 9
PB_GMLA = 10
PB_CVIN = 11
PB_CVB = 12
PB_CVC = 13
PB_GCONV = 14
PB_Q = 15
PB_KV = 16
PB_LORA128 = (16 * 512 + 256) // 128


def _cparams(sem, vmem=VMEM_LIMIT):
    return pltpu.CompilerParams(dimension_semantics=sem, vmem_limit_bytes=vmem)


def _dot(a, b):
    return jnp.dot(a, b, preferred_element_type=F32)


def _dot_nt(a, b):
    return lax.dot_general(a, b, (((1,), (1,)), ((), ())), preferred_element_type=F32)


def _dot_tn(a, b):
    return lax.dot_general(a, b, (((0,), (0,)), ((), ())), preferred_element_type=F32)


def _split2(x):
    hi = x.astype(BF16)
    lo = (x - hi.astype(F32)).astype(BF16)
    return hi, lo


def _split3(x):
    hi = x.astype(BF16)
    r1 = x - hi.astype(F32)
    mid = r1.astype(BF16)
    lo = (r1 - mid.astype(F32)).astype(BF16)
    return hi, mid, lo


def _dot_exact_rhs(x, m_bf16):
    hi, mid, lo = _split3(x)
    return _dot(hi, m_bf16) + _dot(mid, m_bf16) + _dot(lo, m_bf16)


def _dot_hi(a, b):
    ah, al = _split2(a)
    bh, bl = _split2(b)
    return _dot(ah, bh) + _dot(ah, bl) + _dot(al, bh)


def _sigmoid(x):
    return 1.0 / (1.0 + jnp.exp(-x))


def _silu(x):
    return x * _sigmoid(x)


def _rms(x, g):
    return x * lax.rsqrt(jnp.mean(x * x, axis=-1, keepdims=True) + RMS_EPS) * g


def _adaln_kernel(c_ref, w_ref, b_ref, o_ref):
    a = _silu(c_ref[...])
    o_ref[0] = _dot_hi(a, w_ref[0]) + b_ref[0]


def _adaln(cc, w_mod, b_mod):
    depth, d, d3 = w_mod.shape
    tn = 1024
    return pl.pallas_call(
        _adaln_kernel,
        out_shape=jax.ShapeDtypeStruct((depth, 8, d3), F32),
        grid=(depth, d3 // tn),
        in_specs=[
            pl.BlockSpec((8, d), lambda l, j: (0, 0)),
            pl.BlockSpec((1, d, tn), lambda l, j: (l, 0, j)),
            pl.BlockSpec((1, 1, tn), lambda l, j: (l, 0, j)),
        ],
        out_specs=pl.BlockSpec((1, 8, tn), lambda l, j: (l, 0, j)),
        compiler_params=_cparams(("parallel", "parallel")),
        name="adaln",
    )(cc, w_mod, b_mod.reshape(depth, 1, d3))


def _seg_row(blk, tiles_per_batch):
    return jnp.where(blk % tiles_per_batch == tiles_per_batch - 1, 0, 1 + blk // tiles_per_batch)


def _proj_kernel(z_ref, *refs, nsub):
    mod_refs = refs[:2 * nsub]
    g_ref, w_ref, o_ref, h_ref = refs[2 * nsub:]

    @pl.when(pl.program_id(1) == 0)
    def _():
        y = _rms(z_ref[...], g_ref[...])
        for s in range(nsub):
            rows = slice(s * ROW_BLK, (s + 1) * ROW_BLK)
            shift, scale = mod_refs[2 * s][0], mod_refs[2 * s + 1][0]
            h_ref[rows, :] = (y[rows, :] * (1.0 + scale) + shift).astype(BF16)

    o_ref[...] = _dot(h_ref[...], w_ref[...]).astype(o_ref.dtype)


def _proj_in(z2, mods, g_pre, w, *, tiles_per_batch):
    n, d = z2.shape
    tm = 1024 if n % 1024 == 0 else ROW_BLK
    tn = PCOLS // 4
    nsub = tm // ROW_BLK
    mod_specs = []
    for s in range(nsub):
        for col in (0, 1):
            mod_specs.append(pl.BlockSpec(
                (1, 1, d), lambda i, j, s=s, col=col: (_seg_row(i * nsub + s, tiles_per_batch), 0, col)))
    return pl.pallas_call(
        functools.partial(_proj_kernel, nsub=nsub),
        out_shape=jax.ShapeDtypeStruct((n, PCOLS), ACT),
        grid=(n // tm, PCOLS // tn),
        in_specs=[pl.BlockSpec((tm, d), lambda i, j: (i, 0))] + mod_specs + [
            pl.BlockSpec((1, d), lambda i, j: (0, 0)),
            pl.BlockSpec((d, tn), lambda i, j: (0, j)),
        ],
        out_specs=pl.BlockSpec((tm, tn), lambda i, j: (i, j)),
        scratch_shapes=[pltpu.VMEM((tm, d), BF16)],
        compiler_params=_cparams(("parallel", "arbitrary")),
        name="proj_in",
    )(z2, *([mods] * (2 * nsub)), g_pre, w)


def _mla_prep_kernel(qb_ref, kb_ref, cos_ref, sin_ref, gq_ref, gkv_ref, wq_ref, wk_ref, wv_ref,
                     vone_ref, qt_ref, k_ref, vt_ref):
    nh = N_HEADS_MLA
    hw = nh * HEAD_PAD
    qb = qb_ref[0].astype(F32)
    kb = kb_ref[0].astype(F32)
    cos8 = jnp.tile(cos_ref[...], (1, nh))
    sin8 = jnp.tile(sin_ref[...], (1, nh))
    qn = _rms(qb[:, :Q_LORA], gq_ref[...]).astype(BF16)
    qq = _dot(qn, wq_ref[...])
    q = (qq[:, :hw] * cos8 + qq[:, hw:] * sin8) * (ATTN_SCALE * LOG2E)
    kvn = _rms(kb[:, :KV_LORA], gkv_ref[...]).astype(BF16)
    kin = jnp.concatenate([kvn, qb[:, Q_LORA:].astype(BF16)], axis=1)
    kk = _dot(kin, wk_ref[...])
    k_ref[0] = (kk[:, :hw] * cos8 + kk[:, hw:] * sin8).astype(BF16)
    v = _dot(kvn, wv_ref[...]) + vone_ref[...]
    for h in range(nh):
        lanes = slice(h * HEAD_PAD, (h + 1) * HEAD_PAD)
        qt_ref[0, h] = q[:, lanes].T.astype(BF16)
        vt_ref[0, h] = v[:, lanes].T[:VT_ROWS].astype(BF16)


def _mla_prep(p3, cos_t, sin_t, g_q, g_kv, wq, wk, wv, vone):
    b, t, _ = p3.shape
    tm = ROW_BLK
    hw = N_HEADS_MLA * HEAD_PAD
    const = lambda shape: pl.BlockSpec(shape, lambda bi, i: (0,) * len(shape))
    return pl.pallas_call(
        _mla_prep_kernel,
        out_shape=(
            jax.ShapeDtypeStruct((b, N_HEADS_MLA, HEAD_PAD, t), BF16),
            jax.ShapeDtypeStruct((b, t, hw), BF16),
            jax.ShapeDtypeStruct((b, N_HEADS_MLA, VT_ROWS, t), BF16),
        ),
        grid=(b, t // tm),
        in_specs=[
            pl.BlockSpec((1, tm, 512), lambda bi, i: (bi, i, PB_Q)),
            pl.BlockSpec((1, tm, 512), lambda bi, i: (bi, i, PB_KV)),
            pl.BlockSpec((tm, HEAD_PAD), lambda bi, i: (i, 0)),
            pl.BlockSpec((tm, HEAD_PAD), lambda bi, i: (i, 0)),
            const((1, Q_LORA)),
            const((1, KV_LORA)),
            const(wq.shape),
            const(wk.shape),
            const(wv.shape),
            const((1, hw)),
        ],
        out_specs=(
            pl.BlockSpec((1, N_HEADS_MLA, HEAD_PAD, tm), lambda bi, i: (bi, 0, 0, i)),
            pl.BlockSpec((1, tm, hw), lambda bi, i: (bi, i, 0)),
            pl.BlockSpec((1, N_HEADS_MLA, VT_ROWS, tm), lambda bi, i: (bi, 0, 0, i)),
        ),
        compiler_params=_cparams(("parallel", "parallel")),
        name="mla_prep",
    )(p3, p3, cos_t, sin_t, g_q, g_kv, wq, wk, wv, vone)


def _attn_kernel(qt_ref, k_ref, vt_ref, *rest, nk):
    o_ref, m_ref, acc_ref = rest[-3:]
    j = pl.program_id(2)

    @pl.when(j == 0)
    def _():
        m_ref[...] = jnp.full(m_ref.shape, -jnp.inf, F32)
        acc_ref[...] = jnp.zeros(acc_ref.shape, F32)

    def scores(h):
        return _dot(k_ref[0, :, h * HEAD_PAD:(h + 1) * HEAD_PAD], qt_ref[0, h])

    s_next = scores(0)
    for h in range(N_HEADS_MLA):
        s = s_next
        if h + 1 < N_HEADS_MLA:
            s_next = scores(h + 1)
        m_prev = m_ref[h]
        m_new = jnp.maximum(m_prev, jnp.max(s, axis=0, keepdims=True))
        alpha = jnp.exp2(m_prev - m_new)
        p = jnp.exp2(s - m_new[0:1, :]).astype(BF16)
        acc_ref[h] = alpha[0:1, :] * acc_ref[h] + _dot(vt_ref[0, h], p)
        m_ref[h] = m_new

    @pl.when(j == nk - 1)
    def _():
        outs = []
        for h in range(N_HEADS_MLA):
            a = acc_ref[h]
            outs.append(a[:V_HEAD] * (1.0 / a[V_HEAD:V_HEAD + 1]))
        o_ref[0] = jnp.concatenate(outs, axis=0).T.astype(o_ref.dtype)


def _attention(qt, k, vt, y_prev, *, tq, tk, nq, nk, q_off, k_off):
    b, t, hw = k.shape
    in_specs = [
        pl.BlockSpec((1, N_HEADS_MLA, HEAD_PAD, tq), lambda bi, i, j: (bi, 0, 0, i + q_off)),
        pl.BlockSpec((1, tk, hw), lambda bi, i, j: (bi, j + k_off, 0)),
        pl.BlockSpec((1, N_HEADS_MLA, VT_ROWS, tk), lambda bi, i, j: (bi, 0, 0, j + k_off)),
    ]
    args = [qt, k, vt]
    aliases = {}
    if y_prev is not None:
        in_specs.append(pl.BlockSpec(memory_space=pl.ANY))
        args.append(y_prev)
        aliases = {3: 0}
    return pl.pallas_call(
        functools.partial(_attn_kernel, nk=nk),
        out_shape=jax.ShapeDtypeStruct((b, t, W_MLA), ACT),
        grid=(b, nq, nk),
        in_specs=in_specs,
        out_specs=pl.BlockSpec((1, tq, W_MLA), lambda bi, i, j: (bi, i + q_off, 0)),
        scratch_shapes=[
            pltpu.VMEM((N_HEADS_MLA, 8, tq), F32),
            pltpu.VMEM((N_HEADS_MLA, VT_ROWS, tq), F32),
        ],
        input_output_aliases=aliases,
        compiler_params=_cparams(("parallel", "parallel", "arbitrary")),
        name="attn_ctx" if y_prev is not None else "attn_x",
    )(*args)


def _half_lane_masks():
    lane = lax.broadcasted_iota(jnp.int32, (CHUNK, LANES), 1)
    first = lane < RWKV_HEAD
    return first, jnp.where(first, 1.0, 0.0).astype(BF16), jnp.where(first, 0.0, 1.0).astype(BF16)


def _bd4(x, lo, hi):
    xb = x.astype(BF16)
    xl, xr = xb[:, :LANES], xb[:, LANES:]
    z = jnp.zeros((2 * CHUNK, LANES), BF16)
    c0 = jnp.concatenate([xl * lo, xl * hi, z], axis=0)
    c1 = jnp.concatenate([z, xr * lo, xr * hi], axis=0)
    return jnp.concatenate([c0, c1], axis=1)


def _bd4_f32(x, first):
    xl, xr = x[:, :LANES], x[:, LANES:]
    z = jnp.zeros((2 * CHUNK, LANES), F32)
    c0 = jnp.concatenate([jnp.where(first, xl, 0.0), jnp.where(first, 0.0, xl), z], axis=0)
    c1 = jnp.concatenate([z, jnp.where(first, xr, 0.0), jnp.where(first, 0.0, xr)], axis=0)
    return jnp.concatenate([c0, c1], axis=1)


def _diag_blocks(full, first):
    c = CHUNK
    left = jnp.where(first, full[0:c, :LANES], full[c:2 * c, :LANES])
    right = jnp.where(first, full[2 * c:3 * c, LANES:], full[3 * c:4 * c, LANES:])
    return jnp.concatenate([left, right], axis=1)


def _rwkv_prep_kernel(rkv_ref, lora_ref, w0_ref, a0_ref, wup_ref, aup_ref, kk_ref, ka_ref, rk_ref,
                      ones_ref, rh_ref, y0_ref, g_ref, h_ref, bonus_ref):
    fwd = pl.program_id(0) == 0
    c = CHUNK
    w = W_RWKV
    grp = PREP_GROUP
    ones4 = ones_ref[...]
    sgn = jnp.where(fwd, 1, -1)
    ti = lax.broadcasted_iota(jnp.int32, (grp, grp), 0)
    si = lax.broadcasted_iota(jnp.int32, (grp, grp), 1)
    same = jnp.where((ti // c) == (si // c), 1.0, 0.0)
    tri = jnp.where((si - ti) * sgn <= 0, same, 0.0).astype(BF16)
    first, lo, hi = _half_lane_masks()
    tq = lax.broadcasted_iota(jnp.int32, (c, QUAD), 0)
    sq = lax.broadcasted_iota(jnp.int32, (c, QUAD), 1) % c
    before = (sq - tq) * sgn < 0
    upto = (sq - tq) * sgn <= 0
    eye = jnp.where(sq == tq, 1.0, 0.0)

    def head_sum(x):
        xh, xl = _split2(x)
        parts = []
        for qd in range(w // QUAD):
            ls = slice(qd * QUAD, (qd + 1) * QUAD)
            parts.append(_dot(xh[:, ls], ones4) + _dot(xl[:, ls], ones4))
        return jnp.concatenate(parts, axis=1)

    def pm(x, y):
        return _dot(x.astype(BF16), _bd4(y, lo, hi))

    def prologue(g, out):
        gs = slice(g * grp, (g + 1) * grp)
        rkv = rkv_ref[gs, :].astype(F32)
        r, k, v = rkv[:, :w], rkv[:, w:2 * w], rkv[:, 2 * w:]
        lora = lora_ref[gs, :].astype(F32)
        zw = w0_ref[0] + _dot(jnp.tanh(lora[:, :W_LORA]).astype(BF16), wup_ref[0])
        za = a0_ref[0] + _dot(lora[:, W_LORA:].astype(BF16), aup_ref[0])
        yield
        ell = -math.exp(-0.5) * _sigmoid(zw)
        a = _sigmoid(za)
        kkr = k * kk_ref[...]
        k_d = k * (1.0 + (a - 1.0) * ka_ref[...])
        kk_ss = head_sum(kkr * kkr)
        rk_s = head_sum(r * k_d * rk_ref[...])
        yield
        kk = kkr * lax.rsqrt(kk_ss + L2_EPS)
        bonus_ref[0, gs, :] = (rk_s * v).astype(bonus_ref.dtype)
        ell_hi, ell_lo = _split2(ell)
        lc = _dot(tri, ell_hi) + _dot(tri, ell_lo)
        yield
        ltot = jnp.concatenate(
            [jnp.broadcast_to(jnp.where(fwd, lc[ch * c + c - 1:ch * c + c], lc[ch * c:ch * c + 1]), (c, w))
             for ch in range(grp // c)], axis=0)
        e_neg = jnp.exp(-lc)
        e_tail = jnp.exp(ltot - lc)
        kka = kk * a
        out.update(abar=kk * jnp.exp(lc - ell), bbar=kka * e_neg, kbar=k_d * e_neg, rbar=r * jnp.exp(lc),
                   btil=kka * e_tail, ktil=k_d * e_tail, v=v, gam=jnp.exp(ltot))

    def chains(g, q):
        sl = [(slice(ch * c, (ch + 1) * c), slice(qd * QUAD, (qd + 1) * QUAD))
              for ch in range(grp // c) for qd in range(w // QUAD)]
        abar, bbar, kbar, rbar, btil, ktil, v, gam = (
            q[nm] for nm in ("abar", "bbar", "kbar", "rbar", "btil", "ktil", "v", "gam"))
        la = [jnp.concatenate([abar[s], rbar[s]], axis=0).astype(BF16) for s in sl]
        nb = [_dot_nt(la_i, _bd4(bbar[s], lo, hi)) for la_i, s in zip(la, sl)]
        yield
        nk = [_dot_nt(la_i, _bd4(kbar[s], lo, hi)) for la_i, s in zip(la, sl)]
        yield
        n = [jnp.where(before, t[:c], 0.0) for t in nb]
        a_rb = [jnp.where(upto, t[c:], 0.0) for t in nb]
        a_ak = [jnp.where(before, t[:c], 0.0) for t in nk]
        a_rk = [jnp.where(upto, t[c:], 0.0) for t in nk]
        x = [eye - t for t in n]
        p = [pm(t, t) for t in n]
        yield
        py = [pm(jnp.concatenate([u, r_], axis=0), v[s]) for u, r_, s in zip(a_ak, a_rk, sl)]
        yield
        for it in range(5):
            if it < 4:
                xp = [pm(jnp.concatenate([x_i, p_i], axis=0), p_i) for x_i, p_i in zip(x, p)]
                x = [x_i + t[:c] for x_i, t in zip(x, xp)]
                p = [t[c:] for t in xp]
            else:
                x = [x_i + pm(x_i, p_i) for x_i, p_i in zip(x, p)]
            yield
        ah = [pm(x_i, abar[s]) for x_i, s in zip(x, sl)]
        u0 = [pm(x_i, t[:c]) for x_i, t in zip(x, py)]
        yield
        ra = [pm(m_i, t) for m_i, t in zip(a_rb, ah)]
        ru = [pm(m_i, t) for m_i, t in zip(a_rb, u0)]
        gfull = [_dot_tn(btil[s].astype(BF16), t.astype(BF16)) for s, t in zip(sl, ah)]
        hfull = [_dot_tn(jnp.concatenate([ktil[s], -btil[s]], axis=0).astype(BF16),
                         jnp.concatenate([v[s], t], axis=0).astype(BF16)) for s, t in zip(sl, u0)]
        for i, (rs, ls) in enumerate(sl):
            ro = slice(g * grp + rs.start, g * grp + rs.stop)
            rh_ref[0, ro, ls] = (rbar[rs, ls] - ra[i]).astype(BF16)
            y0_ref[0, ro, ls] = py[i][c:] - ru[i]
            g_ref[0, ro, ls] = (eye * gam[rs, ls] - _diag_blocks(gfull[i], first)).astype(g_ref.dtype)
            h_ref[0, ro, ls] = _diag_blocks(hfull[i], first)

    ngroups = rkv_ref.shape[0] // grp
    vals = [dict() for _ in range(ngroups)]
    for _ in prologue(0, vals[0]):
        pass
    for g in range(ngroups):
        nxt = prologue(g + 1, vals[g + 1]) if g + 1 < ngroups else iter(())
        for stage, _ in enumerate(chains(g, vals[g])):
            if stage >= 1 and stage % 2 == 1:
                next(nxt, None)
        for _ in nxt:
            pass


def _rwkv_prep(p2, w0, a0, w_up, a_up, k_k, k_a, r_k, ones4):
    n, _ = p2.shape
    w = W_RWKV
    rows = PREP_ROWS if n % PREP_ROWS == 0 else PREP_GROUP
    const = lambda shape: pl.BlockSpec(shape, lambda d, i: (0,) * len(shape))
    perdir = lambda shape: pl.BlockSpec((1,) + shape, lambda d, i: (d,) + (0,) * len(shape))
    out_spec = pl.BlockSpec((1, rows, w), lambda d, i: (d, i, 0))
    f32_out = jax.ShapeDtypeStruct((2, n, w), F32)
    act_out = jax.ShapeDtypeStruct((2, n, w), ACT)
    return pl.pallas_call(
        _rwkv_prep_kernel,
        out_shape=(act_out, f32_out, act_out, f32_out, act_out),
        grid=(2, n // rows),
        in_specs=[
            pl.BlockSpec((rows, 3 * w), lambda d, i: (i, PB_RKV)),
            pl.BlockSpec((rows, 2 * W_LORA), lambda d, i: (i, PB_LORA128 + d)),
            perdir((1, w)), perdir((1, w)), perdir((W_LORA, w)), perdir((A_LORA, w)),
            const((1, w)), const((1, w)), const((1, w)),
            const((QUAD, QUAD)),
        ],
        out_specs=(out_spec,) * 5,
        compiler_params=_cparams(("parallel", "parallel")),
        name="rwkv_prep",
    )(p2, p2, w0, a0, w_up, a_up, k_k, k_a, r_k, ones4)


def _rwkv_scan_kernel(rhf, y0f, gf, hf, rhb, y0b, gb, hb, yf_ref, yb_ref, st_ref, *, nb):
    @pl.when(pl.program_id(0) == 0)
    def _():
        st_ref[...] = jnp.zeros(st_ref.shape, F32)

    first, lo, hi = _half_lane_masks()
    c = CHUNK
    for d, (rh, y0, g, hh, y_ref) in enumerate(((rhf, y0f, gf, hf, yf_ref), (rhb, y0b, gb, hb, yb_ref))):
        for bi in range(nb):
            for qd in range(W_RWKV // QUAD):
                ls = slice(qd * QUAD, (qd + 1) * QUAD)
                lhs = jnp.concatenate([rh[0, bi, :, ls], _bd4(g[0, bi, :, ls], lo, hi)], axis=0)
                res = _dot(lhs, st_ref[d, bi, qd].astype(BF16))
                y_ref[bi, :, ls] = (res[:c] + y0[0, bi, :, ls]).astype(y_ref.dtype)
                st_ref[d, bi, qd] = res[c:] + _bd4_f32(hh[0, bi, :, ls], first)


def _rwkv_scan(rh, y0, g, h, *, n_x, n_ctx):
    _, b, t, w = rh.shape
    c = CHUNK
    n = n_x + n_ctx
    fidx = lambda j: jnp.where(j < n_ctx, n_x + j, j - n_ctx)
    bidx = lambda j: n - 1 - j
    fspec = pl.BlockSpec((1, b, c, w), lambda j: (0, 0, fidx(j), 0))
    bspec = pl.BlockSpec((1, b, c, w), lambda j: (1, 0, bidx(j), 0))
    y_shape = jax.ShapeDtypeStruct((b, t, w), ACT)
    return pl.pallas_call(
        functools.partial(_rwkv_scan_kernel, nb=b),
        out_shape=(y_shape, y_shape),
        grid=(n,),
        in_specs=[fspec] * 4 + [bspec] * 4,
        out_specs=(
            pl.BlockSpec((b, c, w), lambda j: (0, fidx(j), 0)),
            pl.BlockSpec((b, c, w), lambda j: (0, bidx(j), 0)),
        ),
        scratch_shapes=[pltpu.VMEM((2, b, w // QUAD, QUAD, QUAD), F32)],
        compiler_params=_cparams(("arbitrary",)),
        name="rwkv_scan",
    )(rh, y0, g, h, rh, y0, g, h)


def _merge_kernel(z_ref, gate_ref, gl_ref, grw_ref, gml_ref, cin_ref, cb_ref, cc_ref, gcv_ref,
                  cin_p, cc_p, cin_n, cc_n, ymla_ref, yf_ref, yb_ref, bon_ref,
                  wbm_ref, wbc_ref, wbr_ref, wout_ref, gpost_ref, cw_ref, cbias_ref, gng_ref, gnb_ref,
                  avg_ref, o_ref, *, tiles_per_batch, ctx_tiles):
    tm = z_ref.shape[0]
    d = D_MODEL
    pos = pl.program_id(0) % tiles_per_batch
    x_tiles = tiles_per_batch - ctx_tiles
    first = jnp.logical_or(pos == 0, pos == x_tiles)
    last = jnp.logical_or(pos == x_tiles - 1, pos == tiles_per_batch - 1)

    def f32(ref, idx=slice(None)):
        return ref[idx].astype(F32)

    u = f32(cc_ref) * f32(cin_ref)
    hl = HALO_ROWS - 1
    u_halo_p = jnp.where(first, 0.0, f32(cc_p, slice(hl, hl + 1)) * f32(cin_p, slice(hl, hl + 1)))
    u_halo_n = jnp.where(last, 0.0, f32(cc_n, slice(0, 1)) * f32(cin_n, slice(0, 1)))
    row = lax.broadcasted_iota(jnp.int32, u.shape, 0)
    u_prev = jnp.where(row == 0, u_halo_p, pltpu.roll(u, 1, axis=0))
    u_next = jnp.where(row == tm - 1, u_halo_n, pltpu.roll(u, tm - 1, axis=0))
    cw = cw_ref[...]
    y_conv = f32(cb_ref) * (u_prev * cw[0:1] + u * cw[1:2] + u_next * cw[2:3] + cbias_ref[...])

    avg = avg_ref[...]

    def head_mean(x):
        hi, lo = _split2(x)
        parts = []
        for qd in range(W_RWKV // QUAD):
            ls = slice(qd * QUAD, (qd + 1) * QUAD)
            parts.append(_dot(hi[:, ls], avg) + _dot(lo[:, ls], avg))
        return jnp.concatenate(parts, axis=1)

    yr = f32(yf_ref) + f32(yb_ref)
    mu = head_mean(yr)
    dv = yr - mu
    var = head_mean(dv * dv)
    y_rwkv = dv * lax.rsqrt(var + GN_EPS) * gng_ref[...] + gnb_ref[...] + f32(bon_ref, 0) + f32(bon_ref, 1)

    br_mla = _dot((f32(ymla_ref) * _silu(f32(gml_ref))).astype(BF16), wbm_ref[...])
    br_conv = _dot((y_conv * _silu(f32(gcv_ref))).astype(BF16), wbc_ref[...])
    br_rwkv = _dot((y_rwkv * _silu(f32(grw_ref))).astype(BF16), wbr_ref[...])
    s = _sigmoid(f32(gl_ref))
    merged = s[:, :d] * br_mla + s[:, d:2 * d] * br_conv + s[:, 2 * d:] * br_rwkv
    o = _dot(merged.astype(BF16), wout_ref[...])
    o_ref[...] = z_ref[...] + gate_ref[0] * _rms(o, gpost_ref[...])


def _merge(z2, mods, p2, y_mla, yf, yb, bonus, wbm, wbc, wbr, wout, g_post, conv_w, conv_b,
           gn_g, gn_b, avg_bd, *, tiles_per_batch, ctx_tiles):
    n, d = z2.shape
    tm = ROW_BLK
    hb = tm // HALO_ROWS
    nhb = n // HALO_ROWS
    pcol = lambda blk: pl.BlockSpec((tm, 512), lambda i: (i, blk))
    prev = lambda blk: pl.BlockSpec((HALO_ROWS, 512), lambda i: (jnp.maximum(i * hb - 1, 0), blk))
    nxt = lambda blk: pl.BlockSpec((HALO_ROWS, 512), lambda i: (jnp.minimum((i + 1) * hb, nhb - 1), blk))
    const = lambda shape: pl.BlockSpec(shape, lambda i: (0,) * len(shape))
    row512 = pl.BlockSpec((tm, 512), lambda i: (i, 0))
    return pl.pallas_call(
        functools.partial(_merge_kernel, tiles_per_batch=tiles_per_batch, ctx_tiles=ctx_tiles),
        out_shape=jax.ShapeDtypeStruct((n, d), F32),
        grid=(n // tm,),
        in_specs=[
            pl.BlockSpec((tm, d), lambda i: (i, 0)),
            pl.BlockSpec((1, 1, d), lambda i: (_seg_row(i, tiles_per_batch), 0, 2)),
            pl.BlockSpec((tm, 3 * d), lambda i: (i, PB_GATE)),
            pcol(PB_GRWKV), pcol(PB_GMLA), pcol(PB_CVIN), pcol(PB_CVB), pcol(PB_CVC), pcol(PB_GCONV),
            prev(PB_CVIN), prev(PB_CVC), nxt(PB_CVIN), nxt(PB_CVC),
            row512, row512, row512,
            pl.BlockSpec((2, tm, 512), lambda i: (0, i, 0)),
            const(wbm.shape), const(wbc.shape), const(wbr.shape), const(wout.shape),
            const((1, d)), const(conv_w.shape), const((1, 512)), const((1, 512)), const((1, 512)),
            const(avg_bd.shape),
        ],
        out_specs=pl.BlockSpec((tm, d), lambda i: (i, 0)),
        compiler_params=_cparams(("parallel",)),
        name="merge",
    )(z2, mods, p2, p2, p2, p2, p2, p2, p2, p2, p2, p2, p2, y_mla, yf, yb, bonus,
      wbm, wbc, wbr, wout, g_post, conv_w, conv_b, gn_g, gn_b, avg_bd)


def _pair_swap(w):
    s = w.shape
    return w.reshape(s[:-1] + (s[-1] // 2, 2))[..., ::-1].reshape(s)


def _layout_w_in(w_in):
    sizes = (Q_LORA, KV_LORA, QK_ROPE, W_MLA, CONV_W, CONV_W, CONV_W, CONV_W, W_RWKV, W_RWKV, W_RWKV,
             W_LORA, W_LORA, A_LORA, A_LORA, W_RWKV, 3 * D_MODEL)
    offs = np.concatenate([[0], np.cumsum(sizes)])
    names = ("q_lat", "kv_lat", "kr", "g_mla", "cv_in", "cv_b", "cv_c", "g_conv", "r", "k", "v",
             "wd_f", "wd_b", "ad_f", "ad_b", "g_rwkv", "gl")
    col = {nm: w_in[..., offs[i]:offs[i + 1]] for i, nm in enumerate(names)}
    zeros = jnp.zeros(w_in.shape[:-1] + (512 - Q_LORA - 2 * QK_ROPE,), w_in.dtype)
    parts = [col["gl"], col["r"], col["k"], col["v"], col["g_rwkv"], col["g_mla"], col["cv_in"],
             col["cv_b"], col["cv_c"], col["g_conv"],
             col["q_lat"], col["kr"], _pair_swap(col["kr"]), zeros,
             col["kv_lat"], col["wd_f"], col["ad_f"], col["wd_b"], col["ad_b"]]
    out = jnp.concatenate(parts, axis=-1).astype(BF16)
    assert out.shape[-1] == PCOLS
    return out


def _layout_mla_weights(w_uq, w_ukv):
    depth = w_uq.shape[0]
    nh = N_HEADS_MLA
    wq = w_uq.reshape(depth, Q_LORA, nh, QK_HEAD)
    q_nope, q_rope = wq[..., :QK_NOPE], wq[..., QK_NOPE:]
    zq = jnp.zeros((depth, Q_LORA, nh, HEAD_PAD - QK_HEAD), w_uq.dtype)
    wq_a = jnp.concatenate([q_nope, q_rope, zq], axis=-1)
    wq_b = jnp.concatenate([jnp.zeros_like(q_nope), _pair_swap(q_rope), zq], axis=-1)
    wq_all = jnp.concatenate([wq_a.reshape(depth, Q_LORA, nh * HEAD_PAD),
                              wq_b.reshape(depth, Q_LORA, nh * HEAD_PAD)], axis=-1).astype(BF16)

    wkv = w_ukv.reshape(depth, KV_LORA, nh, QK_NOPE + V_HEAD)
    k_nope, v_w = wkv[..., :QK_NOPE], wkv[..., QK_NOPE:]
    zk = jnp.zeros((depth, KV_LORA, nh, HEAD_PAD - QK_NOPE), w_ukv.dtype)
    wk_top = jnp.concatenate([k_nope, zk], axis=-1).reshape(depth, KV_LORA, nh * HEAD_PAD)
    wv = jnp.concatenate([v_w, zk], axis=-1).reshape(depth, KV_LORA, nh * HEAD_PAD).astype(BF16)
    place = np.zeros((HEAD_PAD, HEAD_PAD), np.float32)
    place[np.arange(QK_ROPE), QK_NOPE + np.arange(QK_ROPE)] = 1.0
    e_a = np.tile(place, (1, nh))
    place_b = np.zeros((HEAD_PAD, HEAD_PAD), np.float32)
    place_b[QK_ROPE + np.arange(QK_ROPE), QK_NOPE + np.arange(QK_ROPE)] = 1.0
    e_b = np.tile(place_b, (1, nh))
    top = jnp.concatenate([wk_top, jnp.zeros_like(wk_top)], axis=-1)
    bot = jnp.broadcast_to(jnp.asarray(np.concatenate([e_a, e_b], axis=1)), (depth, HEAD_PAD, 2 * nh * HEAD_PAD))
    wk_all = jnp.concatenate([top, bot.astype(top.dtype)], axis=1).astype(BF16)
    return wq_all, wk_all, wv


def _rope_tables(seq, ctx_len):
    n_freq = QK_ROPE // 4
    pos = np.arange(seq)
    inv = ROPE_THETA ** (-np.arange(n_freq, dtype=np.float32) / n_freq)
    row = (pos // GRID_W).astype(np.float32)
    colp = (pos % GRID_W).astype(np.float32)
    ang = jnp.concatenate([jnp.asarray(row)[:, None] * jnp.asarray(inv), jnp.asarray(colp)[:, None] * jnp.asarray(inv)], axis=-1)
    cos, sin = jnp.cos(ang), jnp.sin(ang)
    cos2 = jnp.repeat(cos, 2, axis=-1)
    sin2 = jnp.stack([-sin, sin], axis=-1).reshape(seq, QK_ROPE)
    ones = jnp.ones((seq, QK_NOPE), F32)
    pad = jnp.zeros((seq, HEAD_PAD - QK_HEAD), F32)
    cos_x = jnp.concatenate([ones, cos2, pad], axis=-1)
    sin_x = jnp.concatenate([jnp.zeros_like(ones), sin2, pad], axis=-1)
    cos_c = jnp.concatenate([jnp.ones((ctx_len, QK_HEAD), F32), jnp.zeros((ctx_len, HEAD_PAD - QK_HEAD), F32)], axis=-1)
    sin_c = jnp.zeros((ctx_len, HEAD_PAD), F32)
    return jnp.concatenate([cos_x, cos_c], axis=0), jnp.concatenate([sin_x, sin_c], axis=0)


def _block_diag_const(n, blk, value):
    i = np.arange(n)
    return np.where((i[:, None] // blk) == (i[None, :] // blk), value, 0.0).astype(np.float32)


def _pick(n, candidates):
    for cand in candidates:
        if n % cand == 0:
            return cand
    raise ValueError(f"no tile for {n}")


def kernel(x, c, ctx, c_ctx, w_mod, b_mod, g_pre, g_post, w_in, g_q, g_kv, w_uq, w_ukv, conv_w, conv_b,
           w0, w_up, a0, a_up, k_k, k_a, r_k, gn_g, gn_b, w_br_mla, w_br_conv, w_br_rwkv, w_out):
    bsz, seq, d = x.shape
    ctx_len = ctx.shape[1]
    depth = w_mod.shape[0]
    assert d == D_MODEL and ctx_len == ROW_BLK and seq % ROW_BLK == 0
    t = seq + ctx_len
    n = bsz * t
    tiles_per_batch = t // ROW_BLK

    cc = jnp.zeros((8, d), F32).at[0].set(c_ctx).at[1:1 + bsz].set(c)
    mods = _adaln(cc, w_mod, b_mod).reshape(depth, 8, 1, 3 * d)

    w_in_p = _layout_w_in(w_in)
    wq_all, wk_all, wv_all = _layout_mla_weights(w_uq, w_ukv)
    cos_t, sin_t = _rope_tables(seq, ctx_len)
    vone = np.zeros((1, N_HEADS_MLA * HEAD_PAD), np.float32)
    vone[0, V_HEAD::HEAD_PAD] = 1.0
    vone = jnp.asarray(vone)
    ones4 = jnp.asarray(_block_diag_const(QUAD, RWKV_HEAD, 1.0), BF16)
    avg_bd = jnp.asarray(_block_diag_const(QUAD, RWKV_HEAD, 1.0 / RWKV_HEAD), BF16)

    tq = _pick(seq, (2048, 1024, 512, 256))
    tk = _pick(t, (768, 512, 256))

    z = jnp.concatenate([x, ctx], axis=1)
    for l in range(depth):
        z2 = z.reshape(n, d)
        p2 = _proj_in(z2, mods[l], g_pre[l][None], w_in_p[l], tiles_per_batch=tiles_per_batch)
        p3 = p2.reshape(bsz, t, PCOLS)

        qt, k, vt = _mla_prep(p3, cos_t, sin_t, g_q[l][None], g_kv[l][None], wq_all[l], wk_all[l], wv_all[l], vone)
        y_mla = _attention(qt, k, vt, None, tq=tq, tk=tk, nq=seq // tq, nk=t // tk, q_off=0, k_off=0)
        y_mla = _attention(qt, k, vt, y_mla, tq=ctx_len, tk=ctx_len, nq=1, nk=1,
                           q_off=seq // ctx_len, k_off=seq // ctx_len)

        rh, y0, g, h, bonus = _rwkv_prep(
            p2, w0[l][:, None], a0[l][:, None], w_up[l].astype(BF16), a_up[l].astype(BF16),
            k_k[l][None], k_a[l][None], r_k[l].reshape(1, W_RWKV), ones4)
        per_batch = lambda arr: arr.reshape(2, bsz, t, W_RWKV)
        yf, yb = _rwkv_scan(per_batch(rh), per_batch(y0), per_batch(g), per_batch(h),
                            n_x=seq // CHUNK, n_ctx=ctx_len // CHUNK)

        z2 = _merge(z2, mods[l], p2, y_mla.reshape(n, W_MLA), yf.reshape(n, W_RWKV), yb.reshape(n, W_RWKV),
                    bonus,
                    w_br_mla[l].astype(BF16), w_br_conv[l].astype(BF16), w_br_rwkv[l].astype(BF16),
                    w_out[l].astype(BF16), g_post[l][None], conv_w[l], conv_b[l][None],
                    gn_g[l][None], gn_b[l][None], avg_bd,
                    tiles_per_batch=tiles_per_batch, ctx_tiles=ctx_len // ROW_BLK)
        z = z2.reshape(bsz, t, d)
    return z[:, :seq]
```

```python
import functools
import math

import numpy as np
import jax
import jax.numpy as jnp
from jax import lax
from jax.experimental import pallas as pl
from jax.experimental.pallas import tpu as pltpu

F32 = jnp.float32
BF16 = jnp.bfloat16
ACT = BF16

D_MODEL = 1024
DEPTH = 4
GRID_W = 64
N_HEADS_MLA = 8
Q_LORA = 384
KV_LORA = 256
QK_NOPE = 64
QK_ROPE = 32
QK_HEAD = QK_NOPE + QK_ROPE
V_HEAD = 64
W_MLA = N_HEADS_MLA * V_HEAD
ROPE_THETA = 10000.0
ATTN_SCALE = QK_HEAD ** -0.5
CONV_W = 512
RWKV_HEADS = 8
RWKV_HEAD = 64
W_RWKV = RWKV_HEADS * RWKV_HEAD
W_LORA = 64
A_LORA = 64
RMS_EPS = 1e-6
GN_EPS = 64e-5
L2_EPS = 1e-12
LOG2E = math.log2(math.e)

LANES = 128
ROW_BLK = 256
CHUNK = 64
QUAD = 4 * RWKV_HEAD
PREP_GROUP = 256
PREP_ROWS = 512
HEAD_PAD = 128
QK_AHEAD = 1
HALO_ROWS = 16
VT_ROWS = 80
VMEM_LIMIT = 48 * 1024 * 1024

PCOLS = 17 * 512
PB_GATE = 0
PB_RKV = 2
PB_GRWKV = 9
PB_GMLA = 10
PB_CVIN = 11
PB_CVB = 12
PB_CVC = 13
PB_GCONV = 14
PB_Q = 15
PB_KV = 16
PB_LORA128 = (16 * 512 + 256) // 128


def _cparams(sem, vmem=VMEM_LIMIT):
    return pltpu.CompilerParams(dimension_semantics=sem, vmem_limit_bytes=vmem)


def _dot(a, b):
    return jnp.dot(a, b, preferred_element_type=F32)


def _dot_nt(a, b):
    return lax.dot_general(a, b, (((1,), (1,)), ((), ())), preferred_element_type=F32)


def _dot_tn(a, b):
    return lax.dot_general(a, b, (((0,), (0,)), ((), ())), preferred_element_type=F32)


def _split2(x):
    hi = x.astype(BF16)
    lo = (x - hi.astype(F32)).astype(BF16)
    return hi, lo


def _split3(x):
    hi = x.astype(BF16)
    r1 = x - hi.astype(F32)
    mid = r1.astype(BF16)
    lo = (r1 - mid.astype(F32)).astype(BF16)
    return hi, mid, lo


def _dot_exact_rhs(x, m_bf16):
    hi, mid, lo = _split3(x)
    return _dot(hi, m_bf16) + _dot(mid, m_bf16) + _dot(lo, m_bf16)


def _dot_hi(a, b):
    ah, al = _split2(a)
    bh, bl = _split2(b)
    return _dot(ah, bh) + _dot(ah, bl) + _dot(al, bh)


def _sigmoid(x):
    return 1.0 / (1.0 + jnp.exp(-x))


def _silu(x):
    return x * _sigmoid(x)


def _rms(x, g):
    return x * lax.rsqrt(jnp.mean(x * x, axis=-1, keepdims=True) + RMS_EPS) * g


def _adaln_kernel(c_ref, w_ref, b_ref, o_ref):
    a = _silu(c_ref[...])
    o_ref[0] = _dot_hi(a, w_ref[0]) + b_ref[0]


def _adaln(cc, w_mod, b_mod):
    depth, d, d3 = w_mod.shape
    tn = 1024
    return pl.pallas_call(
        _adaln_kernel,
        out_shape=jax.ShapeDtypeStruct((depth, 8, d3), F32),
        grid=(depth, d3 // tn),
        in_specs=[
            pl.BlockSpec((8, d), lambda l, j: (0, 0)),
            pl.BlockSpec((1, d, tn), lambda l, j: (l, 0, j)),
            pl.BlockSpec((1, 1, tn), lambda l, j: (l, 0, j)),
        ],
        out_specs=pl.BlockSpec((1, 8, tn), lambda l, j: (l, 0, j)),
        compiler_params=_cparams(("parallel", "parallel")),
        name="adaln",
    )(cc, w_mod, b_mod.reshape(depth, 1, d3))


def _seg_row(blk, tiles_per_batch):
    return jnp.where(blk % tiles_per_batch == tiles_per_batch - 1, 0, 1 + blk // tiles_per_batch)


def _proj_kernel(z_ref, *refs, nsub):
    mod_refs = refs[:2 * nsub]
    g_ref, w_ref, o_ref, h_ref = refs[2 * nsub:]

    @pl.when(pl.program_id(1) == 0)
    def _():
        y = _rms(z_ref[...], g_ref[...])
        for s in range(nsub):
            rows = slice(s * ROW_BLK, (s + 1) * ROW_BLK)
            shift, scale = mod_refs[2 * s][0], mod_refs[2 * s + 1][0]
            h_ref[rows, :] = (y[rows, :] * (1.0 + scale) + shift).astype(BF16)

    o_ref[...] = _dot(h_ref[...], w_ref[...]).astype(o_ref.dtype)


def _proj_in(z2, mods, g_pre, w, *, tiles_per_batch):
    n, d = z2.shape
    tm = 1024 if n % 1024 == 0 else ROW_BLK
    tn = PCOLS // 4
    nsub = tm // ROW_BLK
    mod_specs = []
    for s in range(nsub):
        for col in (0, 1):
            mod_specs.append(pl.BlockSpec(
                (1, 1, d), lambda i, j, s=s, col=col: (_seg_row(i * nsub + s, tiles_per_batch), 0, col)))
    return pl.pallas_call(
        functools.partial(_proj_kernel, nsub=nsub),
        out_shape=jax.ShapeDtypeStruct((n, PCOLS), ACT),
        grid=(n // tm, PCOLS // tn),
        in_specs=[pl.BlockSpec((tm, d), lambda i, j: (i, 0))] + mod_specs + [
            pl.BlockSpec((1, d), lambda i, j: (0, 0)),
            pl.BlockSpec((d, tn), lambda i, j: (0, j)),
        ],
        out_specs=pl.BlockSpec((tm, tn), lambda i, j: (i, j)),
        scratch_shapes=[pltpu.VMEM((tm, d), BF16)],
        compiler_params=_cparams(("parallel", "arbitrary")),
        name="proj_in",
    )(z2, *([mods] * (2 * nsub)), g_pre, w)


def _mla_prep_kernel(qb_ref, kb_ref, cos_ref, sin_ref, gq_ref, gkv_ref, wq_ref, wk_ref, wv_ref,
                     vone_ref, qt_ref, k_ref, vt_ref):
    nh = N_HEADS_MLA
    hw = nh * HEAD_PAD
    qb = qb_ref[0].astype(F32)
    kb = kb_ref[0].astype(F32)
    cos8 = jnp.tile(cos_ref[...], (1, nh))
    sin8 = jnp.tile(sin_ref[...], (1, nh))
    qn = _rms(qb[:, :Q_LORA], gq_ref[...]).astype(BF16)
    qq = _dot(qn, wq_ref[...])
    q = (qq[:, :hw] * cos8 + qq[:, hw:] * sin8) * (ATTN_SCALE * LOG2E)
    kvn = _rms(kb[:, :KV_LORA], gkv_ref[...]).astype(BF16)
    kin = jnp.concatenate([kvn, qb[:, Q_LORA:].astype(BF16)], axis=1)
    kk = _dot(kin, wk_ref[...])
    k_ref[0] = (kk[:, :hw] * cos8 + kk[:, hw:] * sin8).astype(BF16)
    v = _dot(kvn, wv_ref[...]) + vone_ref[...]
    for h in range(nh):
        lanes = slice(h * HEAD_PAD, (h + 1) * HEAD_PAD)
        qt_ref[0, h] = q[:, lanes].T.astype(BF16)
        vt_ref[0, h] = v[:, lanes].T[:VT_ROWS].astype(BF16)


def _mla_prep(p3, cos_t, sin_t, g_q, g_kv, wq, wk, wv, vone):
    b, t, _ = p3.shape
    tm = ROW_BLK
    hw = N_HEADS_MLA * HEAD_PAD
    const = lambda shape: pl.BlockSpec(shape, lambda bi, i: (0,) * len(shape))
    return pl.pallas_call(
        _mla_prep_kernel,
        out_shape=(
            jax.ShapeDtypeStruct((b, N_HEADS_MLA, HEAD_PAD, t), BF16),
            jax.ShapeDtypeStruct((b, t, hw), BF16),
            jax.ShapeDtypeStruct((b, N_HEADS_MLA, VT_ROWS, t), BF16),
        ),
        grid=(b, t // tm),
        in_specs=[
            pl.BlockSpec((1, tm, 512), lambda bi, i: (bi, i, PB_Q)),
            pl.BlockSpec((1, tm, 512), lambda bi, i: (bi, i, PB_KV)),
            pl.BlockSpec((tm, HEAD_PAD), lambda bi, i: (i, 0)),
            pl.BlockSpec((tm, HEAD_PAD), lambda bi, i: (i, 0)),
            const((1, Q_LORA)),
            const((1, KV_LORA)),
            const(wq.shape),
            const(wk.shape),
            const(wv.shape),
            const((1, hw)),
        ],
        out_specs=(
            pl.BlockSpec((1, N_HEADS_MLA, HEAD_PAD, tm), lambda bi, i: (bi, 0, 0, i)),
            pl.BlockSpec((1, tm, hw), lambda bi, i: (bi, i, 0)),
            pl.BlockSpec((1, N_HEADS_MLA, VT_ROWS, tm), lambda bi, i: (bi, 0, 0, i)),
        ),
        compiler_params=_cparams(("parallel", "parallel")),
        name="mla_prep",
    )(p3, p3, cos_t, sin_t, g_q, g_kv, wq, wk, wv, vone)


def _attn_kernel(qt_ref, k_ref, vt_ref, *rest, nk):
    o_ref, m_ref, acc_ref = rest[-3:]
    j = pl.program_id(2)

    @pl.when(j == 0)
    def _():
        m_ref[...] = jnp.full(m_ref.shape, -jnp.inf, F32)
        acc_ref[...] = jnp.zeros(acc_ref.shape, F32)

    def scores(h):
        return _dot(k_ref[0, :, h * HEAD_PAD:(h + 1) * HEAD_PAD], qt_ref[0, h])

    s_queue = [scores(h) for h in range(QK_AHEAD)]
    for h in range(N_HEADS_MLA):
        s = s_queue.pop(0)
        if h + QK_AHEAD < N_HEADS_MLA:
            s_queue.append(scores(h + QK_AHEAD))
        m_prev = m_ref[h]
        m_new = jnp.maximum(m_prev, jnp.max(s, axis=0, keepdims=True))
        alpha = jnp.exp2(m_prev - m_new)
        p = jnp.exp2(s - m_new[0:1, :]).astype(BF16)
        acc_ref[h] = alpha[0:1, :] * acc_ref[h] + _dot(vt_ref[0, h], p)
        m_ref[h] = m_new

    @pl.when(j == nk - 1)
    def _():
        outs = []
        for h in range(N_HEADS_MLA):
            a = acc_ref[h]
            outs.append(a[:V_HEAD] * (1.0 / a[V_HEAD:V_HEAD + 1]))
        o_ref[0] = jnp.concatenate(outs, axis=0).T.astype(o_ref.dtype)


def _attention(qt, k, vt, y_prev, *, tq, tk, nq, nk, q_off, k_off):
    b, t, hw = k.shape
    in_specs = [
        pl.BlockSpec((1, N_HEADS_MLA, HEAD_PAD, tq), lambda bi, i, j: (bi, 0, 0, i + q_off)),
        pl.BlockSpec((1, tk, hw), lambda bi, i, j: (bi, j + k_off, 0)),
        pl.BlockSpec((1, N_HEADS_MLA, VT_ROWS, tk), lambda bi, i, j: (bi, 0, 0, j + k_off)),
    ]
    args = [qt, k, vt]
    aliases = {}
    if y_prev is not None:
        in_specs.append(pl.BlockSpec(memory_space=pl.ANY))
        args.append(y_prev)
        aliases = {3: 0}
    return pl.pallas_call(
        functools.partial(_attn_kernel, nk=nk),
        out_shape=jax.ShapeDtypeStruct((b, t, W_MLA), ACT),
        grid=(b, nq, nk),
        in_specs=in_specs,
        out_specs=pl.BlockSpec((1, tq, W_MLA), lambda bi, i, j: (bi, i + q_off, 0)),
        scratch_shapes=[
            pltpu.VMEM((N_HEADS_MLA, 8, tq), F32),
            pltpu.VMEM((N_HEADS_MLA, VT_ROWS, tq), F32),
        ],
        input_output_aliases=aliases,
        compiler_params=_cparams(("parallel", "parallel", "arbitrary")),
        name="attn_ctx" if y_prev is not None else "attn_x",
    )(*args)


def _half_lane_masks():
    lane = lax.broadcasted_iota(jnp.int32, (CHUNK, LANES), 1)
    first = lane < RWKV_HEAD
    return first, jnp.where(first, 1.0, 0.0).astype(BF16), jnp.where(first, 0.0, 1.0).astype(BF16)


def _bd4(x, lo, hi):
    xb = x.astype(BF16)
    xl, xr = xb[:, :LANES], xb[:, LANES:]
    z = jnp.zeros((2 * CHUNK, LANES), BF16)
    c0 = jnp.concatenate([xl * lo, xl * hi, z], axis=0)
    c1 = jnp.concatenate([z, xr * lo, xr * hi], axis=0)
    return jnp.concatenate([c0, c1], axis=1)


def _diag_blocks(full, first):
    c = CHUNK
    left = jnp.where(first, full[0:c, :LANES], full[c:2 * c, :LANES])
    right = jnp.where(first, full[2 * c:3 * c, LANES:], full[3 * c:4 * c, LANES:])
    return jnp.concatenate([left, right], axis=1)


def _rwkv_prep_kernel(rkv_ref, lora_ref, w0_ref, a0_ref, wup_ref, aup_ref, kk_ref, ka_ref, rk_ref,
                      ones_ref, rh_ref, y0_ref, g_ref, h_ref, bonus_ref):
    fwd = pl.program_id(0) == 0
    c = CHUNK
    w = W_RWKV
    grp = PREP_GROUP
    ones4 = ones_ref[...]
    sgn = jnp.where(fwd, 1, -1)
    ti = lax.broadcasted_iota(jnp.int32, (grp, grp), 0)
    si = lax.broadcasted_iota(jnp.int32, (grp, grp), 1)
    same = jnp.where((ti // c) == (si // c), 1.0, 0.0)
    tri = jnp.where((si - ti) * sgn <= 0, same, 0.0).astype(BF16)
    first, lo, hi = _half_lane_masks()
    tq = lax.broadcasted_iota(jnp.int32, (c, QUAD), 0)
    sq = lax.broadcasted_iota(jnp.int32, (c, QUAD), 1) % c
    before = (sq - tq) * sgn < 0
    upto = (sq - tq) * sgn <= 0
    eye = jnp.where(sq == tq, 1.0, 0.0)

    def head_sum(x):
        xh, xl = _split2(x)
        parts = []
        for qd in range(w // QUAD):
            ls = slice(qd * QUAD, (qd + 1) * QUAD)
            parts.append(_dot(xh[:, ls], ones4) + _dot(xl[:, ls], ones4))
        return jnp.concatenate(parts, axis=1)

    def pm(x, y):
        return _dot(x.astype(BF16), _bd4(y, lo, hi))

    def prologue(g, out):
        gs = slice(g * grp, (g + 1) * grp)
        rkv = rkv_ref[gs, :].astype(F32)
        r, k, v = rkv[:, :w], rkv[:, w:2 * w], rkv[:, 2 * w:]
        lora = lora_ref[gs, :].astype(F32)
        zw = w0_ref[0] + _dot(jnp.tanh(lora[:, :W_LORA]).astype(BF16), wup_ref[0])
        za = a0_ref[0] + _dot(lora[:, W_LORA:].astype(BF16), aup_ref[0])
        yield
        ell = -math.exp(-0.5) * _sigmoid(zw)
        a = _sigmoid(za)
        kkr = k * kk_ref[...]
        k_d = k * (1.0 + (a - 1.0) * ka_ref[...])
        kk_ss = head_sum(kkr * kkr)
        rk_s = head_sum(r * k_d * rk_ref[...])
        yield
        kk = kkr * lax.rsqrt(kk_ss + L2_EPS)
        bonus_ref[0, gs, :] = (rk_s * v).astype(bonus_ref.dtype)
        ell_hi, ell_lo = _split2(ell)
        lc = _dot(tri, ell_hi) + _dot(tri, ell_lo)
        yield
        ltot = jnp.concatenate(
            [jnp.broadcast_to(jnp.where(fwd, lc[ch * c + c - 1:ch * c + c], lc[ch * c:ch * c + 1]), (c, w))
             for ch in range(grp // c)], axis=0)
        e_neg = jnp.exp(-lc)
        e_tail = jnp.exp(ltot - lc)
        kka = kk * a
        out.update(abar=kk * jnp.exp(lc - ell), bbar=kka * e_neg, kbar=k_d * e_neg, rbar=r * jnp.exp(lc),
                   btil=kka * e_tail, ktil=k_d * e_tail, v=v, gam=jnp.exp(ltot))

    def chains(g, q):
        sl = [(slice(ch * c, (ch + 1) * c), slice(qd * QUAD, (qd + 1) * QUAD))
              for ch in range(grp // c) for qd in range(w // QUAD)]
        abar, bbar, kbar, rbar, btil, ktil, v, gam = (
            q[nm] for nm in ("abar", "bbar", "kbar", "rbar", "btil", "ktil", "v", "gam"))
        la = [jnp.concatenate([abar[s], rbar[s]], axis=0).astype(BF16) for s in sl]
        nb = [_dot_nt(la_i, _bd4(bbar[s], lo, hi)) for la_i, s in zip(la, sl)]
        yield
        nk = [_dot_nt(la_i, _bd4(kbar[s], lo, hi)) for la_i, s in zip(la, sl)]
        yield
        n = [jnp.where(before, t[:c], 0.0) for t in nb]
        a_rb = [jnp.where(upto, t[c:], 0.0) for t in nb]
        a_ak = [jnp.where(before, t[:c], 0.0) for t in nk]
        a_rk = [jnp.where(upto, t[c:], 0.0) for t in nk]
        x = [eye - t for t in n]
        p = [pm(t, t) for t in n]
        yield
        py = [pm(jnp.concatenate([u, r_], axis=0), v[s]) for u, r_, s in zip(a_ak, a_rk, sl)]
        yield
        for it in range(5):
            if it < 4:
                xp = [pm(jnp.concatenate([x_i, p_i], axis=0), p_i) for x_i, p_i in zip(x, p)]
                x = [x_i + t[:c] for x_i, t in zip(x, xp)]
                p = [t[c:] for t in xp]
            else:
                x = [x_i + pm(x_i, p_i) for x_i, p_i in zip(x, p)]
            yield
        ah = [pm(x_i, abar[s]) for x_i, s in zip(x, sl)]
        u0 = [pm(x_i, t[:c]) for x_i, t in zip(x, py)]
        yield
        ra = [pm(m_i, t) for m_i, t in zip(a_rb, ah)]
        ru = [pm(m_i, t) for m_i, t in zip(a_rb, u0)]
        gfull = [_dot_tn(btil[s].astype(BF16), t.astype(BF16)) for s, t in zip(sl, ah)]
        hfull = [_dot_tn(jnp.concatenate([ktil[s], -btil[s]], axis=0).astype(BF16),
                         jnp.concatenate([v[s], t], axis=0).astype(BF16)) for s, t in zip(sl, u0)]
        for i, (rs, ls) in enumerate(sl):
            ro = slice(g * grp + rs.start, g * grp + rs.stop)
            rh_ref[0, ro, ls] = (rbar[rs, ls] - ra[i]).astype(BF16)
            y0_ref[0, ro, ls] = py[i][c:] - ru[i]
            g_ref[0, ro, ls] = (eye * gam[rs, ls] - _diag_blocks(gfull[i], first)).astype(g_ref.dtype)
            h_ref[0, ro, ls] = _diag_blocks(hfull[i], first)

    ngroups = rkv_ref.shape[0] // grp
    vals = [dict() for _ in range(ngroups)]
    for _ in prologue(0, vals[0]):
        pass
    for g in range(ngroups):
        nxt = prologue(g + 1, vals[g + 1]) if g + 1 < ngroups else iter(())
        for stage, _ in enumerate(chains(g, vals[g])):
            if stage >= 1 and stage % 2 == 1:
                next(nxt, None)
        for _ in nxt:
            pass


def _rwkv_prep(p2, w0, a0, w_up, a_up, k_k, k_a, r_k, ones4):
    n, _ = p2.shape
    w = W_RWKV
    rows = PREP_ROWS if n % PREP_ROWS == 0 else PREP_GROUP
    const = lambda shape: pl.BlockSpec(shape, lambda d, i: (0,) * len(shape))
    perdir = lambda shape: pl.BlockSpec((1,) + shape, lambda d, i: (d,) + (0,) * len(shape))
    out_spec = pl.BlockSpec((1, rows, w), lambda d, i: (d, i, 0))
    f32_out = jax.ShapeDtypeStruct((2, n, w), F32)
    act_out = jax.ShapeDtypeStruct((2, n, w), ACT)
    return pl.pallas_call(
        _rwkv_prep_kernel,
        out_shape=(act_out, f32_out, act_out, f32_out, act_out),
        grid=(2, n // rows),
        in_specs=[
            pl.BlockSpec((rows, 3 * w), lambda d, i: (i, PB_RKV)),
            pl.BlockSpec((rows, 2 * W_LORA), lambda d, i: (i, PB_LORA128 + d)),
            perdir((1, w)), perdir((1, w)), perdir((W_LORA, w)), perdir((A_LORA, w)),
            const((1, w)), const((1, w)), const((1, w)),
            const((QUAD, QUAD)),
        ],
        out_specs=(out_spec,) * 5,
        compiler_params=_cparams(("parallel", "parallel")),
        name="rwkv_prep",
    )(p2, p2, w0, a0, w_up, a_up, k_k, k_a, r_k, ones4)


def _rwkv_scan_kernel(rhf, y0f, gf, hf, rhb, y0b, gb, hb, yf_ref, yb_ref, st_ref, *, nb):
    @pl.when(pl.program_id(0) == 0)
    def _():
        st_ref[...] = jnp.zeros(st_ref.shape, F32)

    _, lo, hi = _half_lane_masks()
    c = CHUNK
    for d, (rh, y0, g, hh, y_ref) in enumerate(((rhf, y0f, gf, hf, yf_ref), (rhb, y0b, gb, hb, yb_ref))):
        for bi in range(nb):
            for qd in range(W_RWKV // QUAD):
                ls = slice(qd * QUAD, (qd + 1) * QUAD)
                lhs = jnp.concatenate([rh[0, bi, :, ls], g[0, bi, :, ls]], axis=0)
                res = _dot(lhs, _bd4(st_ref[d, bi, qd], lo, hi))
                y_ref[bi, :, ls] = (res[:c] + y0[0, bi, :, ls]).astype(y_ref.dtype)
                st_ref[d, bi, qd] = res[c:] + hh[0, bi, :, ls]


def _rwkv_scan(rh, y0, g, h, *, n_x, n_ctx):
    _, b, t, w = rh.shape
    c = CHUNK
    n = n_x + n_ctx
    fidx = lambda j: jnp.where(j < n_ctx, n_x + j, j - n_ctx)
    bidx = lambda j: n - 1 - j
    fspec = pl.BlockSpec((1, b, c, w), lambda j: (0, 0, fidx(j), 0))
    bspec = pl.BlockSpec((1, b, c, w), lambda j: (1, 0, bidx(j), 0))
    y_shape = jax.ShapeDtypeStruct((b, t, w), ACT)
    return pl.pallas_call(
        functools.partial(_rwkv_scan_kernel, nb=b),
        out_shape=(y_shape, y_shape),
        grid=(n,),
        in_specs=[fspec] * 4 + [bspec] * 4,
        out_specs=(
            pl.BlockSpec((b, c, w), lambda j: (0, fidx(j), 0)),
            pl.BlockSpec((b, c, w), lambda j: (0, bidx(j), 0)),
        ),
        scratch_shapes=[pltpu.VMEM((2, b, w // QUAD, RWKV_HEAD, QUAD), F32)],
        compiler_params=_cparams(("arbitrary",)),
        name="rwkv_scan",
    )(rh, y0, g, h, rh, y0, g, h)


def _merge_kernel(z_ref, gate_ref, gl_ref, grw_ref, gml_ref, cin_ref, cb_ref, cc_ref, gcv_ref,
                  cin_p, cc_p, cin_n, cc_n, ymla_ref, yf_ref, yb_ref, bon_ref,
                  wbm_ref, wbc_ref, wbr_ref, wout_ref, gpost_ref, cw_ref, cbias_ref, gng_ref, gnb_ref,
                  avg_ref, o_ref, *, tiles_per_batch, ctx_tiles, x_only):
    tm = z_ref.shape[0]
    d = D_MODEL
    x_tiles = tiles_per_batch - ctx_tiles
    pos = pl.program_id(0) % (x_tiles if x_only else tiles_per_batch)
    first = jnp.logical_or(pos == 0, pos == x_tiles)
    last = jnp.logical_or(pos == x_tiles - 1, pos == tiles_per_batch - 1)

    def f32(ref, idx=slice(None)):
        return ref[idx].astype(F32)

    u = f32(cc_ref) * f32(cin_ref)
    hl = HALO_ROWS - 1
    u_halo_p = jnp.where(first, 0.0, f32(cc_p, slice(hl, hl + 1)) * f32(cin_p, slice(hl, hl + 1)))
    u_halo_n = jnp.where(last, 0.0, f32(cc_n, slice(0, 1)) * f32(cin_n, slice(0, 1)))
    row = lax.broadcasted_iota(jnp.int32, u.shape, 0)
    u_prev = jnp.where(row == 0, u_halo_p, pltpu.roll(u, 1, axis=0))
    u_next = jnp.where(row == tm - 1, u_halo_n, pltpu.roll(u, tm - 1, axis=0))
    cw = cw_ref[...]
    y_conv = f32(cb_ref) * (u_prev * cw[0:1] + u * cw[1:2] + u_next * cw[2:3] + cbias_ref[...])

    avg = avg_ref[...]

    def head_mean(x):
        hi, lo = _split2(x)
        parts = []
        for qd in range(W_RWKV // QUAD):
            ls = slice(qd * QUAD, (qd + 1) * QUAD)
            parts.append(_dot(hi[:, ls], avg) + _dot(lo[:, ls], avg))
        return jnp.concatenate(parts, axis=1)

    yr = f32(yf_ref) + f32(yb_ref)
    mu = head_mean(yr)
    dv = yr - mu
    var = head_mean(dv * dv)
    y_rwkv = dv * lax.rsqrt(var + GN_EPS) * gng_ref[...] + gnb_ref[...] + f32(bon_ref, 0) + f32(bon_ref, 1)

    br_mla = _dot((f32(ymla_ref) * _silu(f32(gml_ref))).astype(BF16), wbm_ref[...])
    br_conv = _dot((y_conv * _silu(f32(gcv_ref))).astype(BF16), wbc_ref[...])
    br_rwkv = _dot((y_rwkv * _silu(f32(grw_ref))).astype(BF16), wbr_ref[...])
    s = _sigmoid(f32(gl_ref))
    merged = s[:, :d] * br_mla + s[:, d:2 * d] * br_conv + s[:, 2 * d:] * br_rwkv
    o = _dot(merged.astype(BF16), wout_ref[...])
    o_ref[...] = z_ref[...] + gate_ref[0] * _rms(o, gpost_ref[...])


def _merge(z2, mods, p2, y_mla, yf, yb, bonus, wbm, wbc, wbr, wout, g_post, conv_w, conv_b,
           gn_g, gn_b, avg_bd, *, tiles_per_batch, ctx_tiles, x_only):
    n, d = z2.shape
    tm = ROW_BLK
    hb = tm // HALO_ROWS
    nhb = n // HALO_ROWS
    x_tiles = tiles_per_batch - ctx_tiles
    n_tiles = (n // tm) // tiles_per_batch * x_tiles if x_only else n // tm
    gi = (lambda i: i // x_tiles * tiles_per_batch + i % x_tiles) if x_only else (lambda i: i)
    pcol = lambda blk: pl.BlockSpec((tm, 512), lambda i: (gi(i), blk))
    prev = lambda blk: pl.BlockSpec((HALO_ROWS, 512), lambda i: (jnp.maximum(gi(i) * hb - 1, 0), blk))
    nxt = lambda blk: pl.BlockSpec((HALO_ROWS, 512), lambda i: (jnp.minimum((gi(i) + 1) * hb, nhb - 1), blk))
    const = lambda shape: pl.BlockSpec(shape, lambda i: (0,) * len(shape))
    row512 = pl.BlockSpec((tm, 512), lambda i: (gi(i), 0))
    return pl.pallas_call(
        functools.partial(_merge_kernel, tiles_per_batch=tiles_per_batch, ctx_tiles=ctx_tiles, x_only=x_only),
        out_shape=jax.ShapeDtypeStruct((n_tiles * tm, d), F32),
        grid=(n_tiles,),
        in_specs=[
            pl.BlockSpec((tm, d), lambda i: (gi(i), 0)),
            pl.BlockSpec((1, 1, d), lambda i: (_seg_row(gi(i), tiles_per_batch), 0, 2)),
            pl.BlockSpec((tm, 3 * d), lambda i: (gi(i), PB_GATE)),
            pcol(PB_GRWKV), pcol(PB_GMLA), pcol(PB_CVIN), pcol(PB_CVB), pcol(PB_CVC), pcol(PB_GCONV),
            prev(PB_CVIN), prev(PB_CVC), nxt(PB_CVIN), nxt(PB_CVC),
            row512, row512, row512,
            pl.BlockSpec((2, tm, 512), lambda i: (0, gi(i), 0)),
            const(wbm.shape), const(wbc.shape), const(wbr.shape), const(wout.shape),
            const((1, d)), const(conv_w.shape), const((1, 512)), const((1, 512)), const((1, 512)),
            const(avg_bd.shape),
        ],
        out_specs=pl.BlockSpec((tm, d), lambda i: (i, 0)),
        compiler_params=_cparams(("parallel",)),
        name="merge",
    )(z2, mods, p2, p2, p2, p2, p2, p2, p2, p2, p2, p2, p2, y_mla, yf, yb, bonus,
      wbm, wbc, wbr, wout, g_post, conv_w, conv_b, gn_g, gn_b, avg_bd)


def _pair_swap(w):
    s = w.shape
    return w.reshape(s[:-1] + (s[-1] // 2, 2))[..., ::-1].reshape(s)


def _layout_w_in(w_in):
    sizes = (Q_LORA, KV_LORA, QK_ROPE, W_MLA, CONV_W, CONV_W, CONV_W, CONV_W, W_RWKV, W_RWKV, W_RWKV,
             W_LORA, W_LORA, A_LORA, A_LORA, W_RWKV, 3 * D_MODEL)
    offs = np.concatenate([[0], np.cumsum(sizes)])
    names = ("q_lat", "kv_lat", "kr", "g_mla", "cv_in", "cv_b", "cv_c", "g_conv", "r", "k", "v",
             "wd_f", "wd_b", "ad_f", "ad_b", "g_rwkv", "gl")
    col = {nm: w_in[..., offs[i]:offs[i + 1]] for i, nm in enumerate(names)}
    zeros = jnp.zeros(w_in.shape[:-1] + (512 - Q_LORA - 2 * QK_ROPE,), w_in.dtype)
    parts = [col["gl"], col["r"], col["k"], col["v"], col["g_rwkv"], col["g_mla"], col["cv_in"],
             col["cv_b"], col["cv_c"], col["g_conv"],
             col["q_lat"], col["kr"], _pair_swap(col["kr"]), zeros,
             col["kv_lat"], col["wd_f"], col["ad_f"], col["wd_b"], col["ad_b"]]
    out = jnp.concatenate(parts, axis=-1).astype(BF16)
    assert out.shape[-1] == PCOLS
    return out


def _layout_mla_weights(w_uq, w_ukv):
    depth = w_uq.shape[0]
    nh = N_HEADS_MLA
    wq = w_uq.reshape(depth, Q_LORA, nh, QK_HEAD)
    q_nope, q_rope = wq[..., :QK_NOPE], wq[..., QK_NOPE:]
    zq = jnp.zeros((depth, Q_LORA, nh, HEAD_PAD - QK_HEAD), w_uq.dtype)
    wq_a = jnp.concatenate([q_nope, q_rope, zq], axis=-1)
    wq_b = jnp.concatenate([jnp.zeros_like(q_nope), _pair_swap(q_rope), zq], axis=-1)
    wq_all = jnp.concatenate([wq_a.reshape(depth, Q_LORA, nh * HEAD_PAD),
                              wq_b.reshape(depth, Q_LORA, nh * HEAD_PAD)], axis=-1).astype(BF16)

    wkv = w_ukv.reshape(depth, KV_LORA, nh, QK_NOPE + V_HEAD)
    k_nope, v_w = wkv[..., :QK_NOPE], wkv[..., QK_NOPE:]
    zk = jnp.zeros((depth, KV_LORA, nh, HEAD_PAD - QK_NOPE), w_ukv.dtype)
    wk_top = jnp.concatenate([k_nope, zk], axis=-1).reshape(depth, KV_LORA, nh * HEAD_PAD)
    wv = jnp.concatenate([v_w, zk], axis=-1).reshape(depth, KV_LORA, nh * HEAD_PAD).astype(BF16)
    place = np.zeros((HEAD_PAD, HEAD_PAD), np.float32)
    place[np.arange(QK_ROPE), QK_NOPE + np.arange(QK_ROPE)] = 1.0
    e_a = np.tile(place, (1, nh))
    place_b = np.zeros((HEAD_PAD, HEAD_PAD), np.float32)
    place_b[QK_ROPE + np.arange(QK_ROPE), QK_NOPE + np.arange(QK_ROPE)] = 1.0
    e_b = np.tile(place_b, (1, nh))
    top = jnp.concatenate([wk_top, jnp.zeros_like(wk_top)], axis=-1)
    bot = jnp.broadcast_to(jnp.asarray(np.concatenate([e_a, e_b], axis=1)), (depth, HEAD_PAD, 2 * nh * HEAD_PAD))
    wk_all = jnp.concatenate([top, bot.astype(top.dtype)], axis=1).astype(BF16)
    return wq_all, wk_all, wv


def _rope_tables(seq, ctx_len):
    n_freq = QK_ROPE // 4
    pos = np.arange(seq)
    inv = ROPE_THETA ** (-np.arange(n_freq, dtype=np.float32) / n_freq)
    row = (pos // GRID_W).astype(np.float32)
    colp = (pos % GRID_W).astype(np.float32)
    ang = jnp.concatenate([jnp.asarray(row)[:, None] * jnp.asarray(inv), jnp.asarray(colp)[:, None] * jnp.asarray(inv)], axis=-1)
    cos, sin = jnp.cos(ang), jnp.sin(ang)
    cos2 = jnp.repeat(cos, 2, axis=-1)
    sin2 = jnp.stack([-sin, sin], axis=-1).reshape(seq, QK_ROPE)
    ones = jnp.ones((seq, QK_NOPE), F32)
    pad = jnp.zeros((seq, HEAD_PAD - QK_HEAD), F32)
    cos_x = jnp.concatenate([ones, cos2, pad], axis=-1)
    sin_x = jnp.concatenate([jnp.zeros_like(ones), sin2, pad], axis=-1)
    cos_c = jnp.concatenate([jnp.ones((ctx_len, QK_HEAD), F32), jnp.zeros((ctx_len, HEAD_PAD - QK_HEAD), F32)], axis=-1)
    sin_c = jnp.zeros((ctx_len, HEAD_PAD), F32)
    return jnp.concatenate([cos_x, cos_c], axis=0), jnp.concatenate([sin_x, sin_c], axis=0)


def _block_diag_const(n, blk, value):
    i = np.arange(n)
    return np.where((i[:, None] // blk) == (i[None, :] // blk), value, 0.0).astype(np.float32)


def _pick(n, candidates):
    for cand in candidates:
        if n % cand == 0:
            return cand
    raise ValueError(f"no tile for {n}")


def kernel(x, c, ctx, c_ctx, w_mod, b_mod, g_pre, g_post, w_in, g_q, g_kv, w_uq, w_ukv, conv_w, conv_b,
           w0, w_up, a0, a_up, k_k, k_a, r_k, gn_g, gn_b, w_br_mla, w_br_conv, w_br_rwkv, w_out):
    bsz, seq, d = x.shape
    ctx_len = ctx.shape[1]
    depth = w_mod.shape[0]
    assert d == D_MODEL and ctx_len == ROW_BLK and seq % ROW_BLK == 0
    t = seq + ctx_len
    n = bsz * t
    tiles_per_batch = t // ROW_BLK

    cc = jnp.zeros((8, d), F32).at[0].set(c_ctx).at[1:1 + bsz].set(c)
    mods = _adaln(cc, w_mod, b_mod).reshape(depth, 8, 1, 3 * d)

    w_in_p = _layout_w_in(w_in)
    wq_all, wk_all, wv_all = _layout_mla_weights(w_uq, w_ukv)
    cos_t, sin_t = _rope_tables(seq, ctx_len)
    vone = np.zeros((1, N_HEADS_MLA * HEAD_PAD), np.float32)
    vone[0, V_HEAD::HEAD_PAD] = 1.0
    vone = jnp.asarray(vone)
    ones4 = jnp.asarray(_block_diag_const(QUAD, RWKV_HEAD, 1.0), BF16)
    avg_bd = jnp.asarray(_block_diag_const(QUAD, RWKV_HEAD, 1.0 / RWKV_HEAD), BF16)

    tq = _pick(seq, (2048, 1024, 512, 256))
    tk = _pick(t, (768, 512, 256))

    z = jnp.concatenate([x, ctx], axis=1)
    for l in range(depth):
        z2 = z.reshape(n, d)
        p2 = _proj_in(z2, mods[l], g_pre[l][None], w_in_p[l], tiles_per_batch=tiles_per_batch)
        p3 = p2.reshape(bsz, t, PCOLS)

        qt, k, vt = _mla_prep(p3, cos_t, sin_t, g_q[l][None], g_kv[l][None], wq_all[l], wk_all[l], wv_all[l], vone)
        y_mla = _attention(qt, k, vt, None, tq=tq, tk=tk, nq=seq // tq, nk=t // tk, q_off=0, k_off=0)
        y_mla = _attention(qt, k, vt, y_mla, tq=ctx_len, tk=ctx_len, nq=1, nk=1,
                           q_off=seq // ctx_len, k_off=seq // ctx_len)

        rh, y0, g, h, bonus = _rwkv_prep(
            p2, w0[l][:, None], a0[l][:, None], w_up[l].astype(BF16), a_up[l].astype(BF16),
            k_k[l][None], k_a[l][None], r_k[l].reshape(1, W_RWKV), ones4)
        per_batch = lambda arr: arr.reshape(2, bsz, t, W_RWKV)
        yf, yb = _rwkv_scan(per_batch(rh), per_batch(y0), per_batch(g), per_batch(h),
                            n_x=seq // CHUNK, n_ctx=ctx_len // CHUNK)

        z2 = _merge(z2, mods[l], p2, y_mla.reshape(n, W_MLA), yf.reshape(n, W_RWKV), yb.reshape(n, W_RWKV),
                    bonus,
                    w_br_mla[l].astype(BF16), w_br_conv[l].astype(BF16), w_br_rwkv[l].astype(BF16),
                    w_out[l].astype(BF16), g_post[l][None], conv_w[l], conv_b[l][None],
                    gn_g[l][None], gn_b[l][None], avg_bd,
                    tiles_per_batch=tiles_per_batch, ctx_tiles=ctx_len // ROW_BLK, x_only=l == depth - 1)
        if l < depth - 1:
            z = z2.reshape(bsz, t, d)
    return z2.reshape(bsz, seq, d)
```

```python
import functools
import math

import numpy as np
import jax
import jax.numpy as jnp
from jax import lax
from jax.experimental import pallas as pl
from jax.experimental.pallas import tpu as pltpu

F32 = jnp.float32
BF16 = jnp.bfloat16
ACT = BF16

D_MODEL = 1024
DEPTH = 4
GRID_W = 64
N_HEADS_MLA = 8
Q_LORA = 384
KV_LORA = 256
QK_NOPE = 64
QK_ROPE = 32
QK_HEAD = QK_NOPE + QK_ROPE
V_HEAD = 64
W_MLA = N_HEADS_MLA * V_HEAD
ROPE_THETA = 10000.0
ATTN_SCALE = QK_HEAD ** -0.5
CONV_W = 512
RWKV_HEADS = 8
RWKV_HEAD = 64
W_RWKV = RWKV_HEADS * RWKV_HEAD
W_LORA = 64
A_LORA = 64
RMS_EPS = 1e-6
GN_EPS = 64e-5
L2_EPS = 1e-12
LOG2E = math.log2(math.e)

LANES = 128
ROW_BLK = 256
CHUNK = 64
QUAD = 4 * RWKV_HEAD
PREP_GROUP = 256
PREP_ROWS = 512
HEAD_PAD = 128
MXU_TILE = 256
HALO_ROWS = 16
VT_ROWS = 80
VMEM_LIMIT = 48 * 1024 * 1024

PCOLS = 17 * 512
PB_GATE = 0
PB_RKV = 2
PB_GRWKV = 9
PB_GMLA = 10
PB_CVIN = 11
PB_CVB = 12
PB_CVC = 13
PB_GCONV = 14
PB_Q = 15
PB_KV = 16
PB_LORA128 = (16 * 512 + 256) // 128


def _cparams(sem, vmem=VMEM_LIMIT):
    return pltpu.CompilerParams(dimension_semantics=sem, vmem_limit_bytes=vmem)


def _dot(a, b):
    return jnp.dot(a, b, preferred_element_type=F32)


def _dot_nt(a, b):
    return lax.dot_general(a, b, (((1,), (1,)), ((), ())), preferred_element_type=F32)


def _dot_tn(a, b):
    return lax.dot_general(a, b, (((0,), (0,)), ((), ())), preferred_element_type=F32)


def _split2(x):
    hi = x.astype(BF16)
    lo = (x - hi.astype(F32)).astype(BF16)
    return hi, lo


def _split3(x):
    hi = x.astype(BF16)
    r1 = x - hi.astype(F32)
    mid = r1.astype(BF16)
    lo = (r1 - mid.astype(F32)).astype(BF16)
    return hi, mid, lo


def _dot_exact_rhs(x, m_bf16):
    hi, mid, lo = _split3(x)
    return _dot(hi, m_bf16) + _dot(mid, m_bf16) + _dot(lo, m_bf16)


def _dot_hi(a, b):
    ah, al = _split2(a)
    bh, bl = _split2(b)
    return _dot(ah, bh) + _dot(ah, bl) + _dot(al, bh)


def _sigmoid(x):
    return 1.0 / (1.0 + jnp.exp(-x))


def _silu(x):
    return x * _sigmoid(x)


def _rms(x, g):
    return x * lax.rsqrt(jnp.mean(x * x, axis=-1, keepdims=True) + RMS_EPS) * g


def _adaln_kernel(c_ref, w_ref, b_ref, o_ref):
    a = _silu(c_ref[...])
    o_ref[0] = _dot_hi(a, w_ref[0]) + b_ref[0]


def _adaln(cc, w_mod, b_mod):
    depth, d, d3 = w_mod.shape
    tn = 1024
    return pl.pallas_call(
        _adaln_kernel,
        out_shape=jax.ShapeDtypeStruct((depth, 8, d3), F32),
        grid=(depth, d3 // tn),
        in_specs=[
            pl.BlockSpec((8, d), lambda l, j: (0, 0)),
            pl.BlockSpec((1, d, tn), lambda l, j: (l, 0, j)),
            pl.BlockSpec((1, 1, tn), lambda l, j: (l, 0, j)),
        ],
        out_specs=pl.BlockSpec((1, 8, tn), lambda l, j: (l, 0, j)),
        compiler_params=_cparams(("parallel", "parallel")),
        name="adaln",
    )(cc, w_mod, b_mod.reshape(depth, 1, d3))


def _seg_row(blk, tiles_per_batch):
    return jnp.where(blk % tiles_per_batch == tiles_per_batch - 1, 0, 1 + blk // tiles_per_batch)


def _proj_kernel(z_ref, *refs, nsub):
    mod_refs = refs[:2 * nsub]
    g_ref, w_ref, o_ref, h_ref = refs[2 * nsub:]

    @pl.when(pl.program_id(1) == 0)
    def _():
        y = _rms(z_ref[...], g_ref[...])
        for s in range(nsub):
            rows = slice(s * ROW_BLK, (s + 1) * ROW_BLK)
            shift, scale = mod_refs[2 * s][0], mod_refs[2 * s + 1][0]
            h_ref[rows, :] = (y[rows, :] * (1.0 + scale) + shift).astype(BF16)

    o_ref[...] = _dot(h_ref[...], w_ref[...]).astype(o_ref.dtype)


def _proj_in(z2, mods, g_pre, w, *, tiles_per_batch):
    n, d = z2.shape
    tm = 1024 if n % 1024 == 0 else ROW_BLK
    tn = PCOLS // 4
    nsub = tm // ROW_BLK
    mod_specs = []
    for s in range(nsub):
        for col in (0, 1):
            mod_specs.append(pl.BlockSpec(
                (1, 1, d), lambda i, j, s=s, col=col: (_seg_row(i * nsub + s, tiles_per_batch), 0, col)))
    return pl.pallas_call(
        functools.partial(_proj_kernel, nsub=nsub),
        out_shape=jax.ShapeDtypeStruct((n, PCOLS), ACT),
        grid=(n // tm, PCOLS // tn),
        in_specs=[pl.BlockSpec((tm, d), lambda i, j: (i, 0))] + mod_specs + [
            pl.BlockSpec((1, d), lambda i, j: (0, 0)),
            pl.BlockSpec((d, tn), lambda i, j: (0, j)),
        ],
        out_specs=pl.BlockSpec((tm, tn), lambda i, j: (i, j)),
        scratch_shapes=[pltpu.VMEM((tm, d), BF16)],
        compiler_params=_cparams(("parallel", "arbitrary")),
        name="proj_in",
    )(z2, *([mods] * (2 * nsub)), g_pre, w)


def _mla_prep_kernel(qb_ref, kb_ref, cos_ref, sin_ref, gq_ref, gkv_ref, wq_ref, wk_ref, wv_ref,
                     vone_ref, qt_ref, k_ref, vt_ref):
    nh = N_HEADS_MLA
    hw = nh * HEAD_PAD
    qb = qb_ref[0].astype(F32)
    kb = kb_ref[0].astype(F32)
    cos8 = jnp.tile(cos_ref[...], (1, nh))
    sin8 = jnp.tile(sin_ref[...], (1, nh))
    qn = _rms(qb[:, :Q_LORA], gq_ref[...]).astype(BF16)
    qq = _dot(qn, wq_ref[...])
    q = (qq[:, :hw] * cos8 + qq[:, hw:] * sin8) * (ATTN_SCALE * LOG2E)
    kvn = _rms(kb[:, :KV_LORA], gkv_ref[...]).astype(BF16)
    kin = jnp.concatenate([kvn, qb[:, Q_LORA:].astype(BF16)], axis=1)
    kk = _dot(kin, wk_ref[...])
    k_ref[0] = (kk[:, :hw] * cos8 + kk[:, hw:] * sin8).astype(BF16)
    v = _dot(kvn, wv_ref[...]) + vone_ref[...]
    for h in range(nh):
        lanes = slice(h * HEAD_PAD, (h + 1) * HEAD_PAD)
        qt_ref[0, h] = q[:, lanes].T.astype(BF16)
        vt_ref[0, h] = v[:, lanes].T[:VT_ROWS].astype(BF16)


def _mla_prep(p3, cos_t, sin_t, g_q, g_kv, wq, wk, wv, vone):
    b, t, _ = p3.shape
    tm = ROW_BLK
    hw = N_HEADS_MLA * HEAD_PAD
    const = lambda shape: pl.BlockSpec(shape, lambda bi, i: (0,) * len(shape))
    return pl.pallas_call(
        _mla_prep_kernel,
        out_shape=(
            jax.ShapeDtypeStruct((b, N_HEADS_MLA, HEAD_PAD, t), BF16),
            jax.ShapeDtypeStruct((b, t, hw), BF16),
            jax.ShapeDtypeStruct((b, N_HEADS_MLA, VT_ROWS, t), BF16),
        ),
        grid=(b, t // tm),
        in_specs=[
            pl.BlockSpec((1, tm, 512), lambda bi, i: (bi, i, PB_Q)),
            pl.BlockSpec((1, tm, 512), lambda bi, i: (bi, i, PB_KV)),
            pl.BlockSpec((tm, HEAD_PAD), lambda bi, i: (i, 0)),
            pl.BlockSpec((tm, HEAD_PAD), lambda bi, i: (i, 0)),
            const((1, Q_LORA)),
            const((1, KV_LORA)),
            const(wq.shape),
            const(wk.shape),
            const(wv.shape),
            const((1, hw)),
        ],
        out_specs=(
            pl.BlockSpec((1, N_HEADS_MLA, HEAD_PAD, tm), lambda bi, i: (bi, 0, 0, i)),
            pl.BlockSpec((1, tm, hw), lambda bi, i: (bi, i, 0)),
            pl.BlockSpec((1, N_HEADS_MLA, VT_ROWS, tm), lambda bi, i: (bi, 0, 0, i)),
        ),
        compiler_params=_cparams(("parallel", "parallel")),
        name="mla_prep",
    )(p3, p3, cos_t, sin_t, g_q, g_kv, wq, wk, wv, vone)


def _attn_kernel(qt_ref, k_ref, vt_ref, *rest, nk):
    o_ref, m_ref, acc_ref = rest[-3:]
    j = pl.program_id(2)

    @pl.when(j == 0)
    def _():
        m_ref[...] = jnp.full(m_ref.shape, -jnp.inf, F32)
        acc_ref[...] = jnp.zeros(acc_ref.shape, F32)

    nh = N_HEADS_MLA
    mt = MXU_TILE
    nqb = qt_ref.shape[3] // mt

    def qk_tiles(h):
        out = {}
        k_h = k_ref[0, :, h * HEAD_PAD:(h + 1) * HEAD_PAD]

        def make(n):
            def run():
                out[n] = _dot(k_h, qt_ref[0, h, :, n * mt:(n + 1) * mt])
            return run
        return out, [make(n) for n in range(nqb)]

    def softmax(h, s):
        p, alpha = {}, {}
        for n in range(nqb):
            lanes = slice(n * mt, (n + 1) * mt)
            m_prev = m_ref[h, :, lanes]
            m_new = jnp.maximum(m_prev, jnp.max(s[n], axis=0, keepdims=True))
            alpha[n] = jnp.exp2(m_prev - m_new)[0:1, :]
            p[n] = jnp.exp2(s[n] - m_new[0:1, :]).astype(BF16)
            m_ref[h, :, lanes] = m_new
        return p, alpha

    def pv_tiles(h, p, alpha):
        vt_h = vt_ref[0, h]

        def make(n):
            def run():
                lanes = slice(n * mt, (n + 1) * mt)
                acc_ref[h, :, lanes] = alpha[n] * acc_ref[h, :, lanes] + _dot(vt_h, p[n])
            return run
        return [make(n) for n in range(nqb)]

    def run_interleaved(a_ops, c_ops):
        for i in range(max(len(a_ops), len(c_ops))):
            if i < len(a_ops):
                a_ops[i]()
            if i < len(c_ops):
                c_ops[i]()

    s, pa = {}, {}
    for h in range(min(2, nh)):
        s[h], ops = qk_tiles(h)
        run_interleaved(ops, [])
    pa[0] = softmax(0, s[0])
    for h in range(nh):
        if h + 1 < nh:
            pa[h + 1] = softmax(h + 1, s[h + 1])
        a_ops = []
        if h + 2 < nh:
            s[h + 2], a_ops = qk_tiles(h + 2)
        run_interleaved(a_ops, pv_tiles(h, *pa[h]))

    @pl.when(j == nk - 1)
    def _():
        outs = []
        for h in range(N_HEADS_MLA):
            a = acc_ref[h]
            outs.append(a[:V_HEAD] * (1.0 / a[V_HEAD:V_HEAD + 1]))
        o_ref[0] = jnp.concatenate(outs, axis=0).T.astype(o_ref.dtype)


def _attention(qt, k, vt, y_prev, *, tq, tk, nq, nk, q_off, k_off):
    b, t, hw = k.shape
    in_specs = [
        pl.BlockSpec((1, N_HEADS_MLA, HEAD_PAD, tq), lambda bi, i, j: (bi, 0, 0, i + q_off)),
        pl.BlockSpec((1, tk, hw), lambda bi, i, j: (bi, j + k_off, 0)),
        pl.BlockSpec((1, N_HEADS_MLA, VT_ROWS, tk), lambda bi, i, j: (bi, 0, 0, j + k_off)),
    ]
    args = [qt, k, vt]
    aliases = {}
    if y_prev is not None:
        in_specs.append(pl.BlockSpec(memory_space=pl.ANY))
        args.append(y_prev)
        aliases = {3: 0}
    return pl.pallas_call(
        functools.partial(_attn_kernel, nk=nk),
        out_shape=jax.ShapeDtypeStruct((b, t, W_MLA), ACT),
        grid=(b, nq, nk),
        in_specs=in_specs,
        out_specs=pl.BlockSpec((1, tq, W_MLA), lambda bi, i, j: (bi, i + q_off, 0)),
        scratch_shapes=[
            pltpu.VMEM((N_HEADS_MLA, 8, tq), F32),
            pltpu.VMEM((N_HEADS_MLA, VT_ROWS, tq), F32),
        ],
        input_output_aliases=aliases,
        compiler_params=_cparams(("parallel", "parallel", "arbitrary")),
        name="attn_ctx" if y_prev is not None else "attn_x",
    )(*args)


def _half_lane_masks():
    lane = lax.broadcasted_iota(jnp.int32, (CHUNK, LANES), 1)
    first = lane < RWKV_HEAD
    return first, jnp.where(first, 1.0, 0.0).astype(BF16), jnp.where(first, 0.0, 1.0).astype(BF16)


def _bd4(x, lo, hi):
    xb = x.astype(BF16)
    xl, xr = xb[:, :LANES], xb[:, LANES:]
    z = jnp.zeros((2 * CHUNK, LANES), BF16)
    c0 = jnp.concatenate([xl * lo, xl * hi, z], axis=0)
    c1 = jnp.concatenate([z, xr * lo, xr * hi], axis=0)
    return jnp.concatenate([c0, c1], axis=1)


def _diag_blocks(full, first):
    c = CHUNK
    left = jnp.where(first, full[0:c, :LANES], full[c:2 * c, :LANES])
    right = jnp.where(first, full[2 * c:3 * c, LANES:], full[3 * c:4 * c, LANES:])
    return jnp.concatenate([left, right], axis=1)


def _rwkv_prep_kernel(rkv_ref, lora_ref, w0_ref, a0_ref, wup_ref, aup_ref, kk_ref, ka_ref, rk_ref,
                      ones_ref, rh_ref, y0_ref, g_ref, h_ref, bonus_ref):
    fwd = pl.program_id(0) == 0
    c = CHUNK
    w = W_RWKV
    grp = PREP_GROUP
    ones4 = ones_ref[...]
    sgn = jnp.where(fwd, 1, -1)
    ti = lax.broadcasted_iota(jnp.int32, (grp, grp), 0)
    si = lax.broadcasted_iota(jnp.int32, (grp, grp), 1)
    same = jnp.where((ti // c) == (si // c), 1.0, 0.0)
    tri = jnp.where((si - ti) * sgn <= 0, same, 0.0).astype(BF16)
    first, lo, hi = _half_lane_masks()
    tq = lax.broadcasted_iota(jnp.int32, (c, QUAD), 0)
    sq = lax.broadcasted_iota(jnp.int32, (c, QUAD), 1) % c
    before = (sq - tq) * sgn < 0
    upto = (sq - tq) * sgn <= 0
    eye = jnp.where(sq == tq, 1.0, 0.0)

    def head_sum(x):
        xh, xl = _split2(x)
        parts = []
        for qd in range(w // QUAD):
            ls = slice(qd * QUAD, (qd + 1) * QUAD)
            parts.append(_dot(xh[:, ls], ones4) + _dot(xl[:, ls], ones4))
        return jnp.concatenate(parts, axis=1)

    def pm(x, y):
        return _dot(x.astype(BF16), _bd4(y, lo, hi))

    def prologue(g, out):
        gs = slice(g * grp, (g + 1) * grp)
        rkv = rkv_ref[gs, :].astype(F32)
        r, k, v = rkv[:, :w], rkv[:, w:2 * w], rkv[:, 2 * w:]
        lora = lora_ref[gs, :].astype(F32)
        zw = w0_ref[0] + _dot(jnp.tanh(lora[:, :W_LORA]).astype(BF16), wup_ref[0])
        za = a0_ref[0] + _dot(lora[:, W_LORA:].astype(BF16), aup_ref[0])
        yield
        ell = -math.exp(-0.5) * _sigmoid(zw)
        a = _sigmoid(za)
        kkr = k * kk_ref[...]
        k_d = k * (1.0 + (a - 1.0) * ka_ref[...])
        kk_ss = head_sum(kkr * kkr)
        rk_s = head_sum(r * k_d * rk_ref[...])
        yield
        kk = kkr * lax.rsqrt(kk_ss + L2_EPS)
        bonus_ref[0, gs, :] = (rk_s * v).astype(bonus_ref.dtype)
        ell_hi, ell_lo = _split2(ell)
        lc = _dot(tri, ell_hi) + _dot(tri, ell_lo)
        yield
        ltot = jnp.concatenate(
            [jnp.broadcast_to(jnp.where(fwd, lc[ch * c + c - 1:ch * c + c], lc[ch * c:ch * c + 1]), (c, w))
             for ch in range(grp // c)], axis=0)
        e_neg = jnp.exp(-lc)
        e_tail = jnp.exp(ltot - lc)
        kka = kk * a
        out.update(abar=kk * jnp.exp(lc - ell), bbar=kka * e_neg, kbar=k_d * e_neg, rbar=r * jnp.exp(lc),
                   btil=kka * e_tail, ktil=k_d * e_tail, v=v, gam=jnp.exp(ltot))

    def chains(g, q):
        sl = [(slice(ch * c, (ch + 1) * c), slice(qd * QUAD, (qd + 1) * QUAD))
              for ch in range(grp // c) for qd in range(w // QUAD)]
        abar, bbar, kbar, rbar, btil, ktil, v, gam = (
            q[nm] for nm in ("abar", "bbar", "kbar", "rbar", "btil", "ktil", "v", "gam"))
        la = [jnp.concatenate([abar[s], rbar[s]], axis=0).astype(BF16) for s in sl]
        nb = [_dot_nt(la_i, _bd4(bbar[s], lo, hi)) for la_i, s in zip(la, sl)]
        yield
        nk = [_dot_nt(la_i, _bd4(kbar[s], lo, hi)) for la_i, s in zip(la, sl)]
        yield
        n = [jnp.where(before, t[:c], 0.0) for t in nb]
        a_rb = [jnp.where(upto, t[c:], 0.0) for t in nb]
        a_ak = [jnp.where(before, t[:c], 0.0) for t in nk]
        a_rk = [jnp.where(upto, t[c:], 0.0) for t in nk]
        x = [eye - t for t in n]
        p = [pm(t, t) for t in n]
        yield
        py = [pm(jnp.concatenate([u, r_], axis=0), v[s]) for u, r_, s in zip(a_ak, a_rk, sl)]
        yield
        for it in range(5):
            if it < 4:
                xp = [pm(jnp.concatenate([x_i, p_i], axis=0), p_i) for x_i, p_i in zip(x, p)]
                x = [x_i + t[:c] for x_i, t in zip(x, xp)]
                p = [t[c:] for t in xp]
            else:
                x = [x_i + pm(x_i, p_i) for x_i, p_i in zip(x, p)]
            yield
        ah = [pm(x_i, abar[s]) for x_i, s in zip(x, sl)]
        u0 = [pm(x_i, t[:c]) for x_i, t in zip(x, py)]
        yield
        ra = [pm(m_i, t) for m_i, t in zip(a_rb, ah)]
        ru = [pm(m_i, t) for m_i, t in zip(a_rb, u0)]
        gfull = [_dot_tn(btil[s].astype(BF16), t.astype(BF16)) for s, t in zip(sl, ah)]
        hfull = [_dot_tn(jnp.concatenate([ktil[s], -btil[s]], axis=0).astype(BF16),
                         jnp.concatenate([v[s], t], axis=0).astype(BF16)) for s, t in zip(sl, u0)]
        for i, (rs, ls) in enumerate(sl):
            ro = slice(g * grp + rs.start, g * grp + rs.stop)
            rh_ref[0, ro, ls] = (rbar[rs, ls] - ra[i]).astype(BF16)
            y0_ref[0, ro, ls] = py[i][c:] - ru[i]
            g_ref[0, ro, ls] = (eye * gam[rs, ls] - _diag_blocks(gfull[i], first)).astype(g_ref.dtype)
            h_ref[0, ro, ls] = _diag_blocks(hfull[i], first)

    ngroups = rkv_ref.shape[0] // grp
    vals = [dict() for _ in range(ngroups)]
    for _ in prologue(0, vals[0]):
        pass
    for g in range(ngroups):
        nxt = prologue(g + 1, vals[g + 1]) if g + 1 < ngroups else iter(())
        for stage, _ in enumerate(chains(g, vals[g])):
            if stage >= 1 and stage % 2 == 1:
                next(nxt, None)
        for _ in nxt:
            pass


def _rwkv_prep(p2, w0, a0, w_up, a_up, k_k, k_a, r_k, ones4):
    n, _ = p2.shape
    w = W_RWKV
    rows = PREP_ROWS if n % PREP_ROWS == 0 else PREP_GROUP
    const = lambda shape: pl.BlockSpec(shape, lambda d, i: (0,) * len(shape))
    perdir = lambda shape: pl.BlockSpec((1,) + shape, lambda d, i: (d,) + (0,) * len(shape))
    out_spec = pl.BlockSpec((1, rows, w), lambda d, i: (d, i, 0))
    f32_out = jax.ShapeDtypeStruct((2, n, w), F32)
    act_out = jax.ShapeDtypeStruct((2, n, w), ACT)
    return pl.pallas_call(
        _rwkv_prep_kernel,
        out_shape=(act_out, f32_out, act_out, f32_out, act_out),
        grid=(2, n // rows),
        in_specs=[
            pl.BlockSpec((rows, 3 * w), lambda d, i: (i, PB_RKV)),
            pl.BlockSpec((rows, 2 * W_LORA), lambda d, i: (i, PB_LORA128 + d)),
            perdir((1, w)), perdir((1, w)), perdir((W_LORA, w)), perdir((A_LORA, w)),
            const((1, w)), const((1, w)), const((1, w)),
            const((QUAD, QUAD)),
        ],
        out_specs=(out_spec,) * 5,
        compiler_params=_cparams(("parallel", "parallel")),
        name="rwkv_prep",
    )(p2, p2, w0, a0, w_up, a_up, k_k, k_a, r_k, ones4)


def _rwkv_scan_kernel(rhf, y0f, gf, hf, rhb, y0b, gb, hb, yf_ref, yb_ref, st_ref, *, nb):
    @pl.when(pl.program_id(0) == 0)
    def _():
        st_ref[...] = jnp.zeros(st_ref.shape, F32)

    _, lo, hi = _half_lane_masks()
    c = CHUNK
    nsub = rhf.shape[2] // c
    for step in range(nsub):
        for d, sub, (rh, y0, g, hh, y_ref) in ((0, step, (rhf, y0f, gf, hf, yf_ref)),
                                               (1, nsub - 1 - step, (rhb, y0b, gb, hb, yb_ref))):
            rs = slice(sub * c, (sub + 1) * c)
            for bi in range(nb):
                for qd in range(W_RWKV // QUAD):
                    ls = slice(qd * QUAD, (qd + 1) * QUAD)
                    lhs = jnp.concatenate([rh[0, bi, rs, ls], g[0, bi, rs, ls]], axis=0)
                    res = _dot(lhs, _bd4(st_ref[d, bi, qd], lo, hi))
                    y_ref[bi, rs, ls] = (res[:c] + y0[0, bi, rs, ls]).astype(y_ref.dtype)
                    st_ref[d, bi, qd] = res[c:] + hh[0, bi, rs, ls]


def _rwkv_scan(rh, y0, g, h, *, n_x, n_ctx):
    _, b, t, w = rh.shape
    c = ROW_BLK
    n = n_x + n_ctx
    fidx = lambda j: jnp.where(j < n_ctx, n_x + j, j - n_ctx)
    bidx = lambda j: n - 1 - j
    fspec = pl.BlockSpec((1, b, c, w), lambda j: (0, 0, fidx(j), 0))
    bspec = pl.BlockSpec((1, b, c, w), lambda j: (1, 0, bidx(j), 0))
    y_shape = jax.ShapeDtypeStruct((b, t, w), ACT)
    return pl.pallas_call(
        functools.partial(_rwkv_scan_kernel, nb=b),
        out_shape=(y_shape, y_shape),
        grid=(n,),
        in_specs=[fspec] * 4 + [bspec] * 4,
        out_specs=(
            pl.BlockSpec((b, c, w), lambda j: (0, fidx(j), 0)),
            pl.BlockSpec((b, c, w), lambda j: (0, bidx(j), 0)),
        ),
        scratch_shapes=[pltpu.VMEM((2, b, w // QUAD, RWKV_HEAD, QUAD), F32)],
        compiler_params=_cparams(("arbitrary",)),
        name="rwkv_scan",
    )(rh, y0, g, h, rh, y0, g, h)


def _merge_kernel(z_ref, gate_ref, gl_ref, grw_ref, gml_ref, cin_ref, cb_ref, cc_ref, gcv_ref,
                  cin_p, cc_p, cin_n, cc_n, ymla_ref, yf_ref, yb_ref, bon_ref,
                  wbm_ref, wbc_ref, wbr_ref, wout_ref, gpost_ref, cw_ref, cbias_ref, gng_ref, gnb_ref,
                  avg_ref, o_ref, *, tiles_per_batch, ctx_tiles, x_only):
    tm = z_ref.shape[0]
    d = D_MODEL
    x_tiles = tiles_per_batch - ctx_tiles
    pos = pl.program_id(0) % (x_tiles if x_only else tiles_per_batch)
    first = jnp.logical_or(pos == 0, pos == x_tiles)
    last = jnp.logical_or(pos == x_tiles - 1, pos == tiles_per_batch - 1)

    def f32(ref, idx=slice(None)):
        return ref[idx].astype(F32)

    u = f32(cc_ref) * f32(cin_ref)
    hl = HALO_ROWS - 1
    u_halo_p = jnp.where(first, 0.0, f32(cc_p, slice(hl, hl + 1)) * f32(cin_p, slice(hl, hl + 1)))
    u_halo_n = jnp.where(last, 0.0, f32(cc_n, slice(0, 1)) * f32(cin_n, slice(0, 1)))
    row = lax.broadcasted_iota(jnp.int32, u.shape, 0)
    u_prev = jnp.where(row == 0, u_halo_p, pltpu.roll(u, 1, axis=0))
    u_next = jnp.where(row == tm - 1, u_halo_n, pltpu.roll(u, tm - 1, axis=0))
    cw = cw_ref[...]
    y_conv = f32(cb_ref) * (u_prev * cw[0:1] + u * cw[1:2] + u_next * cw[2:3] + cbias_ref[...])

    avg = avg_ref[...]

    def head_mean(x):
        hi, lo = _split2(x)
        parts = []
        for qd in range(W_RWKV // QUAD):
            ls = slice(qd * QUAD, (qd + 1) * QUAD)
            parts.append(_dot(hi[:, ls], avg) + _dot(lo[:, ls], avg))
        return jnp.concatenate(parts, axis=1)

    yr = f32(yf_ref) + f32(yb_ref)
    mu = head_mean(yr)
    dv = yr - mu
    var = head_mean(dv * dv)
    y_rwkv = dv * lax.rsqrt(var + GN_EPS) * gng_ref[...] + gnb_ref[...] + f32(bon_ref, 0) + f32(bon_ref, 1)

    br_mla = _dot((f32(ymla_ref) * _silu(f32(gml_ref))).astype(BF16), wbm_ref[...])
    br_conv = _dot((y_conv * _silu(f32(gcv_ref))).astype(BF16), wbc_ref[...])
    br_rwkv = _dot((y_rwkv * _silu(f32(grw_ref))).astype(BF16), wbr_ref[...])
    s = _sigmoid(f32(gl_ref))
    merged = s[:, :d] * br_mla + s[:, d:2 * d] * br_conv + s[:, 2 * d:] * br_rwkv
    o = _dot(merged.astype(BF16), wout_ref[...])
    o_ref[...] = z_ref[...] + gate_ref[0] * _rms(o, gpost_ref[...])


def _merge(z2, mods, p2, y_mla, yf, yb, bonus, wbm, wbc, wbr, wout, g_post, conv_w, conv_b,
           gn_g, gn_b, avg_bd, *, tiles_per_batch, ctx_tiles, x_only):
    n, d = z2.shape
    tm = ROW_BLK
    hb = tm // HALO_ROWS
    nhb = n // HALO_ROWS
    x_tiles = tiles_per_batch - ctx_tiles
    n_tiles = (n // tm) // tiles_per_batch * x_tiles if x_only else n // tm
    gi = (lambda i: i // x_tiles * tiles_per_batch + i % x_tiles) if x_only else (lambda i: i)
    pcol = lambda blk: pl.BlockSpec((tm, 512), lambda i: (gi(i), blk))
    prev = lambda blk: pl.BlockSpec((HALO_ROWS, 512), lambda i: (jnp.maximum(gi(i) * hb - 1, 0), blk))
    nxt = lambda blk: pl.BlockSpec((HALO_ROWS, 512), lambda i: (jnp.minimum((gi(i) + 1) * hb, nhb - 1), blk))
    const = lambda shape: pl.BlockSpec(shape, lambda i: (0,) * len(shape))
    row512 = pl.BlockSpec((tm, 512), lambda i: (gi(i), 0))
    return pl.pallas_call(
        functools.partial(_merge_kernel, tiles_per_batch=tiles_per_batch, ctx_tiles=ctx_tiles, x_only=x_only),
        out_shape=jax.ShapeDtypeStruct((n_tiles * tm, d), F32),
        grid=(n_tiles,),
        in_specs=[
            pl.BlockSpec((tm, d), lambda i: (gi(i), 0)),
            pl.BlockSpec((1, 1, d), lambda i: (_seg_row(gi(i), tiles_per_batch), 0, 2)),
            pl.BlockSpec((tm, 3 * d), lambda i: (gi(i), PB_GATE)),
            pcol(PB_GRWKV), pcol(PB_GMLA), pcol(PB_CVIN), pcol(PB_CVB), pcol(PB_CVC), pcol(PB_GCONV),
            prev(PB_CVIN), prev(PB_CVC), nxt(PB_CVIN), nxt(PB_CVC),
            row512, row512, row512,
            pl.BlockSpec((2, tm, 512), lambda i: (0, gi(i), 0)),
            const(wbm.shape), const(wbc.shape), const(wbr.shape), const(wout.shape),
            const((1, d)), const(conv_w.shape), const((1, 512)), const((1, 512)), const((1, 512)),
            const(avg_bd.shape),
        ],
        out_specs=pl.BlockSpec((tm, d), lambda i: (i, 0)),
        compiler_params=_cparams(("parallel",)),
        name="merge",
    )(z2, mods, p2, p2, p2, p2, p2, p2, p2, p2, p2, p2, p2, y_mla, yf, yb, bonus,
      wbm, wbc, wbr, wout, g_post, conv_w, conv_b, gn_g, gn_b, avg_bd)


def _pair_swap(w):
    s = w.shape
    return w.reshape(s[:-1] + (s[-1] // 2, 2))[..., ::-1].reshape(s)


def _layout_w_in(w_in):
    sizes = (Q_LORA, KV_LORA, QK_ROPE, W_MLA, CONV_W, CONV_W, CONV_W, CONV_W, W_RWKV, W_RWKV, W_RWKV,
             W_LORA, W_LORA, A_LORA, A_LORA, W_RWKV, 3 * D_MODEL)
    offs = np.concatenate([[0], np.cumsum(sizes)])
    names = ("q_lat", "kv_lat", "kr", "g_mla", "cv_in", "cv_b", "cv_c", "g_conv", "r", "k", "v",
             "wd_f", "wd_b", "ad_f", "ad_b", "g_rwkv", "gl")
    col = {nm: w_in[..., offs[i]:offs[i + 1]] for i, nm in enumerate(names)}
    zeros = jnp.zeros(w_in.shape[:-1] + (512 - Q_LORA - 2 * QK_ROPE,), w_in.dtype)
    parts = [col["gl"], col["r"], col["k"], col["v"], col["g_rwkv"], col["g_mla"], col["cv_in"],
             col["cv_b"], col["cv_c"], col["g_conv"],
             col["q_lat"], col["kr"], _pair_swap(col["kr"]), zeros,
             col["kv_lat"], col["wd_f"], col["ad_f"], col["wd_b"], col["ad_b"]]
    out = jnp.concatenate(parts, axis=-1).astype(BF16)
    assert out.shape[-1] == PCOLS
    return out


def _layout_mla_weights(w_uq, w_ukv):
    depth = w_uq.shape[0]
    nh = N_HEADS_MLA
    wq = w_uq.reshape(depth, Q_LORA, nh, QK_HEAD)
    q_nope, q_rope = wq[..., :QK_NOPE], wq[..., QK_NOPE:]
    zq = jnp.zeros((depth, Q_LORA, nh, HEAD_PAD - QK_HEAD), w_uq.dtype)
    wq_a = jnp.concatenate([q_nope, q_rope, zq], axis=-1)
    wq_b = jnp.concatenate([jnp.zeros_like(q_nope), _pair_swap(q_rope), zq], axis=-1)
    wq_all = jnp.concatenate([wq_a.reshape(depth, Q_LORA, nh * HEAD_PAD),
                              wq_b.reshape(depth, Q_LORA, nh * HEAD_PAD)], axis=-1).astype(BF16)

    wkv = w_ukv.reshape(depth, KV_LORA, nh, QK_NOPE + V_HEAD)
    k_nope, v_w = wkv[..., :QK_NOPE], wkv[..., QK_NOPE:]
    zk = jnp.zeros((depth, KV_LORA, nh, HEAD_PAD - QK_NOPE), w_ukv.dtype)
    wk_top = jnp.concatenate([k_nope, zk], axis=-1).reshape(depth, KV_LORA, nh * HEAD_PAD)
    wv = jnp.concatenate([v_w, zk], axis=-1).reshape(depth, KV_LORA, nh * HEAD_PAD).astype(BF16)
    place = np.zeros((HEAD_PAD, HEAD_PAD), np.float32)
    place[np.arange(QK_ROPE), QK_NOPE + np.arange(QK_ROPE)] = 1.0
    e_a = np.tile(place, (1, nh))
    place_b = np.zeros((HEAD_PAD, HEAD_PAD), np.float32)
    place_b[QK_ROPE + np.arange(QK_ROPE), QK_NOPE + np.arange(QK_ROPE)] = 1.0
    e_b = np.tile(place_b, (1, nh))
    top = jnp.concatenate([wk_top, jnp.zeros_like(wk_top)], axis=-1)
    bot = jnp.broadcast_to(jnp.asarray(np.concatenate([e_a, e_b], axis=1)), (depth, HEAD_PAD, 2 * nh * HEAD_PAD))
    wk_all = jnp.concatenate([top, bot.astype(top.dtype)], axis=1).astype(BF16)
    return wq_all, wk_all, wv


def _rope_tables(seq, ctx_len):
    n_freq = QK_ROPE // 4
    pos = np.arange(seq)
    inv = ROPE_THETA ** (-np.arange(n_freq, dtype=np.float32) / n_freq)
    row = (pos // GRID_W).astype(np.float32)
    colp = (pos % GRID_W).astype(np.float32)
    ang = jnp.concatenate([jnp.asarray(row)[:, None] * jnp.asarray(inv), jnp.asarray(colp)[:, None] * jnp.asarray(inv)], axis=-1)
    cos, sin = jnp.cos(ang), jnp.sin(ang)
    cos2 = jnp.repeat(cos, 2, axis=-1)
    sin2 = jnp.stack([-sin, sin], axis=-1).reshape(seq, QK_ROPE)
    ones = jnp.ones((seq, QK_NOPE), F32)
    pad = jnp.zeros((seq, HEAD_PAD - QK_HEAD), F32)
    cos_x = jnp.concatenate([ones, cos2, pad], axis=-1)
    sin_x = jnp.concatenate([jnp.zeros_like(ones), sin2, pad], axis=-1)
    cos_c = jnp.concatenate([jnp.ones((ctx_len, QK_HEAD), F32), jnp.zeros((ctx_len, HEAD_PAD - QK_HEAD), F32)], axis=-1)
    sin_c = jnp.zeros((ctx_len, HEAD_PAD), F32)
    return jnp.concatenate([cos_x, cos_c], axis=0), jnp.concatenate([sin_x, sin_c], axis=0)


def _block_diag_const(n, blk, value):
    i = np.arange(n)
    return np.where((i[:, None] // blk) == (i[None, :] // blk), value, 0.0).astype(np.float32)


def _pick(n, candidates):
    for cand in candidates:
        if n % cand == 0:
            return cand
    raise ValueError(f"no tile for {n}")


def kernel(x, c, ctx, c_ctx, w_mod, b_mod, g_pre, g_post, w_in, g_q, g_kv, w_uq, w_ukv, conv_w, conv_b,
           w0, w_up, a0, a_up, k_k, k_a, r_k, gn_g, gn_b, w_br_mla, w_br_conv, w_br_rwkv, w_out):
    bsz, seq, d = x.shape
    ctx_len = ctx.shape[1]
    depth = w_mod.shape[0]
    assert d == D_MODEL and ctx_len == ROW_BLK and seq % ROW_BLK == 0
    t = seq + ctx_len
    n = bsz * t
    tiles_per_batch = t // ROW_BLK

    cc = jnp.zeros((8, d), F32).at[0].set(c_ctx).at[1:1 + bsz].set(c)
    mods = _adaln(cc, w_mod, b_mod).reshape(depth, 8, 1, 3 * d)

    w_in_p = _layout_w_in(w_in)
    wq_all, wk_all, wv_all = _layout_mla_weights(w_uq, w_ukv)
    cos_t, sin_t = _rope_tables(seq, ctx_len)
    vone = np.zeros((1, N_HEADS_MLA * HEAD_PAD), np.float32)
    vone[0, V_HEAD::HEAD_PAD] = 1.0
    vone = jnp.asarray(vone)
    ones4 = jnp.asarray(_block_diag_const(QUAD, RWKV_HEAD, 1.0), BF16)
    avg_bd = jnp.asarray(_block_diag_const(QUAD, RWKV_HEAD, 1.0 / RWKV_HEAD), BF16)

    tq = _pick(seq, (2048, 1024, 512, 256))
    tk = _pick(t, (768, 512, 256))

    z = jnp.concatenate([x, ctx], axis=1)
    for l in range(depth):
        z2 = z.reshape(n, d)
        p2 = _proj_in(z2, mods[l], g_pre[l][None], w_in_p[l], tiles_per_batch=tiles_per_batch)
        p3 = p2.reshape(bsz, t, PCOLS)

        qt, k, vt = _mla_prep(p3, cos_t, sin_t, g_q[l][None], g_kv[l][None], wq_all[l], wk_all[l], wv_all[l], vone)
        y_mla = _attention(qt, k, vt, None, tq=tq, tk=tk, nq=seq // tq, nk=t // tk, q_off=0, k_off=0)
        y_mla = _attention(qt, k, vt, y_mla, tq=ctx_len, tk=ctx_len, nq=1, nk=1,
                           q_off=seq // ctx_len, k_off=seq // ctx_len)

        rh, y0, g, h, bonus = _rwkv_prep(
            p2, w0[l][:, None], a0[l][:, None], w_up[l].astype(BF16), a_up[l].astype(BF16),
            k_k[l][None], k_a[l][None], r_k[l].reshape(1, W_RWKV), ones4)
        per_batch = lambda arr: arr.reshape(2, bsz, t, W_RWKV)
        yf, yb = _rwkv_scan(per_batch(rh), per_batch(y0), per_batch(g), per_batch(h),
                            n_x=seq // ROW_BLK, n_ctx=ctx_len // ROW_BLK)

        z2 = _merge(z2, mods[l], p2, y_mla.reshape(n, W_MLA), yf.reshape(n, W_RWKV), yb.reshape(n, W_RWKV),
                    bonus,
                    w_br_mla[l].astype(BF16), w_br_conv[l].astype(BF16), w_br_rwkv[l].astype(BF16),
                    w_out[l].astype(BF16), g_post[l][None], conv_w[l], conv_b[l][None],
                    gn_g[l][None], gn_b[l][None], avg_bd,
                    tiles_per_batch=tiles_per_batch, ctx_tiles=ctx_len // ROW_BLK, x_only=l == depth - 1)
        if l < depth - 1:
            z = z2.reshape(bsz, t, d)
    return z2.reshape(bsz, seq, d)
```

```python
import functools
import math

import numpy as np
import jax
import jax.numpy as jnp
from jax import lax
from jax.experimental import pallas as pl
from jax.experimental.pallas import tpu as pltpu

F32 = jnp.float32
BF16 = jnp.bfloat16
ACT = BF16

D_MODEL = 1024
DEPTH = 4
GRID_W = 64
N_HEADS_MLA = 8
Q_LORA = 384
KV_LORA = 256
QK_NOPE = 64
QK_ROPE = 32
QK_HEAD = QK_NOPE + QK_ROPE
V_HEAD = 64
W_MLA = N_HEADS_MLA * V_HEAD
ROPE_THETA = 10000.0
ATTN_SCALE = QK_HEAD ** -0.5
CONV_W = 512
RWKV_HEADS = 8
RWKV_HEAD = 64
W_RWKV = RWKV_HEADS * RWKV_HEAD
W_LORA = 64
A_LORA = 64
RMS_EPS = 1e-6
GN_EPS = 64e-5
L2_EPS = 1e-12
LOG2E = math.log2(math.e)

LANES = 128
ROW_BLK = 256
CHUNK = 64
QUAD = 4 * RWKV_HEAD
PREP_GROUP = 256
PREP_ROWS = 512
HEAD_PAD = 128
MXU_TILE = 256
HALO_ROWS = 16
VT_ROWS = 80
VMEM_LIMIT = 48 * 1024 * 1024

PCOLS = 17 * 512
PB_GATE = 0
PB_RKV = 2
PB_GRWKV = 9
PB_GMLA = 10
PB_CVIN = 11
PB_CVB = 12
PB_CVC = 13
PB_GCONV = 14
PB_Q = 15
PB_KV = 16
PB_LORA128 = (16 * 512 + 256) // 128


def _cparams(sem, vmem=VMEM_LIMIT):
    return pltpu.CompilerParams(dimension_semantics=sem, vmem_limit_bytes=vmem)


def _dot(a, b):
    return jnp.dot(a, b, preferred_element_type=F32)


def _dot_nt(a, b):
    return lax.dot_general(a, b, (((1,), (1,)), ((), ())), preferred_element_type=F32)


def _dot_tn(a, b):
    return lax.dot_general(a, b, (((0,), (0,)), ((), ())), preferred_element_type=F32)


def _split2(x):
    hi = x.astype(BF16)
    lo = (x - hi.astype(F32)).astype(BF16)
    return hi, lo


def _split3(x):
    hi = x.astype(BF16)
    r1 = x - hi.astype(F32)
    mid = r1.astype(BF16)
    lo = (r1 - mid.astype(F32)).astype(BF16)
    return hi, mid, lo


def _dot_exact_rhs(x, m_bf16):
    hi, mid, lo = _split3(x)
    return _dot(hi, m_bf16) + _dot(mid, m_bf16) + _dot(lo, m_bf16)


def _dot_hi(a, b):
    ah, al = _split2(a)
    bh, bl = _split2(b)
    return _dot(ah, bh) + _dot(ah, bl) + _dot(al, bh)


def _sigmoid(x):
    return 1.0 / (1.0 + jnp.exp(-x))


def _silu(x):
    return x * _sigmoid(x)


def _rms(x, g):
    return x * lax.rsqrt(jnp.mean(x * x, axis=-1, keepdims=True) + RMS_EPS) * g


def _adaln_kernel(c_ref, w_ref, b_ref, o_ref):
    a = _silu(c_ref[...])
    o_ref[0] = _dot_hi(a, w_ref[0]) + b_ref[0]


def _adaln(cc, w_mod, b_mod):
    depth, d, d3 = w_mod.shape
    tn = 1024
    return pl.pallas_call(
        _adaln_kernel,
        out_shape=jax.ShapeDtypeStruct((depth, 8, d3), F32),
        grid=(depth, d3 // tn),
        in_specs=[
            pl.BlockSpec((8, d), lambda l, j: (0, 0)),
            pl.BlockSpec((1, d, tn), lambda l, j: (l, 0, j)),
            pl.BlockSpec((1, 1, tn), lambda l, j: (l, 0, j)),
        ],
        out_specs=pl.BlockSpec((1, 8, tn), lambda l, j: (l, 0, j)),
        compiler_params=_cparams(("parallel", "parallel")),
        name="adaln",
    )(cc, w_mod, b_mod.reshape(depth, 1, d3))


def _seg_row(blk, tiles_per_batch):
    return jnp.where(blk % tiles_per_batch == tiles_per_batch - 1, 0, 1 + blk // tiles_per_batch)


def _proj_kernel(z_ref, *refs, nsub):
    mod_refs = refs[:2 * nsub]
    g_ref, w_ref, o_ref, h_ref = refs[2 * nsub:]

    @pl.when(pl.program_id(1) == 0)
    def _():
        y = _rms(z_ref[...], g_ref[...])
        for s in range(nsub):
            rows = slice(s * ROW_BLK, (s + 1) * ROW_BLK)
            shift, scale = mod_refs[2 * s][0], mod_refs[2 * s + 1][0]
            h_ref[rows, :] = (y[rows, :] * (1.0 + scale) + shift).astype(BF16)

    o_ref[...] = _dot(h_ref[...], w_ref[...]).astype(o_ref.dtype)


def _proj_in(z2, mods, g_pre, w, *, tiles_per_batch):
    n, d = z2.shape
    tm = 1024 if n % 1024 == 0 else ROW_BLK
    tn = PCOLS // 4
    nsub = tm // ROW_BLK
    mod_specs = []
    for s in range(nsub):
        for col in (0, 1):
            mod_specs.append(pl.BlockSpec(
                (1, 1, d), lambda i, j, s=s, col=col: (_seg_row(i * nsub + s, tiles_per_batch), 0, col)))
    return pl.pallas_call(
        functools.partial(_proj_kernel, nsub=nsub),
        out_shape=jax.ShapeDtypeStruct((n, PCOLS), ACT),
        grid=(n // tm, PCOLS // tn),
        in_specs=[pl.BlockSpec((tm, d), lambda i, j: (i, 0))] + mod_specs + [
            pl.BlockSpec((1, d), lambda i, j: (0, 0)),
            pl.BlockSpec((d, tn), lambda i, j: (0, j)),
        ],
        out_specs=pl.BlockSpec((tm, tn), lambda i, j: (i, j)),
        scratch_shapes=[pltpu.VMEM((tm, d), BF16)],
        compiler_params=_cparams(("parallel", "arbitrary")),
        name="proj_in",
    )(z2, *([mods] * (2 * nsub)), g_pre, w)


def _mla_prep_kernel(qb_ref, kb_ref, cos_ref, sin_ref, gq_ref, gkv_ref, wq_ref, wk_ref, wv_ref,
                     vone_ref, qt_ref, k_ref, vt_ref):
    nh = N_HEADS_MLA
    hw = nh * HEAD_PAD
    qb = qb_ref[0].astype(F32)
    kb = kb_ref[0].astype(F32)
    cos8 = jnp.tile(cos_ref[...], (1, nh))
    sin8 = jnp.tile(sin_ref[...], (1, nh))
    qn = _rms(qb[:, :Q_LORA], gq_ref[...]).astype(BF16)
    qq = _dot(qn, wq_ref[...])
    q = (qq[:, :hw] * cos8 + qq[:, hw:] * sin8) * (ATTN_SCALE * LOG2E)
    kvn = _rms(kb[:, :KV_LORA], gkv_ref[...]).astype(BF16)
    kin = jnp.concatenate([kvn, qb[:, Q_LORA:].astype(BF16)], axis=1)
    kk = _dot(kin, wk_ref[...])
    k_ref[0] = (kk[:, :hw] * cos8 + kk[:, hw:] * sin8).astype(BF16)
    v = _dot(kvn, wv_ref[...]) + vone_ref[...]
    for h in range(nh):
        lanes = slice(h * HEAD_PAD, (h + 1) * HEAD_PAD)
        qt_ref[0, h] = q[:, lanes].T.astype(BF16)
        vt_ref[0, h] = v[:, lanes].T[:VT_ROWS].astype(BF16)


def _mla_prep(p3, cos_t, sin_t, g_q, g_kv, wq, wk, wv, vone, *, tm):
    b, t, _ = p3.shape
    hw = N_HEADS_MLA * HEAD_PAD
    const = lambda shape: pl.BlockSpec(shape, lambda bi, i: (0,) * len(shape))
    return pl.pallas_call(
        _mla_prep_kernel,
        out_shape=(
            jax.ShapeDtypeStruct((b, N_HEADS_MLA, HEAD_PAD, t), BF16),
            jax.ShapeDtypeStruct((b, t, hw), BF16),
            jax.ShapeDtypeStruct((b, N_HEADS_MLA, VT_ROWS, t), BF16),
        ),
        grid=(b, t // tm),
        in_specs=[
            pl.BlockSpec((1, tm, 512), lambda bi, i: (bi, i, PB_Q)),
            pl.BlockSpec((1, tm, 512), lambda bi, i: (bi, i, PB_KV)),
            pl.BlockSpec((tm, HEAD_PAD), lambda bi, i: (i, 0)),
            pl.BlockSpec((tm, HEAD_PAD), lambda bi, i: (i, 0)),
            const((1, Q_LORA)),
            const((1, KV_LORA)),
            const(wq.shape),
            const(wk.shape),
            const(wv.shape),
            const((1, hw)),
        ],
        out_specs=(
            pl.BlockSpec((1, N_HEADS_MLA, HEAD_PAD, tm), lambda bi, i: (bi, 0, 0, i)),
            pl.BlockSpec((1, tm, hw), lambda bi, i: (bi, i, 0)),
            pl.BlockSpec((1, N_HEADS_MLA, VT_ROWS, tm), lambda bi, i: (bi, 0, 0, i)),
        ),
        compiler_params=_cparams(("parallel", "parallel")),
        name="mla_prep",
    )(p3, p3, cos_t, sin_t, g_q, g_kv, wq, wk, wv, vone)


def _attn_kernel(qt_ref, k_ref, vt_ref, *rest, nk):
    o_ref, m_ref, acc_ref = rest[-3:]
    j = pl.program_id(2)

    @pl.when(j == 0)
    def _():
        m_ref[...] = jnp.full(m_ref.shape, -jnp.inf, F32)
        acc_ref[...] = jnp.zeros(acc_ref.shape, F32)

    nh = N_HEADS_MLA
    mt = MXU_TILE
    nqb = qt_ref.shape[3] // mt

    def qk_tiles(h):
        out = {}
        k_h = k_ref[0, :, h * HEAD_PAD:(h + 1) * HEAD_PAD]

        def make(n):
            def run():
                out[n] = _dot(k_h, qt_ref[0, h, :, n * mt:(n + 1) * mt])
            return run
        return out, [make(n) for n in range(nqb)]

    def softmax(h, s):
        p, alpha = {}, {}
        for n in range(nqb):
            lanes = slice(n * mt, (n + 1) * mt)
            m_prev = m_ref[h, :, lanes]
            m_new = jnp.maximum(m_prev, jnp.max(s[n], axis=0, keepdims=True))
            alpha[n] = jnp.exp2(m_prev - m_new)[0:1, :]
            p[n] = jnp.exp2(s[n] - m_new[0:1, :]).astype(BF16)
            m_ref[h, :, lanes] = m_new
        return p, alpha

    def pv_tiles(h, p, alpha):
        vt_h = vt_ref[0, h]

        def make(n):
            def run():
                lanes = slice(n * mt, (n + 1) * mt)
                acc_ref[h, :, lanes] = alpha[n] * acc_ref[h, :, lanes] + _dot(vt_h, p[n])
            return run
        return [make(n) for n in range(nqb)]

    def run_interleaved(a_ops, c_ops):
        for i in range(max(len(a_ops), len(c_ops))):
            if i < len(a_ops):
                a_ops[i]()
            if i < len(c_ops):
                c_ops[i]()

    s, pa = {}, {}
    for h in range(min(2, nh)):
        s[h], ops = qk_tiles(h)
        run_interleaved(ops, [])
    pa[0] = softmax(0, s[0])
    for h in range(nh):
        if h + 1 < nh:
            pa[h + 1] = softmax(h + 1, s[h + 1])
        a_ops = []
        if h + 2 < nh:
            s[h + 2], a_ops = qk_tiles(h + 2)
        run_interleaved(a_ops, pv_tiles(h, *pa[h]))

    @pl.when(j == nk - 1)
    def _():
        outs = []
        for h in range(N_HEADS_MLA):
            a = acc_ref[h]
            outs.append(a[:V_HEAD] * (1.0 / a[V_HEAD:V_HEAD + 1]))
        o_ref[0] = jnp.concatenate(outs, axis=0).T.astype(o_ref.dtype)


def _attention(qt, k, vt, y_prev, *, tq, tk, nq, nk, q_off, k_off):
    b, t, hw = k.shape
    in_specs = [
        pl.BlockSpec((1, N_HEADS_MLA, HEAD_PAD, tq), lambda bi, i, j: (bi, 0, 0, i + q_off)),
        pl.BlockSpec((1, tk, hw), lambda bi, i, j: (bi, j + k_off, 0)),
        pl.BlockSpec((1, N_HEADS_MLA, VT_ROWS, tk), lambda bi, i, j: (bi, 0, 0, j + k_off)),
    ]
    args = [qt, k, vt]
    aliases = {}
    if y_prev is not None:
        in_specs.append(pl.BlockSpec(memory_space=pl.ANY))
        args.append(y_prev)
        aliases = {3: 0}
    return pl.pallas_call(
        functools.partial(_attn_kernel, nk=nk),
        out_shape=jax.ShapeDtypeStruct((b, t, W_MLA), ACT),
        grid=(b, nq, nk),
        in_specs=in_specs,
        out_specs=pl.BlockSpec((1, tq, W_MLA), lambda bi, i, j: (bi, i + q_off, 0)),
        scratch_shapes=[
            pltpu.VMEM((N_HEADS_MLA, 8, tq), F32),
            pltpu.VMEM((N_HEADS_MLA, VT_ROWS, tq), F32),
        ],
        input_output_aliases=aliases,
        compiler_params=_cparams(("parallel", "parallel", "arbitrary")),
        name="attn_ctx" if y_prev is not None else "attn_x",
    )(*args)


def _half_lane_masks():
    lane = lax.broadcasted_iota(jnp.int32, (CHUNK, LANES), 1)
    first = lane < RWKV_HEAD
    return first, jnp.where(first, 1.0, 0.0).astype(BF16), jnp.where(first, 0.0, 1.0).astype(BF16)


def _bd4(x, lo, hi):
    xb = x.astype(BF16)
    xl, xr = xb[:, :LANES], xb[:, LANES:]
    z = jnp.zeros((2 * CHUNK, LANES), BF16)
    c0 = jnp.concatenate([xl * lo, xl * hi, z], axis=0)
    c1 = jnp.concatenate([z, xr * lo, xr * hi], axis=0)
    return jnp.concatenate([c0, c1], axis=1)


def _diag_blocks(full, first):
    c = CHUNK
    left = jnp.where(first, full[0:c, :LANES], full[c:2 * c, :LANES])
    right = jnp.where(first, full[2 * c:3 * c, LANES:], full[3 * c:4 * c, LANES:])
    return jnp.concatenate([left, right], axis=1)


def _rwkv_prep_kernel(rkv_ref, lora_ref, w0_ref, a0_ref, wup_ref, aup_ref, kk_ref, ka_ref, rk_ref,
                      ones_ref, rh_ref, y0_ref, g_ref, h_ref, bonus_ref):
    fwd = pl.program_id(0) == 0
    c = CHUNK
    w = W_RWKV
    grp = PREP_GROUP
    ones4 = ones_ref[...]
    sgn = jnp.where(fwd, 1, -1)
    ti = lax.broadcasted_iota(jnp.int32, (grp, grp), 0)
    si = lax.broadcasted_iota(jnp.int32, (grp, grp), 1)
    same = jnp.where((ti // c) == (si // c), 1.0, 0.0)
    tri = jnp.where((si - ti) * sgn <= 0, same, 0.0).astype(BF16)
    first, lo, hi = _half_lane_masks()
    tq = lax.broadcasted_iota(jnp.int32, (c, QUAD), 0)
    sq = lax.broadcasted_iota(jnp.int32, (c, QUAD), 1) % c
    before = (sq - tq) * sgn < 0
    upto = (sq - tq) * sgn <= 0
    eye = jnp.where(sq == tq, 1.0, 0.0)

    def head_sum(x):
        xh, xl = _split2(x)
        parts = []
        for qd in range(w // QUAD):
            ls = slice(qd * QUAD, (qd + 1) * QUAD)
            parts.append(_dot(xh[:, ls], ones4) + _dot(xl[:, ls], ones4))
        return jnp.concatenate(parts, axis=1)

    def pm(x, y):
        return _dot(x.astype(BF16), _bd4(y, lo, hi))

    def prologue(g, out):
        gs = slice(g * grp, (g + 1) * grp)
        rkv = rkv_ref[gs, :].astype(F32)
        r, k, v = rkv[:, :w], rkv[:, w:2 * w], rkv[:, 2 * w:]
        lora = lora_ref[gs, :].astype(F32)
        zw = w0_ref[0] + _dot(jnp.tanh(lora[:, :W_LORA]).astype(BF16), wup_ref[0])
        za = a0_ref[0] + _dot(lora[:, W_LORA:].astype(BF16), aup_ref[0])
        yield
        ell = -math.exp(-0.5) * _sigmoid(zw)
        a = _sigmoid(za)
        kkr = k * kk_ref[...]
        k_d = k * (1.0 + (a - 1.0) * ka_ref[...])
        kk_ss = head_sum(kkr * kkr)
        rk_s = head_sum(r * k_d * rk_ref[...])
        yield
        kk = kkr * lax.rsqrt(kk_ss + L2_EPS)
        bonus_ref[0, gs, :] = (rk_s * v).astype(bonus_ref.dtype)
        ell_hi, ell_lo = _split2(ell)
        lc = _dot(tri, ell_hi) + _dot(tri, ell_lo)
        yield
        ltot = jnp.concatenate(
            [jnp.broadcast_to(jnp.where(fwd, lc[ch * c + c - 1:ch * c + c], lc[ch * c:ch * c + 1]), (c, w))
             for ch in range(grp // c)], axis=0)
        e_neg = jnp.exp(-lc)
        e_tail = jnp.exp(ltot - lc)
        kka = kk * a
        out.update(abar=kk * jnp.exp(lc - ell), bbar=kka * e_neg, kbar=k_d * e_neg, rbar=r * jnp.exp(lc),
                   btil=kka * e_tail, ktil=k_d * e_tail, v=v, gam=jnp.exp(ltot))

    def chains(g, q):
        sl = [(slice(ch * c, (ch + 1) * c), slice(qd * QUAD, (qd + 1) * QUAD))
              for ch in range(grp // c) for qd in range(w // QUAD)]
        abar, bbar, kbar, rbar, btil, ktil, v, gam = (
            q[nm] for nm in ("abar", "bbar", "kbar", "rbar", "btil", "ktil", "v", "gam"))
        la = [jnp.concatenate([abar[s], rbar[s]], axis=0).astype(BF16) for s in sl]
        nb = [_dot_nt(la_i, _bd4(bbar[s], lo, hi)) for la_i, s in zip(la, sl)]
        yield
        nk = [_dot_nt(la_i, _bd4(kbar[s], lo, hi)) for la_i, s in zip(la, sl)]
        yield
        n = [jnp.where(before, t[:c], 0.0) for t in nb]
        a_rb = [jnp.where(upto, t[c:], 0.0) for t in nb]
        a_ak = [jnp.where(before, t[:c], 0.0) for t in nk]
        a_rk = [jnp.where(upto, t[c:], 0.0) for t in nk]
        x = [eye - t for t in n]
        p = [pm(t, t) for t in n]
        yield
        py = [pm(jnp.concatenate([u, r_], axis=0), v[s]) for u, r_, s in zip(a_ak, a_rk, sl)]
        yield
        for it in range(5):
            if it < 4:
                xp = [pm(jnp.concatenate([x_i, p_i], axis=0), p_i) for x_i, p_i in zip(x, p)]
                x = [x_i + t[:c] for x_i, t in zip(x, xp)]
                p = [t[c:] for t in xp]
            else:
                x = [x_i + pm(x_i, p_i) for x_i, p_i in zip(x, p)]
            yield
        ah = [pm(x_i, abar[s]) for x_i, s in zip(x, sl)]
        u0 = [pm(x_i, t[:c]) for x_i, t in zip(x, py)]
        yield
        ra = [pm(m_i, t) for m_i, t in zip(a_rb, ah)]
        ru = [pm(m_i, t) for m_i, t in zip(a_rb, u0)]
        gfull = [_dot_tn(btil[s].astype(BF16), t.astype(BF16)) for s, t in zip(sl, ah)]
        hfull = [_dot_tn(jnp.concatenate([ktil[s], -btil[s]], axis=0).astype(BF16),
                         jnp.concatenate([v[s], t], axis=0).astype(BF16)) for s, t in zip(sl, u0)]
        for i, (rs, ls) in enumerate(sl):
            ro = slice(g * grp + rs.start, g * grp + rs.stop)
            rh_ref[0, ro, ls] = (rbar[rs, ls] - ra[i]).astype(BF16)
            y0_ref[0, ro, ls] = py[i][c:] - ru[i]
            g_ref[0, ro, ls] = (eye * gam[rs, ls] - _diag_blocks(gfull[i], first)).astype(g_ref.dtype)
            h_ref[0, ro, ls] = _diag_blocks(hfull[i], first)

    ngroups = rkv_ref.shape[0] // grp
    vals = [dict() for _ in range(ngroups)]
    for _ in prologue(0, vals[0]):
        pass
    for g in range(ngroups):
        nxt = prologue(g + 1, vals[g + 1]) if g + 1 < ngroups else iter(())
        for stage, _ in enumerate(chains(g, vals[g])):
            if stage >= 1 and stage % 2 == 1:
                next(nxt, None)
        for _ in nxt:
            pass


def _rwkv_prep(p2, w0, a0, w_up, a_up, k_k, k_a, r_k, ones4):
    n, _ = p2.shape
    w = W_RWKV
    rows = PREP_ROWS if n % PREP_ROWS == 0 else PREP_GROUP
    const = lambda shape: pl.BlockSpec(shape, lambda d, i: (0,) * len(shape))
    perdir = lambda shape: pl.BlockSpec((1,) + shape, lambda d, i: (d,) + (0,) * len(shape))
    out_spec = pl.BlockSpec((1, rows, w), lambda d, i: (d, i, 0))
    f32_out = jax.ShapeDtypeStruct((2, n, w), F32)
    act_out = jax.ShapeDtypeStruct((2, n, w), ACT)
    return pl.pallas_call(
        _rwkv_prep_kernel,
        out_shape=(act_out, f32_out, act_out, f32_out, act_out),
        grid=(2, n // rows),
        in_specs=[
            pl.BlockSpec((rows, 3 * w), lambda d, i: (i, PB_RKV)),
            pl.BlockSpec((rows, 2 * W_LORA), lambda d, i: (i, PB_LORA128 + d)),
            perdir((1, w)), perdir((1, w)), perdir((W_LORA, w)), perdir((A_LORA, w)),
            const((1, w)), const((1, w)), const((1, w)),
            const((QUAD, QUAD)),
        ],
        out_specs=(out_spec,) * 5,
        compiler_params=_cparams(("parallel", "parallel")),
        name="rwkv_prep",
    )(p2, p2, w0, a0, w_up, a_up, k_k, k_a, r_k, ones4)


def _rwkv_scan_kernel(rhf, y0f, gf, hf, rhb, y0b, gb, hb, yf_ref, yb_ref, st_ref, *, nb):
    @pl.when(pl.program_id(0) == 0)
    def _():
        st_ref[...] = jnp.zeros(st_ref.shape, F32)

    _, lo, hi = _half_lane_masks()
    c = CHUNK
    nsub = rhf.shape[2] // c
    for step in range(nsub):
        for d, sub, (rh, y0, g, hh, y_ref) in ((0, step, (rhf, y0f, gf, hf, yf_ref)),
                                               (1, nsub - 1 - step, (rhb, y0b, gb, hb, yb_ref))):
            rs = slice(sub * c, (sub + 1) * c)
            for bi in range(nb):
                for qd in range(W_RWKV // QUAD):
                    ls = slice(qd * QUAD, (qd + 1) * QUAD)
                    lhs = jnp.concatenate([rh[0, bi, rs, ls], g[0, bi, rs, ls]], axis=0)
                    res = _dot(lhs, _bd4(st_ref[d, bi, qd], lo, hi))
                    y_ref[bi, rs, ls] = (res[:c] + y0[0, bi, rs, ls]).astype(y_ref.dtype)
                    st_ref[d, bi, qd] = res[c:] + hh[0, bi, rs, ls]


def _rwkv_scan(rh, y0, g, h, *, n_x, n_ctx):
    _, b, t, w = rh.shape
    c = ROW_BLK
    n = n_x + n_ctx
    fidx = lambda j: jnp.where(j < n_ctx, n_x + j, j - n_ctx)
    bidx = lambda j: n - 1 - j
    fspec = pl.BlockSpec((1, b, c, w), lambda j: (0, 0, fidx(j), 0))
    bspec = pl.BlockSpec((1, b, c, w), lambda j: (1, 0, bidx(j), 0))
    y_shape = jax.ShapeDtypeStruct((b, t, w), ACT)
    return pl.pallas_call(
        functools.partial(_rwkv_scan_kernel, nb=b),
        out_shape=(y_shape, y_shape),
        grid=(n,),
        in_specs=[fspec] * 4 + [bspec] * 4,
        out_specs=(
            pl.BlockSpec((b, c, w), lambda j: (0, fidx(j), 0)),
            pl.BlockSpec((b, c, w), lambda j: (0, bidx(j), 0)),
        ),
        scratch_shapes=[pltpu.VMEM((2, b, w // QUAD, RWKV_HEAD, QUAD), F32)],
        compiler_params=_cparams(("arbitrary",)),
        name="rwkv_scan",
    )(rh, y0, g, h, rh, y0, g, h)


def _merge_kernel(z_ref, gate_x_ref, gate_c_ref, gl_ref, grw_ref, gml_ref, cin_ref, cb_ref, cc_ref, gcv_ref,
                  cin_p, cc_p, cin_n, cc_n, ymla_ref, yf_ref, yb_ref, bon_ref,
                  wbm_ref, wbc_ref, wbr_ref, wout_ref, gpost_ref, cw_ref, cbias_ref, gng_ref, gnb_ref,
                  avg_ref, o_ref, *, tiles_per_batch, seq, ctx_len):
    tm = z_ref.shape[0]
    d = D_MODEL
    pos = (pl.program_id(0) % tiles_per_batch) * tm + lax.broadcasted_iota(jnp.int32, (tm, 1), 0)
    has_prev = jnp.logical_and(pos != 0, pos != seq)
    has_next = jnp.logical_and(pos != seq - 1, pos != seq + ctx_len - 1)

    def f32(ref, idx=slice(None)):
        return ref[idx].astype(F32)

    u = f32(cc_ref) * f32(cin_ref)
    hl = HALO_ROWS - 1
    u_halo_p = f32(cc_p, slice(hl, hl + 1)) * f32(cin_p, slice(hl, hl + 1))
    u_halo_n = f32(cc_n, slice(0, 1)) * f32(cin_n, slice(0, 1))
    row = lax.broadcasted_iota(jnp.int32, u.shape, 0)
    u_prev = jnp.where(has_prev, jnp.where(row == 0, u_halo_p, pltpu.roll(u, 1, axis=0)), 0.0)
    u_next = jnp.where(has_next, jnp.where(row == tm - 1, u_halo_n, pltpu.roll(u, tm - 1, axis=0)), 0.0)
    cw = cw_ref[...]
    y_conv = f32(cb_ref) * (u_prev * cw[0:1] + u * cw[1:2] + u_next * cw[2:3] + cbias_ref[...])

    avg = avg_ref[...]

    def head_mean(x):
        hi, lo = _split2(x)
        parts = []
        for qd in range(W_RWKV // QUAD):
            ls = slice(qd * QUAD, (qd + 1) * QUAD)
            parts.append(_dot(hi[:, ls], avg) + _dot(lo[:, ls], avg))
        return jnp.concatenate(parts, axis=1)

    yr = f32(yf_ref) + f32(yb_ref)
    mu = head_mean(yr)
    dv = yr - mu
    var = head_mean(dv * dv)
    y_rwkv = dv * lax.rsqrt(var + GN_EPS) * gng_ref[...] + gnb_ref[...] + f32(bon_ref, 0) + f32(bon_ref, 1)

    br_mla = _dot((f32(ymla_ref) * _silu(f32(gml_ref))).astype(BF16), wbm_ref[...])
    br_conv = _dot((y_conv * _silu(f32(gcv_ref))).astype(BF16), wbc_ref[...])
    br_rwkv = _dot((y_rwkv * _silu(f32(grw_ref))).astype(BF16), wbr_ref[...])
    s = _sigmoid(f32(gl_ref))
    merged = s[:, :d] * br_mla + s[:, d:2 * d] * br_conv + s[:, 2 * d:] * br_rwkv
    o = _dot(merged.astype(BF16), wout_ref[...])
    gate = jnp.where(pos >= seq, gate_c_ref[0], gate_x_ref[0])
    o_ref[...] = z_ref[...] + gate * _rms(o, gpost_ref[...])


def _merge(z2, mods, p2, y_mla, yf, yb, bonus, wbm, wbc, wbr, wout, g_post, conv_w, conv_b,
           gn_g, gn_b, avg_bd, *, seq, ctx_len, tm, x_only):
    n, d = z2.shape
    t = seq + ctx_len
    assert t % tm == 0 and tm % HALO_ROWS == 0 and (not x_only or seq % tm == 0)
    hb = tm // HALO_ROWS
    nhb = n // HALO_ROWS
    tiles_per_batch = t // tm
    x_tiles = seq // tm
    n_tiles = (n // t) * x_tiles if x_only else n // tm
    gi = (lambda i: i // x_tiles * tiles_per_batch + i % x_tiles) if x_only else (lambda i: i)
    pcol = lambda blk: pl.BlockSpec((tm, 512), lambda i: (gi(i), blk))
    prev = lambda blk: pl.BlockSpec((HALO_ROWS, 512), lambda i: (jnp.maximum(gi(i) * hb - 1, 0), blk))
    nxt = lambda blk: pl.BlockSpec((HALO_ROWS, 512), lambda i: (jnp.minimum((gi(i) + 1) * hb, nhb - 1), blk))
    const = lambda shape: pl.BlockSpec(shape, lambda i: (0,) * len(shape))
    row512 = pl.BlockSpec((tm, 512), lambda i: (gi(i), 0))
    return pl.pallas_call(
        functools.partial(_merge_kernel, tiles_per_batch=x_tiles if x_only else tiles_per_batch,
                          seq=seq, ctx_len=ctx_len),
        out_shape=jax.ShapeDtypeStruct((n_tiles * tm, d), F32),
        grid=(n_tiles,),
        in_specs=[
            pl.BlockSpec((tm, d), lambda i: (gi(i), 0)),
            pl.BlockSpec((1, 1, d), lambda i: (1 + gi(i) // tiles_per_batch, 0, 2)),
            pl.BlockSpec((1, 1, d), lambda i: (0, 0, 2)),
            pl.BlockSpec((tm, 3 * d), lambda i: (gi(i), PB_GATE)),
            pcol(PB_GRWKV), pcol(PB_GMLA), pcol(PB_CVIN), pcol(PB_CVB), pcol(PB_CVC), pcol(PB_GCONV),
            prev(PB_CVIN), prev(PB_CVC), nxt(PB_CVIN), nxt(PB_CVC),
            row512, row512, row512,
            pl.BlockSpec((2, tm, 512), lambda i: (0, gi(i), 0)),
            const(wbm.shape), const(wbc.shape), const(wbr.shape), const(wout.shape),
            const((1, d)), const(conv_w.shape), const((1, 512)), const((1, 512)), const((1, 512)),
            const(avg_bd.shape),
        ],
        out_specs=pl.BlockSpec((tm, d), lambda i: (i, 0)),
        compiler_params=_cparams(("parallel",)),
        name="merge",
    )(z2, mods, mods, p2, p2, p2, p2, p2, p2, p2, p2, p2, p2, p2, y_mla, yf, yb, bonus,
      wbm, wbc, wbr, wout, g_post, conv_w, conv_b, gn_g, gn_b, avg_bd)


def _pair_swap(w):
    s = w.shape
    return w.reshape(s[:-1] + (s[-1] // 2, 2))[..., ::-1].reshape(s)


def _layout_w_in(w_in):
    sizes = (Q_LORA, KV_LORA, QK_ROPE, W_MLA, CONV_W, CONV_W, CONV_W, CONV_W, W_RWKV, W_RWKV, W_RWKV,
             W_LORA, W_LORA, A_LORA, A_LORA, W_RWKV, 3 * D_MODEL)
    offs = np.concatenate([[0], np.cumsum(sizes)])
    names = ("q_lat", "kv_lat", "kr", "g_mla", "cv_in", "cv_b", "cv_c", "g_conv", "r", "k", "v",
             "wd_f", "wd_b", "ad_f", "ad_b", "g_rwkv", "gl")
    col = {nm: w_in[..., offs[i]:offs[i + 1]] for i, nm in enumerate(names)}
    zeros = jnp.zeros(w_in.shape[:-1] + (512 - Q_LORA - 2 * QK_ROPE,), w_in.dtype)
    parts = [col["gl"], col["r"], col["k"], col["v"], col["g_rwkv"], col["g_mla"], col["cv_in"],
             col["cv_b"], col["cv_c"], col["g_conv"],
             col["q_lat"], col["kr"], _pair_swap(col["kr"]), zeros,
             col["kv_lat"], col["wd_f"], col["ad_f"], col["wd_b"], col["ad_b"]]
    out = jnp.concatenate(parts, axis=-1).astype(BF16)
    assert out.shape[-1] == PCOLS
    return out


def _layout_mla_weights(w_uq, w_ukv):
    depth = w_uq.shape[0]
    nh = N_HEADS_MLA
    wq = w_uq.reshape(depth, Q_LORA, nh, QK_HEAD)
    q_nope, q_rope = wq[..., :QK_NOPE], wq[..., QK_NOPE:]
    zq = jnp.zeros((depth, Q_LORA, nh, HEAD_PAD - QK_HEAD), w_uq.dtype)
    wq_a = jnp.concatenate([q_nope, q_rope, zq], axis=-1)
    wq_b = jnp.concatenate([jnp.zeros_like(q_nope), _pair_swap(q_rope), zq], axis=-1)
    wq_all = jnp.concatenate([wq_a.reshape(depth, Q_LORA, nh * HEAD_PAD),
                              wq_b.reshape(depth, Q_LORA, nh * HEAD_PAD)], axis=-1).astype(BF16)

    wkv = w_ukv.reshape(depth, KV_LORA, nh, QK_NOPE + V_HEAD)
    k_nope, v_w = wkv[..., :QK_NOPE], wkv[..., QK_NOPE:]
    zk = jnp.zeros((depth, KV_LORA, nh, HEAD_PAD - QK_NOPE), w_ukv.dtype)
    wk_top = jnp.concatenate([k_nope, zk], axis=-1).reshape(depth, KV_LORA, nh * HEAD_PAD)
    wv = jnp.concatenate([v_w, zk], axis=-1).reshape(depth, KV_LORA, nh * HEAD_PAD).astype(BF16)
    place = np.zeros((HEAD_PAD, HEAD_PAD), np.float32)
    place[np.arange(QK_ROPE), QK_NOPE + np.arange(QK_ROPE)] = 1.0
    e_a = np.tile(place, (1, nh))
    place_b = np.zeros((HEAD_PAD, HEAD_PAD), np.float32)
    place_b[QK_ROPE + np.arange(QK_ROPE), QK_NOPE + np.arange(QK_ROPE)] = 1.0
    e_b = np.tile(place_b, (1, nh))
    top = jnp.concatenate([wk_top, jnp.zeros_like(wk_top)], axis=-1)
    bot = jnp.broadcast_to(jnp.asarray(np.concatenate([e_a, e_b], axis=1)), (depth, HEAD_PAD, 2 * nh * HEAD_PAD))
    wk_all = jnp.concatenate([top, bot.astype(top.dtype)], axis=1).astype(BF16)
    return wq_all, wk_all, wv


def _rope_tables(seq, ctx_len):
    n_freq = QK_ROPE // 4
    pos = np.arange(seq)
    inv = ROPE_THETA ** (-np.arange(n_freq, dtype=np.float32) / n_freq)
    row = (pos // GRID_W).astype(np.float32)
    colp = (pos % GRID_W).astype(np.float32)
    ang = jnp.concatenate([jnp.asarray(row)[:, None] * jnp.asarray(inv), jnp.asarray(colp)[:, None] * jnp.asarray(inv)], axis=-1)
    cos, sin = jnp.cos(ang), jnp.sin(ang)
    cos2 = jnp.repeat(cos, 2, axis=-1)
    sin2 = jnp.stack([-sin, sin], axis=-1).reshape(seq, QK_ROPE)
    ones = jnp.ones((seq, QK_NOPE), F32)
    pad = jnp.zeros((seq, HEAD_PAD - QK_HEAD), F32)
    cos_x = jnp.concatenate([ones, cos2, pad], axis=-1)
    sin_x = jnp.concatenate([jnp.zeros_like(ones), sin2, pad], axis=-1)
    cos_c = jnp.concatenate([jnp.ones((ctx_len, QK_HEAD), F32), jnp.zeros((ctx_len, HEAD_PAD - QK_HEAD), F32)], axis=-1)
    sin_c = jnp.zeros((ctx_len, HEAD_PAD), F32)
    return jnp.concatenate([cos_x, cos_c], axis=0), jnp.concatenate([sin_x, sin_c], axis=0)


def _block_diag_const(n, blk, value):
    i = np.arange(n)
    return np.where((i[:, None] // blk) == (i[None, :] // blk), value, 0.0).astype(np.float32)


def _pick(n, candidates):
    for cand in candidates:
        if n % cand == 0:
            return cand
    raise ValueError(f"no tile for {n}")


def kernel(x, c, ctx, c_ctx, w_mod, b_mod, g_pre, g_post, w_in, g_q, g_kv, w_uq, w_ukv, conv_w, conv_b,
           w0, w_up, a0, a_up, k_k, k_a, r_k, gn_g, gn_b, w_br_mla, w_br_conv, w_br_rwkv, w_out):
    bsz, seq, d = x.shape
    ctx_len = ctx.shape[1]
    depth = w_mod.shape[0]
    assert d == D_MODEL and ctx_len == ROW_BLK and seq % ROW_BLK == 0
    t = seq + ctx_len
    n = bsz * t
    tiles_per_batch = t // ROW_BLK

    cc = jnp.zeros((8, d), F32).at[0].set(c_ctx).at[1:1 + bsz].set(c)
    mods = _adaln(cc, w_mod, b_mod).reshape(depth, 8, 1, 3 * d)

    w_in_p = _layout_w_in(w_in)
    wq_all, wk_all, wv_all = _layout_mla_weights(w_uq, w_ukv)
    cos_t, sin_t = _rope_tables(seq, ctx_len)
    vone = np.zeros((1, N_HEADS_MLA * HEAD_PAD), np.float32)
    vone[0, V_HEAD::HEAD_PAD] = 1.0
    vone = jnp.asarray(vone)
    ones4 = jnp.asarray(_block_diag_const(QUAD, RWKV_HEAD, 1.0), BF16)
    avg_bd = jnp.asarray(_block_diag_const(QUAD, RWKV_HEAD, 1.0 / RWKV_HEAD), BF16)

    row_tile = _pick(t, (768, 512, 256))
    tq = _pick(seq, (2048, 1024, 512, 256))
    tk = _pick(t, (768, 512, 256))

    z = jnp.concatenate([x, ctx], axis=1)
    for l in range(depth):
        z2 = z.reshape(n, d)
        p2 = _proj_in(z2, mods[l], g_pre[l][None], w_in_p[l], tiles_per_batch=tiles_per_batch)
        p3 = p2.reshape(bsz, t, PCOLS)

        qt, k, vt = _mla_prep(p3, cos_t, sin_t, g_q[l][None], g_kv[l][None], wq_all[l], wk_all[l], wv_all[l], vone,
                              tm=row_tile)
        y_mla = _attention(qt, k, vt, None, tq=tq, tk=tk, nq=seq // tq, nk=t // tk, q_off=0, k_off=0)
        y_mla = _attention(qt, k, vt, y_mla, tq=ctx_len, tk=ctx_len, nq=1, nk=1,
                           q_off=seq // ctx_len, k_off=seq // ctx_len)

        rh, y0, g, h, bonus = _rwkv_prep(
            p2, w0[l][:, None], a0[l][:, None], w_up[l].astype(BF16), a_up[l].astype(BF16),
            k_k[l][None], k_a[l][None], r_k[l].reshape(1, W_RWKV), ones4)
        per_batch = lambda arr: arr.reshape(2, bsz, t, W_RWKV)
        yf, yb = _rwkv_scan(per_batch(rh), per_batch(y0), per_batch(g), per_batch(h),
                            n_x=seq // ROW_BLK, n_ctx=ctx_len // ROW_BLK)

        z2 = _merge(z2, mods[l], p2, y_mla.reshape(n, W_MLA), yf.reshape(n, W_RWKV), yb.reshape(n, W_RWKV),
                    bonus,
                    w_br_mla[l].astype(BF16), w_br_conv[l].astype(BF16), w_br_rwkv[l].astype(BF16),
                    w_out[l].astype(BF16), g_post[l][None], conv_w[l], conv_b[l][None],
                    gn_g[l][None], gn_b[l][None], avg_bd,
                    seq=seq, ctx_len=ctx_len, x_only=l == depth - 1,
                    tm=ROW_BLK if l == depth - 1 else row_tile)
        if l < depth - 1:
            z = z2.reshape(bsz, t, d)
    return z2.reshape(bsz, seq, d)
```

```python
import functools
import math

import numpy as np
import jax
import jax.numpy as jnp
from jax import lax
from jax.experimental import pallas as pl
from jax.experimental.pallas import tpu as pltpu

F32 = jnp.float32
BF16 = jnp.bfloat16
ACT = BF16

D_MODEL = 1024
GRID_W = 64
N_HEADS_MLA = 8
Q_LORA = 384
KV_LORA = 256
QK_NOPE = 64
QK_ROPE = 32
QK_HEAD = QK_NOPE + QK_ROPE
V_HEAD = 64
W_MLA = N_HEADS_MLA * V_HEAD
ROPE_THETA = 10000.0
ATTN_SCALE = QK_HEAD ** -0.5
CONV_W = 512
RWKV_HEADS = 8
RWKV_HEAD = 64
W_RWKV = RWKV_HEADS * RWKV_HEAD
W_LORA = 64
A_LORA = 64
RMS_EPS = 1e-6
GN_EPS = 64e-5
L2_EPS = 1e-12
LOG2E = math.log2(math.e)

LANES = 128
ROW_BLK = 256
CHUNK = 64
QUAD = 4 * RWKV_HEAD
PREP_GROUP = 256
PREP_ROWS = 1024
HEAD_PAD = 128
MXU_TILE = 256
HALO_ROWS = 16
VT_ROWS = 80
VMEM_LIMIT = 48 * 1024 * 1024

PCOLS = 17 * 512
PB_GATE = 0
PB_RKV = 2
PB_GRWKV = 9
PB_GMLA = 10
PB_CVIN = 11
PB_CVB = 12
PB_CVC = 13
PB_GCONV = 14
PB_Q = 15
PB_KV = 16
PB_LORA128 = (16 * 512 + 256) // 128


def _cparams(sem, vmem=VMEM_LIMIT):
    return pltpu.CompilerParams(dimension_semantics=sem, vmem_limit_bytes=vmem)


def _dot(a, b):
    return jnp.dot(a, b, preferred_element_type=F32)


def _dot_nt(a, b):
    return lax.dot_general(a, b, (((1,), (1,)), ((), ())), preferred_element_type=F32)


def _dot_tn(a, b):
    return lax.dot_general(a, b, (((0,), (0,)), ((), ())), preferred_element_type=F32)


def _split2(x):
    hi = x.astype(BF16)
    lo = (x - hi.astype(F32)).astype(BF16)
    return hi, lo


def _dot_hi(a, b):
    ah, al = _split2(a)
    bh, bl = _split2(b)
    return _dot(ah, bh) + _dot(ah, bl) + _dot(al, bh)


def _sigmoid(x):
    return 1.0 / (1.0 + jnp.exp(-x))


def _silu(x):
    return x * _sigmoid(x)


def _rms(x, g):
    return x * lax.rsqrt(jnp.mean(x * x, axis=-1, keepdims=True) + RMS_EPS) * g


def _adaln_kernel(c_ref, w_ref, b_ref, o_ref):
    a = _silu(c_ref[...])
    o_ref[0] = _dot_hi(a, w_ref[0]) + b_ref[0]


def _adaln(cc, w_mod, b_mod):
    depth, d, d3 = w_mod.shape
    tn = 1024
    return pl.pallas_call(
        _adaln_kernel,
        out_shape=jax.ShapeDtypeStruct((depth, 8, d3), F32),
        grid=(depth, d3 // tn),
        in_specs=[
            pl.BlockSpec((8, d), lambda l, j: (0, 0)),
            pl.BlockSpec((1, d, tn), lambda l, j: (l, 0, j)),
            pl.BlockSpec((1, 1, tn), lambda l, j: (l, 0, j)),
        ],
        out_specs=pl.BlockSpec((1, 8, tn), lambda l, j: (l, 0, j)),
        compiler_params=_cparams(("parallel", "parallel")),
        name="adaln",
    )(cc, w_mod, b_mod.reshape(depth, 1, d3))


def _seg_row(blk, tiles_per_batch):
    return jnp.where(blk % tiles_per_batch == tiles_per_batch - 1, 0, 1 + blk // tiles_per_batch)


def _proj_kernel(z_ref, *refs, nsub):
    mod_refs = refs[:2 * nsub]
    g_ref, w_ref, o_ref, h_ref = refs[2 * nsub:]

    @pl.when(pl.program_id(1) == 0)
    def _():
        y = _rms(z_ref[...], g_ref[...])
        for s in range(nsub):
            rows = slice(s * ROW_BLK, (s + 1) * ROW_BLK)
            shift, scale = mod_refs[2 * s][0], mod_refs[2 * s + 1][0]
            h_ref[rows, :] = (y[rows, :] * (1.0 + scale) + shift).astype(BF16)

    o_ref[...] = _dot(h_ref[...], w_ref[...]).astype(o_ref.dtype)


def _proj_in(z2, mods, g_pre, w, *, tiles_per_batch):
    n, d = z2.shape
    tm = 1024 if n % 1024 == 0 else ROW_BLK
    tn = PCOLS // 4
    nsub = tm // ROW_BLK
    mod_specs = []
    for s in range(nsub):
        for col in (0, 1):
            mod_specs.append(pl.BlockSpec(
                (1, 1, d), lambda i, j, s=s, col=col: (_seg_row(i * nsub + s, tiles_per_batch), 0, col)))
    return pl.pallas_call(
        functools.partial(_proj_kernel, nsub=nsub),
        out_shape=jax.ShapeDtypeStruct((n, PCOLS), ACT),
        grid=(n // tm, PCOLS // tn),
        in_specs=[pl.BlockSpec((tm, d), lambda i, j: (i, 0))] + mod_specs + [
            pl.BlockSpec((1, d), lambda i, j: (0, 0)),
            pl.BlockSpec((d, tn), lambda i, j: (0, j)),
        ],
        out_specs=pl.BlockSpec((tm, tn), lambda i, j: (i, j)),
        scratch_shapes=[pltpu.VMEM((tm, d), BF16)],
        compiler_params=_cparams(("parallel", "arbitrary")),
        name="proj_in",
    )(z2, *([mods] * (2 * nsub)), g_pre, w)


def _mla_prep_kernel(qb_ref, kb_ref, cos_ref, sin_ref, gq_ref, gkv_ref, wq_ref, wk_ref, wv_ref,
                     vone_ref, qt_ref, k_ref, vt_ref):
    nh = N_HEADS_MLA
    hw = nh * HEAD_PAD
    qb = qb_ref[0].astype(F32)
    kb = kb_ref[0].astype(F32)
    cos8 = jnp.tile(cos_ref[...], (1, nh))
    sin8 = jnp.tile(sin_ref[...], (1, nh))
    qn = _rms(qb[:, :Q_LORA], gq_ref[...]).astype(BF16)
    qq = _dot(qn, wq_ref[...])
    q = (qq[:, :hw] * cos8 + qq[:, hw:] * sin8) * (ATTN_SCALE * LOG2E)
    kvn = _rms(kb[:, :KV_LORA], gkv_ref[...]).astype(BF16)
    kin = jnp.concatenate([kvn, qb[:, Q_LORA:].astype(BF16)], axis=1)
    kk = _dot(kin, wk_ref[...])
    k_ref[0] = (kk[:, :hw] * cos8 + kk[:, hw:] * sin8).astype(BF16)
    v = _dot(kvn, wv_ref[...]) + vone_ref[...]
    for h in range(nh):
        lanes = slice(h * HEAD_PAD, (h + 1) * HEAD_PAD)
        qt_ref[0, h] = q[:, lanes].T.astype(BF16)
        vt_ref[0, h] = v[:, lanes].T[:VT_ROWS].astype(BF16)


def _mla_prep(p3, cos_t, sin_t, g_q, g_kv, wq, wk, wv, vone, *, tm):
    b, t, _ = p3.shape
    hw = N_HEADS_MLA * HEAD_PAD
    const = lambda shape: pl.BlockSpec(shape, lambda bi, i: (0,) * len(shape))
    return pl.pallas_call(
        _mla_prep_kernel,
        out_shape=(
            jax.ShapeDtypeStruct((b, N_HEADS_MLA, HEAD_PAD, t), BF16),
            jax.ShapeDtypeStruct((b, t, hw), BF16),
            jax.ShapeDtypeStruct((b, N_HEADS_MLA, VT_ROWS, t), BF16),
        ),
        grid=(b, t // tm),
        in_specs=[
            pl.BlockSpec((1, tm, 512), lambda bi, i: (bi, i, PB_Q)),
            pl.BlockSpec((1, tm, 512), lambda bi, i: (bi, i, PB_KV)),
            pl.BlockSpec((tm, HEAD_PAD), lambda bi, i: (i, 0)),
            pl.BlockSpec((tm, HEAD_PAD), lambda bi, i: (i, 0)),
            const((1, Q_LORA)),
            const((1, KV_LORA)),
            const(wq.shape),
            const(wk.shape),
            const(wv.shape),
            const((1, hw)),
        ],
        out_specs=(
            pl.BlockSpec((1, N_HEADS_MLA, HEAD_PAD, tm), lambda bi, i: (bi, 0, 0, i)),
            pl.BlockSpec((1, tm, hw), lambda bi, i: (bi, i, 0)),
            pl.BlockSpec((1, N_HEADS_MLA, VT_ROWS, tm), lambda bi, i: (bi, 0, 0, i)),
        ),
        compiler_params=_cparams(("parallel", "parallel")),
        name="mla_prep",
    )(p3, p3, cos_t, sin_t, g_q, g_kv, wq, wk, wv, vone)


def _attn_kernel(qt_ref, k_ref, vt_ref, *rest, nk):
    o_ref, m_ref, acc_ref = rest[-3:]
    j = pl.program_id(2)

    @pl.when(j == 0)
    def _():
        m_ref[...] = jnp.full(m_ref.shape, -jnp.inf, F32)
        acc_ref[...] = jnp.zeros(acc_ref.shape, F32)

    nh = N_HEADS_MLA
    mt = MXU_TILE
    nqb = qt_ref.shape[3] // mt

    def qk_tiles(h):
        out = {}
        k_h = k_ref[0, :, h * HEAD_PAD:(h + 1) * HEAD_PAD]

        def make(n):
            def run():
                out[n] = _dot(k_h, qt_ref[0, h, :, n * mt:(n + 1) * mt])
            return run
        return out, [make(n) for n in range(nqb)]

    def softmax(h, s):
        p, alpha = {}, {}
        for n in range(nqb):
            lanes = slice(n * mt, (n + 1) * mt)
            m_prev = m_ref[h, :, lanes]
            m_new = jnp.maximum(m_prev, jnp.max(s[n], axis=0, keepdims=True))
            alpha[n] = jnp.exp2(m_prev - m_new)[0:1, :]
            p[n] = jnp.exp2(s[n] - m_new[0:1, :]).astype(BF16)
            m_ref[h, :, lanes] = m_new
        return p, alpha

    def pv_tiles(h, p, alpha):
        vt_h = vt_ref[0, h]

        def make(n):
            def run():
                lanes = slice(n * mt, (n + 1) * mt)
                acc_ref[h, :, lanes] = alpha[n] * acc_ref[h, :, lanes] + _dot(vt_h, p[n])
            return run
        return [make(n) for n in range(nqb)]

    def run_interleaved(a_ops, c_ops):
        for i in range(max(len(a_ops), len(c_ops))):
            if i < len(a_ops):
                a_ops[i]()
            if i < len(c_ops):
                c_ops[i]()

    s, pa = {}, {}
    for h in range(min(2, nh)):
        s[h], ops = qk_tiles(h)
        run_interleaved(ops, [])
    pa[0] = softmax(0, s[0])
    for h in range(nh):
        if h + 1 < nh:
            pa[h + 1] = softmax(h + 1, s[h + 1])
        a_ops = []
        if h + 2 < nh:
            s[h + 2], a_ops = qk_tiles(h + 2)
        run_interleaved(a_ops, pv_tiles(h, *pa[h]))

    @pl.when(j == nk - 1)
    def _():
        outs = []
        for h in range(N_HEADS_MLA):
            a = acc_ref[h]
            outs.append(a[:V_HEAD] * (1.0 / a[V_HEAD:V_HEAD + 1]))
        o_ref[0] = jnp.concatenate(outs, axis=0).T.astype(o_ref.dtype)


def _attention(qt, k, vt, y_prev, *, tq, tk, nq, nk, q_off, k_off):
    b, t, hw = k.shape
    in_specs = [
        pl.BlockSpec((1, N_HEADS_MLA, HEAD_PAD, tq), lambda bi, i, j: (bi, 0, 0, i + q_off)),
        pl.BlockSpec((1, tk, hw), lambda bi, i, j: (bi, j + k_off, 0)),
        pl.BlockSpec((1, N_HEADS_MLA, VT_ROWS, tk), lambda bi, i, j: (bi, 0, 0, j + k_off)),
    ]
    args = [qt, k, vt]
    aliases = {}
    if y_prev is not None:
        in_specs.append(pl.BlockSpec(memory_space=pl.ANY))
        args.append(y_prev)
        aliases = {3: 0}
    return pl.pallas_call(
        functools.partial(_attn_kernel, nk=nk),
        out_shape=jax.ShapeDtypeStruct((b, t, W_MLA), ACT),
        grid=(b, nq, nk),
        in_specs=in_specs,
        out_specs=pl.BlockSpec((1, tq, W_MLA), lambda bi, i, j: (bi, i + q_off, 0)),
        scratch_shapes=[
            pltpu.VMEM((N_HEADS_MLA, 8, tq), F32),
            pltpu.VMEM((N_HEADS_MLA, VT_ROWS, tq), F32),
        ],
        input_output_aliases=aliases,
        compiler_params=_cparams(("parallel", "parallel", "arbitrary")),
        name="attn_ctx" if y_prev is not None else "attn_x",
    )(*args)


def _half_lane_masks():
    lane = lax.broadcasted_iota(jnp.int32, (CHUNK, LANES), 1)
    first = lane < RWKV_HEAD
    return first, jnp.where(first, 1.0, 0.0).astype(BF16), jnp.where(first, 0.0, 1.0).astype(BF16)


def _bd4(x, lo, hi):
    xb = x.astype(BF16)
    xl, xr = xb[:, :LANES], xb[:, LANES:]
    z = jnp.zeros((2 * CHUNK, LANES), BF16)
    c0 = jnp.concatenate([xl * lo, xl * hi, z], axis=0)
    c1 = jnp.concatenate([z, xr * lo, xr * hi], axis=0)
    return jnp.concatenate([c0, c1], axis=1)


def _diag_blocks(full, first):
    c = CHUNK
    left = jnp.where(first, full[0:c, :LANES], full[c:2 * c, :LANES])
    right = jnp.where(first, full[2 * c:3 * c, LANES:], full[3 * c:4 * c, LANES:])
    return jnp.concatenate([left, right], axis=1)


def _rwkv_prep_kernel(rkv_ref, lora_ref, w0_ref, a0_ref, wup_ref, aup_ref, kk_ref, ka_ref, rk_ref,
                      ones_ref, rh_ref, y0_ref, g_ref, h_ref, bonus_ref):
    fwd = pl.program_id(0) == 0
    c = CHUNK
    w = W_RWKV
    grp = PREP_GROUP
    ones4 = ones_ref[...]
    sgn = jnp.where(fwd, 1, -1)
    ti = lax.broadcasted_iota(jnp.int32, (grp, grp), 0)
    si = lax.broadcasted_iota(jnp.int32, (grp, grp), 1)
    same = jnp.where((ti // c) == (si // c), 1.0, 0.0)
    tri = jnp.where((si - ti) * sgn <= 0, same, 0.0).astype(BF16)
    first, lo, hi = _half_lane_masks()
    tq = lax.broadcasted_iota(jnp.int32, (c, QUAD), 0)
    sq = lax.broadcasted_iota(jnp.int32, (c, QUAD), 1) % c
    before = (sq - tq) * sgn < 0
    upto = (sq - tq) * sgn <= 0
    eye = jnp.where(sq == tq, 1.0, 0.0)

    def head_sum(x):
        xh, xl = _split2(x)
        parts = []
        for qd in range(w // QUAD):
            ls = slice(qd * QUAD, (qd + 1) * QUAD)
            parts.append(_dot(xh[:, ls], ones4) + _dot(xl[:, ls], ones4))
        return jnp.concatenate(parts, axis=1)

    def pm(x, y):
        return _dot(x.astype(BF16), _bd4(y, lo, hi))

    def prologue(g, out):
        gs = slice(g * grp, (g + 1) * grp)
        rkv = rkv_ref[gs, :].astype(F32)
        r, k, v = rkv[:, :w], rkv[:, w:2 * w], rkv[:, 2 * w:]
        lora = lora_ref[gs, :].astype(F32)
        zw = w0_ref[0] + _dot(jnp.tanh(lora[:, :W_LORA]).astype(BF16), wup_ref[0])
        za = a0_ref[0] + _dot(lora[:, W_LORA:].astype(BF16), aup_ref[0])
        yield
        ell = -math.exp(-0.5) * _sigmoid(zw)
        a = _sigmoid(za)
        kkr = k * kk_ref[...]
        k_d = k * (1.0 + (a - 1.0) * ka_ref[...])
        kk_ss = head_sum(kkr * kkr)
        rk_s = head_sum(r * k_d * rk_ref[...])
        yield
        kk = kkr * lax.rsqrt(kk_ss + L2_EPS)
        bonus_ref[0, gs, :] = (rk_s * v).astype(bonus_ref.dtype)
        ell_hi, ell_lo = _split2(ell)
        lc = _dot(tri, ell_hi) + _dot(tri, ell_lo)
        yield
        ltot = jnp.concatenate(
            [jnp.broadcast_to(jnp.where(fwd, lc[ch * c + c - 1:ch * c + c], lc[ch * c:ch * c + 1]), (c, w))
             for ch in range(grp // c)], axis=0)
        e_neg = jnp.exp(-lc)
        e_tail = jnp.exp(ltot - lc)
        kka = kk * a
        out.update(abar=kk * jnp.exp(lc - ell), bbar=kka * e_neg, kbar=k_d * e_neg, rbar=r * jnp.exp(lc),
                   btil=kka * e_tail, ktil=k_d * e_tail, v=v, gam=jnp.exp(ltot))

    def chains(g, q):
        sl = [(slice(ch * c, (ch + 1) * c), slice(qd * QUAD, (qd + 1) * QUAD))
              for ch in range(grp // c) for qd in range(w // QUAD)]
        abar, bbar, kbar, rbar, btil, ktil, v, gam = (
            q[nm] for nm in ("abar", "bbar", "kbar", "rbar", "btil", "ktil", "v", "gam"))
        la = [jnp.concatenate([abar[s], rbar[s]], axis=0).astype(BF16) for s in sl]
        nb = [_dot_nt(la_i, _bd4(bbar[s], lo, hi)) for la_i, s in zip(la, sl)]
        yield
        nk = [_dot_nt(la_i, _bd4(kbar[s], lo, hi)) for la_i, s in zip(la, sl)]
        yield
        n = [jnp.where(before, t[:c], 0.0) for t in nb]
        a_rb = [jnp.where(upto, t[c:], 0.0) for t in nb]
        a_ak = [jnp.where(before, t[:c], 0.0) for t in nk]
        a_rk = [jnp.where(upto, t[c:], 0.0) for t in nk]
        x = [eye - t for t in n]
        p = [pm(t, t) for t in n]
        yield
        py = [pm(jnp.concatenate([u, r_], axis=0), v[s]) for u, r_, s in zip(a_ak, a_rk, sl)]
        yield
        for it in range(5):
            if it < 4:
                xp = [pm(jnp.concatenate([x_i, p_i], axis=0), p_i) for x_i, p_i in zip(x, p)]
                x = [x_i + t[:c] for x_i, t in zip(x, xp)]
                p = [t[c:] for t in xp]
            else:
                x = [x_i + pm(x_i, p_i) for x_i, p_i in zip(x, p)]
            yield
        ah = [pm(x_i, abar[s]) for x_i, s in zip(x, sl)]
        u0 = [pm(x_i, t[:c]) for x_i, t in zip(x, py)]
        yield
        ra = [pm(m_i, t) for m_i, t in zip(a_rb, ah)]
        ru = [pm(m_i, t) for m_i, t in zip(a_rb, u0)]
        gfull = [_dot_tn(btil[s].astype(BF16), t.astype(BF16)) for s, t in zip(sl, ah)]
        hfull = [_dot_tn(jnp.concatenate([ktil[s], -btil[s]], axis=0).astype(BF16),
                         jnp.concatenate([v[s], t], axis=0).astype(BF16)) for s, t in zip(sl, u0)]
        for i, (rs, ls) in enumerate(sl):
            ro = slice(g * grp + rs.start, g * grp + rs.stop)
            rh_ref[0, ro, ls] = (rbar[rs, ls] - ra[i]).astype(BF16)
            y0_ref[0, ro, ls] = py[i][c:] - ru[i]
            g_ref[0, ro, ls] = (eye * gam[rs, ls] - _diag_blocks(gfull[i], first)).astype(g_ref.dtype)
            h_ref[0, ro, ls] = _diag_blocks(hfull[i], first)

    ngroups = rkv_ref.shape[0] // grp
    vals = [dict() for _ in range(ngroups)]
    for _ in prologue(0, vals[0]):
        pass
    for g in range(ngroups):
        nxt = prologue(g + 1, vals[g + 1]) if g + 1 < ngroups else iter(())
        for stage, _ in enumerate(chains(g, vals[g])):
            if stage >= 1 and stage % 2 == 1:
                next(nxt, None)
        for _ in nxt:
            pass


def _rwkv_prep(p2, w0, a0, w_up, a_up, k_k, k_a, r_k, ones4):
    n, _ = p2.shape
    w = W_RWKV
    rows = PREP_ROWS if n % PREP_ROWS == 0 else PREP_GROUP
    const = lambda shape: pl.BlockSpec(shape, lambda d, i: (0,) * len(shape))
    perdir = lambda shape: pl.BlockSpec((1,) + shape, lambda d, i: (d,) + (0,) * len(shape))
    out_spec = pl.BlockSpec((1, rows, w), lambda d, i: (d, i, 0))
    f32_out = jax.ShapeDtypeStruct((2, n, w), F32)
    act_out = jax.ShapeDtypeStruct((2, n, w), ACT)
    return pl.pallas_call(
        _rwkv_prep_kernel,
        out_shape=(act_out, f32_out, act_out, f32_out, act_out),
        grid=(2, n // rows),
        in_specs=[
            pl.BlockSpec((rows, 3 * w), lambda d, i: (i, PB_RKV)),
            pl.BlockSpec((rows, 2 * W_LORA), lambda d, i: (i, PB_LORA128 + d)),
            perdir((1, w)), perdir((1, w)), perdir((W_LORA, w)), perdir((A_LORA, w)),
            const((1, w)), const((1, w)), const((1, w)),
            const((QUAD, QUAD)),
        ],
        out_specs=(out_spec,) * 5,
        compiler_params=_cparams(("parallel", "parallel")),
        name="rwkv_prep",
    )(p2, p2, w0, a0, w_up, a_up, k_k, k_a, r_k, ones4)


def _rwkv_scan_kernel(rhf, y0f, gf, hf, rhb, y0b, gb, hb, yf_ref, yb_ref, st_ref, *, nb):
    @pl.when(pl.program_id(0) == 0)
    def _():
        st_ref[...] = jnp.zeros(st_ref.shape, F32)

    _, lo, hi = _half_lane_masks()
    c = CHUNK
    nsub = rhf.shape[2] // c
    for step in range(nsub):
        for d, sub, (rh, y0, g, hh, y_ref) in ((0, step, (rhf, y0f, gf, hf, yf_ref)),
                                               (1, nsub - 1 - step, (rhb, y0b, gb, hb, yb_ref))):
            rs = slice(sub * c, (sub + 1) * c)
            for bi in range(nb):
                for qd in range(W_RWKV // QUAD):
                    ls = slice(qd * QUAD, (qd + 1) * QUAD)
                    lhs = jnp.concatenate([rh[0, bi, rs, ls], g[0, bi, rs, ls]], axis=0)
                    res = _dot(lhs, _bd4(st_ref[d, bi, qd], lo, hi))
                    y_ref[bi, rs, ls] = (res[:c] + y0[0, bi, rs, ls]).astype(y_ref.dtype)
                    st_ref[d, bi, qd] = res[c:] + hh[0, bi, rs, ls]


def _rwkv_scan(rh, y0, g, h, *, n_x, n_ctx):
    _, b, t, w = rh.shape
    c = ROW_BLK
    n = n_x + n_ctx
    fidx = lambda j: jnp.where(j < n_ctx, n_x + j, j - n_ctx)
    bidx = lambda j: n - 1 - j
    fspec = pl.BlockSpec((1, b, c, w), lambda j: (0, 0, fidx(j), 0))
    bspec = pl.BlockSpec((1, b, c, w), lambda j: (1, 0, bidx(j), 0))
    y_shape = jax.ShapeDtypeStruct((b, t, w), ACT)
    return pl.pallas_call(
        functools.partial(_rwkv_scan_kernel, nb=b),
        out_shape=(y_shape, y_shape),
        grid=(n,),
        in_specs=[fspec] * 4 + [bspec] * 4,
        out_specs=(
            pl.BlockSpec((b, c, w), lambda j: (0, fidx(j), 0)),
            pl.BlockSpec((b, c, w), lambda j: (0, bidx(j), 0)),
        ),
        scratch_shapes=[pltpu.VMEM((2, b, w // QUAD, RWKV_HEAD, QUAD), F32)],
        compiler_params=_cparams(("arbitrary",)),
        name="rwkv_scan",
    )(rh, y0, g, h, rh, y0, g, h)


def _merge_kernel(z_ref, gate_x_ref, gate_c_ref, gl_ref, grw_ref, gml_ref, cin_ref, cb_ref, cc_ref, gcv_ref,
                  cin_p, cc_p, cin_n, cc_n, ymla_ref, yf_ref, yb_ref, bon_ref,
                  wbm_ref, wbc_ref, wbr_ref, wout_ref, gpost_ref, cw_ref, cbias_ref, gng_ref, gnb_ref,
                  avg_ref, o_ref, *, tiles_per_batch, seq, ctx_len):
    tm = z_ref.shape[0]
    d = D_MODEL
    pos = (pl.program_id(0) % tiles_per_batch) * tm + lax.broadcasted_iota(jnp.int32, (tm, 1), 0)
    has_prev = jnp.logical_and(pos != 0, pos != seq)
    has_next = jnp.logical_and(pos != seq - 1, pos != seq + ctx_len - 1)

    def f32(ref, idx=slice(None)):
        return ref[idx].astype(F32)

    u = f32(cc_ref) * f32(cin_ref)
    hl = HALO_ROWS - 1
    u_halo_p = f32(cc_p, slice(hl, hl + 1)) * f32(cin_p, slice(hl, hl + 1))
    u_halo_n = f32(cc_n, slice(0, 1)) * f32(cin_n, slice(0, 1))
    row = lax.broadcasted_iota(jnp.int32, u.shape, 0)
    u_prev = jnp.where(has_prev, jnp.where(row == 0, u_halo_p, pltpu.roll(u, 1, axis=0)), 0.0)
    u_next = jnp.where(has_next, jnp.where(row == tm - 1, u_halo_n, pltpu.roll(u, tm - 1, axis=0)), 0.0)
    cw = cw_ref[...]
    y_conv = f32(cb_ref) * (u_prev * cw[0:1] + u * cw[1:2] + u_next * cw[2:3] + cbias_ref[...])

    avg = avg_ref[...]

    def head_mean(x):
        hi, lo = _split2(x)
        parts = []
        for qd in range(W_RWKV // QUAD):
            ls = slice(qd * QUAD, (qd + 1) * QUAD)
            parts.append(_dot(hi[:, ls], avg) + _dot(lo[:, ls], avg))
        return jnp.concatenate(parts, axis=1)

    yr = f32(yf_ref) + f32(yb_ref)
    mu = head_mean(yr)
    dv = yr - mu
    var = head_mean(dv * dv)
    y_rwkv = dv * lax.rsqrt(var + GN_EPS) * gng_ref[...] + gnb_ref[...] + f32(bon_ref, 0) + f32(bon_ref, 1)

    br_mla = _dot((f32(ymla_ref) * _silu(f32(gml_ref))).astype(BF16), wbm_ref[...])
    br_conv = _dot((y_conv * _silu(f32(gcv_ref))).astype(BF16), wbc_ref[...])
    br_rwkv = _dot((y_rwkv * _silu(f32(grw_ref))).astype(BF16), wbr_ref[...])
    s = _sigmoid(f32(gl_ref))
    merged = s[:, :d] * br_mla + s[:, d:2 * d] * br_conv + s[:, 2 * d:] * br_rwkv
    o = _dot(merged.astype(BF16), wout_ref[...])
    gate = jnp.where(pos >= seq, gate_c_ref[0], gate_x_ref[0])
    o_ref[...] = z_ref[...] + gate * _rms(o, gpost_ref[...])


def _merge(z2, mods, p2, y_mla, yf, yb, bonus, wbm, wbc, wbr, wout, g_post, conv_w, conv_b,
           gn_g, gn_b, avg_bd, *, seq, ctx_len, tm, x_only):
    n, d = z2.shape
    t = seq + ctx_len
    assert t % tm == 0 and tm % HALO_ROWS == 0 and (not x_only or seq % tm == 0)
    hb = tm // HALO_ROWS
    nhb = n // HALO_ROWS
    tiles_per_batch = t // tm
    x_tiles = seq // tm
    n_tiles = (n // t) * x_tiles if x_only else n // tm
    gi = (lambda i: i // x_tiles * tiles_per_batch + i % x_tiles) if x_only else (lambda i: i)
    pcol = lambda blk: pl.BlockSpec((tm, 512), lambda i: (gi(i), blk))
    prev = lambda blk: pl.BlockSpec((HALO_ROWS, 512), lambda i: (jnp.maximum(gi(i) * hb - 1, 0), blk))
    nxt = lambda blk: pl.BlockSpec((HALO_ROWS, 512), lambda i: (jnp.minimum((gi(i) + 1) * hb, nhb - 1), blk))
    const = lambda shape: pl.BlockSpec(shape, lambda i: (0,) * len(shape))
    row512 = pl.BlockSpec((tm, 512), lambda i: (gi(i), 0))
    return pl.pallas_call(
        functools.partial(_merge_kernel, tiles_per_batch=x_tiles if x_only else tiles_per_batch,
                          seq=seq, ctx_len=ctx_len),
        out_shape=jax.ShapeDtypeStruct((n_tiles * tm, d), F32),
        grid=(n_tiles,),
        in_specs=[
            pl.BlockSpec((tm, d), lambda i: (gi(i), 0)),
            pl.BlockSpec((1, 1, d), lambda i: (1 + gi(i) // tiles_per_batch, 0, 2)),
            pl.BlockSpec((1, 1, d), lambda i: (0, 0, 2)),
            pl.BlockSpec((tm, 3 * d), lambda i: (gi(i), PB_GATE)),
            pcol(PB_GRWKV), pcol(PB_GMLA), pcol(PB_CVIN), pcol(PB_CVB), pcol(PB_CVC), pcol(PB_GCONV),
            prev(PB_CVIN), prev(PB_CVC), nxt(PB_CVIN), nxt(PB_CVC),
            row512, row512, row512,
            pl.BlockSpec((2, tm, 512), lambda i: (0, gi(i), 0)),
            const(wbm.shape), const(wbc.shape), const(wbr.shape), const(wout.shape),
            const((1, d)), const(conv_w.shape), const((1, 512)), const((1, 512)), const((1, 512)),
            const(avg_bd.shape),
        ],
        out_specs=pl.BlockSpec((tm, d), lambda i: (i, 0)),
        compiler_params=_cparams(("parallel",)),
        name="merge",
    )(z2, mods, mods, p2, p2, p2, p2, p2, p2, p2, p2, p2, p2, p2, y_mla, yf, yb, bonus,
      wbm, wbc, wbr, wout, g_post, conv_w, conv_b, gn_g, gn_b, avg_bd)


def _pair_swap(w):
    s = w.shape
    return w.reshape(s[:-1] + (s[-1] // 2, 2))[..., ::-1].reshape(s)


def _layout_w_in(w_in):
    sizes = (Q_LORA, KV_LORA, QK_ROPE, W_MLA, CONV_W, CONV_W, CONV_W, CONV_W, W_RWKV, W_RWKV, W_RWKV,
             W_LORA, W_LORA, A_LORA, A_LORA, W_RWKV, 3 * D_MODEL)
    offs = np.concatenate([[0], np.cumsum(sizes)])
    names = ("q_lat", "kv_lat", "kr", "g_mla", "cv_in", "cv_b", "cv_c", "g_conv", "r", "k", "v",
             "wd_f", "wd_b", "ad_f", "ad_b", "g_rwkv", "gl")
    col = {nm: w_in[..., offs[i]:offs[i + 1]] for i, nm in enumerate(names)}
    zeros = jnp.zeros(w_in.shape[:-1] + (512 - Q_LORA - 2 * QK_ROPE,), w_in.dtype)
    parts = [col["gl"], col["r"], col["k"], col["v"], col["g_rwkv"], col["g_mla"], col["cv_in"],
             col["cv_b"], col["cv_c"], col["g_conv"],
             col["q_lat"], col["kr"], _pair_swap(col["kr"]), zeros,
             col["kv_lat"], col["wd_f"], col["ad_f"], col["wd_b"], col["ad_b"]]
    out = jnp.concatenate(parts, axis=-1).astype(BF16)
    assert out.shape[-1] == PCOLS
    return out


def _layout_mla_weights(w_uq, w_ukv):
    depth = w_uq.shape[0]
    nh = N_HEADS_MLA
    wq = w_uq.reshape(depth, Q_LORA, nh, QK_HEAD)
    q_nope, q_rope = wq[..., :QK_NOPE], wq[..., QK_NOPE:]
    zq = jnp.zeros((depth, Q_LORA, nh, HEAD_PAD - QK_HEAD), w_uq.dtype)
    wq_a = jnp.concatenate([q_nope, q_rope, zq], axis=-1)
    wq_b = jnp.concatenate([jnp.zeros_like(q_nope), _pair_swap(q_rope), zq], axis=-1)
    wq_all = jnp.concatenate([wq_a.reshape(depth, Q_LORA, nh * HEAD_PAD),
                              wq_b.reshape(depth, Q_LORA, nh * HEAD_PAD)], axis=-1).astype(BF16)

    wkv = w_ukv.reshape(depth, KV_LORA, nh, QK_NOPE + V_HEAD)
    k_nope, v_w = wkv[..., :QK_NOPE], wkv[..., QK_NOPE:]
    zk = jnp.zeros((depth, KV_LORA, nh, HEAD_PAD - QK_NOPE), w_ukv.dtype)
    wk_top = jnp.concatenate([k_nope, zk], axis=-1).reshape(depth, KV_LORA, nh * HEAD_PAD)
    wv = jnp.concatenate([v_w, zk], axis=-1).reshape(depth, KV_LORA, nh * HEAD_PAD).astype(BF16)
    place = np.zeros((HEAD_PAD, HEAD_PAD), np.float32)
    place[np.arange(QK_ROPE), QK_NOPE + np.arange(QK_ROPE)] = 1.0
    e_a = np.tile(place, (1, nh))
    place_b = np.zeros((HEAD_PAD, HEAD_PAD), np.float32)
    place_b[QK_ROPE + np.arange(QK_ROPE), QK_NOPE + np.arange(QK_ROPE)] = 1.0
    e_b = np.tile(place_b, (1, nh))
    top = jnp.concatenate([wk_top, jnp.zeros_like(wk_top)], axis=-1)
    bot = jnp.broadcast_to(jnp.asarray(np.concatenate([e_a, e_b], axis=1)), (depth, HEAD_PAD, 2 * nh * HEAD_PAD))
    wk_all = jnp.concatenate([top, bot.astype(top.dtype)], axis=1).astype(BF16)
    return wq_all, wk_all, wv


def _rope_tables(seq, ctx_len):
    n_freq = QK_ROPE // 4
    pos = np.arange(seq)
    inv = ROPE_THETA ** (-np.arange(n_freq, dtype=np.float32) / n_freq)
    row = (pos // GRID_W).astype(np.float32)
    colp = (pos % GRID_W).astype(np.float32)
    ang = jnp.concatenate([jnp.asarray(row)[:, None] * jnp.asarray(inv), jnp.asarray(colp)[:, None] * jnp.asarray(inv)], axis=-1)
    cos, sin = jnp.cos(ang), jnp.sin(ang)
    cos2 = jnp.repeat(cos, 2, axis=-1)
    sin2 = jnp.stack([-sin, sin], axis=-1).reshape(seq, QK_ROPE)
    ones = jnp.ones((seq, QK_NOPE), F32)
    pad = jnp.zeros((seq, HEAD_PAD - QK_HEAD), F32)
    cos_x = jnp.concatenate([ones, cos2, pad], axis=-1)
    sin_x = jnp.concatenate([jnp.zeros_like(ones), sin2, pad], axis=-1)
    cos_c = jnp.concatenate([jnp.ones((ctx_len, QK_HEAD), F32), jnp.zeros((ctx_len, HEAD_PAD - QK_HEAD), F32)], axis=-1)
    sin_c = jnp.zeros((ctx_len, HEAD_PAD), F32)
    return jnp.concatenate([cos_x, cos_c], axis=0), jnp.concatenate([sin_x, sin_c], axis=0)


def _block_diag_const(n, blk, value):
    i = np.arange(n)
    return np.where((i[:, None] // blk) == (i[None, :] // blk), value, 0.0).astype(np.float32)


def _pick(n, candidates):
    for cand in candidates:
        if n % cand == 0:
            return cand
    raise ValueError(f"no tile for {n}")


def kernel(x, c, ctx, c_ctx, w_mod, b_mod, g_pre, g_post, w_in, g_q, g_kv, w_uq, w_ukv, conv_w, conv_b,
           w0, w_up, a0, a_up, k_k, k_a, r_k, gn_g, gn_b, w_br_mla, w_br_conv, w_br_rwkv, w_out):
    bsz, seq, d = x.shape
    ctx_len = ctx.shape[1]
    depth = w_mod.shape[0]
    assert d == D_MODEL and ctx_len == ROW_BLK and seq % ROW_BLK == 0
    t = seq + ctx_len
    n = bsz * t
    tiles_per_batch = t // ROW_BLK

    cc = jnp.zeros((8, d), F32).at[0].set(c_ctx).at[1:1 + bsz].set(c)
    mods = _adaln(cc, w_mod, b_mod).reshape(depth, 8, 1, 3 * d)

    w_in_p = _layout_w_in(w_in)
    wq_all, wk_all, wv_all = _layout_mla_weights(w_uq, w_ukv)
    cos_t, sin_t = _rope_tables(seq, ctx_len)
    vone = np.zeros((1, N_HEADS_MLA * HEAD_PAD), np.float32)
    vone[0, V_HEAD::HEAD_PAD] = 1.0
    vone = jnp.asarray(vone)
    ones4 = jnp.asarray(_block_diag_const(QUAD, RWKV_HEAD, 1.0), BF16)
    avg_bd = jnp.asarray(_block_diag_const(QUAD, RWKV_HEAD, 1.0 / RWKV_HEAD), BF16)

    row_tile = _pick(t, (768, 512, 256))
    tq = _pick(seq, (2048, 1024, 512, 256))
    tk = _pick(t, (768, 512, 256))

    z = jnp.concatenate([x, ctx], axis=1)
    for l in range(depth):
        z2 = z.reshape(n, d)
        p2 = _proj_in(z2, mods[l], g_pre[l][None], w_in_p[l], tiles_per_batch=tiles_per_batch)
        p3 = p2.reshape(bsz, t, PCOLS)

        qt, k, vt = _mla_prep(p3, cos_t, sin_t, g_q[l][None], g_kv[l][None], wq_all[l], wk_all[l], wv_all[l], vone,
                              tm=row_tile)
        y_mla = _attention(qt, k, vt, None, tq=tq, tk=tk, nq=seq // tq, nk=t // tk, q_off=0, k_off=0)
        y_mla = _attention(qt, k, vt, y_mla, tq=ctx_len, tk=ctx_len, nq=1, nk=1,
                           q_off=seq // ctx_len, k_off=seq // ctx_len)

        rh, y0, g, h, bonus = _rwkv_prep(
            p2, w0[l][:, None], a0[l][:, None], w_up[l].astype(BF16), a_up[l].astype(BF16),
            k_k[l][None], k_a[l][None], r_k[l].reshape(1, W_RWKV), ones4)
        per_batch = lambda arr: arr.reshape(2, bsz, t, W_RWKV)
        yf, yb = _rwkv_scan(per_batch(rh), per_batch(y0), per_batch(g), per_batch(h),
                            n_x=seq // ROW_BLK, n_ctx=ctx_len // ROW_BLK)

        z2 = _merge(z2, mods[l], p2, y_mla.reshape(n, W_MLA), yf.reshape(n, W_RWKV), yb.reshape(n, W_RWKV),
                    bonus,
                    w_br_mla[l].astype(BF16), w_br_conv[l].astype(BF16), w_br_rwkv[l].astype(BF16),
                    w_out[l].astype(BF16), g_post[l][None], conv_w[l], conv_b[l][None],
                    gn_g[l][None], gn_b[l][None], avg_bd,
                    seq=seq, ctx_len=ctx_len, x_only=l == depth - 1,
                    tm=ROW_BLK if l == depth - 1 else row_tile)
        if l < depth - 1:
            z = z2.reshape(bsz, t, d)
    return z2.reshape(bsz, seq, d)
```

```python
import functools
import math

import numpy as np
import jax
import jax.numpy as jnp
from jax import lax
from jax.experimental import pallas as pl
from jax.experimental.pallas import tpu as pltpu

F32 = jnp.float32
BF16 = jnp.bfloat16
ACT = BF16

D_MODEL = 1024
GRID_W = 64
N_HEADS_MLA = 8
Q_LORA = 384
KV_LORA = 256
QK_NOPE = 64
QK_ROPE = 32
QK_HEAD = QK_NOPE + QK_ROPE
V_HEAD = 64
W_MLA = N_HEADS_MLA * V_HEAD
ROPE_THETA = 10000.0
ATTN_SCALE = QK_HEAD ** -0.5
CONV_W = 512
RWKV_HEADS = 8
RWKV_HEAD = 64
W_RWKV = RWKV_HEADS * RWKV_HEAD
W_LORA = 64
A_LORA = 64
RMS_EPS = 1e-6
GN_EPS = 64e-5
L2_EPS = 1e-12
LOG2E = math.log2(math.e)

LANES = 128
ROW_BLK = 256
CHUNK = 64
QUAD = 4 * RWKV_HEAD
PREP_GROUP = 256
PREP_ROWS = 1024
HEAD_PAD = 128
MXU_TILE = 256
HALO_ROWS = 16
VT_ROWS = 80
VMEM_LIMIT = 48 * 1024 * 1024

PCOLS = 17 * 512
PB_GATE = 0
PB_RKV = 2
PB_GRWKV = 9
PB_GMLA = 10
PB_CVIN = 11
PB_CVB = 12
PB_CVC = 13
PB_GCONV = 14
PB_Q = 15
PB_KV = 16
PB_LORA128 = (16 * 512 + 256) // 128


def _cparams(sem, vmem=VMEM_LIMIT):
    return pltpu.CompilerParams(dimension_semantics=sem, vmem_limit_bytes=vmem)


def _dot(a, b):
    return jnp.dot(a, b, preferred_element_type=F32)


def _dot_nt(a, b):
    return lax.dot_general(a, b, (((1,), (1,)), ((), ())), preferred_element_type=F32)


def _dot_tn(a, b):
    return lax.dot_general(a, b, (((0,), (0,)), ((), ())), preferred_element_type=F32)


def _split2(x):
    hi = x.astype(BF16)
    lo = (x - hi.astype(F32)).astype(BF16)
    return hi, lo


def _dot_hi(a, b):
    ah, al = _split2(a)
    bh, bl = _split2(b)
    return _dot(ah, bh) + _dot(ah, bl) + _dot(al, bh)


def _sigmoid(x):
    return 1.0 / (1.0 + jnp.exp(-x))


def _silu(x):
    return x * _sigmoid(x)


def _rms(x, g):
    return x * lax.rsqrt(jnp.mean(x * x, axis=-1, keepdims=True) + RMS_EPS) * g


def _adaln_kernel(c_ref, w_ref, b_ref, o_ref):
    a = _silu(c_ref[...])
    o_ref[0] = _dot_hi(a, w_ref[0]) + b_ref[0]


def _adaln(cc, w_mod, b_mod):
    depth, d, d3 = w_mod.shape
    tn = 1024
    return pl.pallas_call(
        _adaln_kernel,
        out_shape=jax.ShapeDtypeStruct((depth, 8, d3), F32),
        grid=(depth, d3 // tn),
        in_specs=[
            pl.BlockSpec((8, d), lambda l, j: (0, 0)),
            pl.BlockSpec((1, d, tn), lambda l, j: (l, 0, j)),
            pl.BlockSpec((1, 1, tn), lambda l, j: (l, 0, j)),
        ],
        out_specs=pl.BlockSpec((1, 8, tn), lambda l, j: (l, 0, j)),
        compiler_params=_cparams(("parallel", "parallel")),
        name="adaln",
    )(cc, w_mod, b_mod.reshape(depth, 1, d3))


def _seg_row(blk, tiles_per_batch):
    return jnp.where(blk % tiles_per_batch == tiles_per_batch - 1, 0, 1 + blk // tiles_per_batch)


def _proj_kernel(z_ref, *refs, nsub):
    mod_refs = refs[:2 * nsub]
    g_ref, w_ref, o_ref, h_ref = refs[2 * nsub:]

    @pl.when(pl.program_id(1) == 0)
    def _():
        y = _rms(z_ref[...], g_ref[...])
        for s in range(nsub):
            rows = slice(s * ROW_BLK, (s + 1) * ROW_BLK)
            shift, scale = mod_refs[2 * s][0], mod_refs[2 * s + 1][0]
            h_ref[rows, :] = (y[rows, :] * (1.0 + scale) + shift).astype(BF16)

    o_ref[...] = _dot(h_ref[...], w_ref[...]).astype(o_ref.dtype)


def _proj_in(z2, mods, g_pre, w, *, tiles_per_batch):
    n, d = z2.shape
    tm = 1024 if n % 1024 == 0 else ROW_BLK
    tn = PCOLS // 4
    nsub = tm // ROW_BLK
    mod_specs = []
    for s in range(nsub):
        for col in (0, 1):
            mod_specs.append(pl.BlockSpec(
                (1, 1, d), lambda i, j, s=s, col=col: (_seg_row(i * nsub + s, tiles_per_batch), 0, col)))
    return pl.pallas_call(
        functools.partial(_proj_kernel, nsub=nsub),
        out_shape=jax.ShapeDtypeStruct((n, PCOLS), ACT),
        grid=(n // tm, PCOLS // tn),
        in_specs=[pl.BlockSpec((tm, d), lambda i, j: (i, 0))] + mod_specs + [
            pl.BlockSpec((1, d), lambda i, j: (0, 0)),
            pl.BlockSpec((d, tn), lambda i, j: (0, j)),
        ],
        out_specs=pl.BlockSpec((tm, tn), lambda i, j: (i, j)),
        scratch_shapes=[pltpu.VMEM((tm, d), BF16)],
        compiler_params=_cparams(("parallel", "arbitrary")),
        name="proj_in",
    )(z2, *([mods] * (2 * nsub)), g_pre, w)


def _mla_prep_kernel(qb_ref, kb_ref, cos_ref, sin_ref, gq_ref, gkv_ref, wq_ref, wk_ref, wv_ref,
                     vone_ref, qt_ref, k_ref, vt_ref):
    nh = N_HEADS_MLA
    hw = nh * HEAD_PAD
    qb = qb_ref[0].astype(F32)
    kb = kb_ref[0].astype(F32)
    cos8 = jnp.tile(cos_ref[...], (1, nh))
    sin8 = jnp.tile(sin_ref[...], (1, nh))
    qn = _rms(qb[:, :Q_LORA], gq_ref[...]).astype(BF16)
    qq = _dot(qn, wq_ref[...])
    q = (qq[:, :hw] * cos8 + qq[:, hw:] * sin8) * (ATTN_SCALE * LOG2E)
    kvn = _rms(kb[:, :KV_LORA], gkv_ref[...]).astype(BF16)
    kin = jnp.concatenate([kvn, qb[:, Q_LORA:].astype(BF16)], axis=1)
    kk = _dot(kin, wk_ref[...])
    k_ref[0] = (kk[:, :hw] * cos8 + kk[:, hw:] * sin8).astype(BF16)
    v = _dot(kvn, wv_ref[...]) + vone_ref[...]
    for h in range(nh):
        lanes = slice(h * HEAD_PAD, (h + 1) * HEAD_PAD)
        qt_ref[0, h] = q[:, lanes].T.astype(BF16)
        vt_ref[0, h] = v[:, lanes].T[:VT_ROWS].astype(BF16)


def _mla_prep(p3, cos_t, sin_t, g_q, g_kv, wq, wk, wv, vone, *, tm):
    b, t, _ = p3.shape
    hw = N_HEADS_MLA * HEAD_PAD
    const = lambda shape: pl.BlockSpec(shape, lambda bi, i: (0,) * len(shape))
    return pl.pallas_call(
        _mla_prep_kernel,
        out_shape=(
            jax.ShapeDtypeStruct((b, N_HEADS_MLA, HEAD_PAD, t), BF16),
            jax.ShapeDtypeStruct((b, t, hw), BF16),
            jax.ShapeDtypeStruct((b, N_HEADS_MLA, VT_ROWS, t), BF16),
        ),
        grid=(b, t // tm),
        in_specs=[
            pl.BlockSpec((1, tm, 512), lambda bi, i: (bi, i, PB_Q)),
            pl.BlockSpec((1, tm, 512), lambda bi, i: (bi, i, PB_KV)),
            pl.BlockSpec((tm, HEAD_PAD), lambda bi, i: (i, 0)),
            pl.BlockSpec((tm, HEAD_PAD), lambda bi, i: (i, 0)),
            const((1, Q_LORA)),
            const((1, KV_LORA)),
            const(wq.shape),
            const(wk.shape),
            const(wv.shape),
            const((1, hw)),
        ],
        out_specs=(
            pl.BlockSpec((1, N_HEADS_MLA, HEAD_PAD, tm), lambda bi, i: (bi, 0, 0, i)),
            pl.BlockSpec((1, tm, hw), lambda bi, i: (bi, i, 0)),
            pl.BlockSpec((1, N_HEADS_MLA, VT_ROWS, tm), lambda bi, i: (bi, 0, 0, i)),
        ),
        compiler_params=_cparams(("parallel", "parallel")),
        name="mla_prep",
    )(p3, p3, cos_t, sin_t, g_q, g_kv, wq, wk, wv, vone)


def _attn_kernel(qt_ref, k_ref, vt_ref, *rest, nk):
    o_ref, m_ref, acc_ref = rest[-3:]
    j = pl.program_id(2)

    @pl.when(j == 0)
    def _():
        m_ref[...] = jnp.full(m_ref.shape, -jnp.inf, F32)
        acc_ref[...] = jnp.zeros(acc_ref.shape, F32)

    nh = N_HEADS_MLA
    mt = MXU_TILE
    nqb = qt_ref.shape[3] // mt

    def qk_tiles(h):
        out = {}
        k_h = k_ref[0, :, h * HEAD_PAD:(h + 1) * HEAD_PAD]

        def make(n):
            def run():
                out[n] = _dot(k_h, qt_ref[0, h, :, n * mt:(n + 1) * mt])
            return run
        return out, [make(n) for n in range(nqb)]

    def softmax(h, s):
        p, alpha = {}, {}
        for n in range(nqb):
            lanes = slice(n * mt, (n + 1) * mt)
            m_prev = m_ref[h, :, lanes]
            m_new = jnp.maximum(m_prev, jnp.max(s[n], axis=0, keepdims=True))
            alpha[n] = jnp.exp2(m_prev - m_new)[0:1, :]
            p[n] = jnp.exp2((s[n] - m_new[0:1, :]).astype(BF16))
            m_ref[h, :, lanes] = m_new
        return p, alpha

    def pv_tiles(h, p, alpha):
        vt_h = vt_ref[0, h]

        def make(n):
            def run():
                lanes = slice(n * mt, (n + 1) * mt)
                acc_ref[h, :, lanes] = alpha[n] * acc_ref[h, :, lanes] + _dot(vt_h, p[n])
            return run
        return [make(n) for n in range(nqb)]

    def run_interleaved(a_ops, c_ops):
        for i in range(max(len(a_ops), len(c_ops))):
            if i < len(a_ops):
                a_ops[i]()
            if i < len(c_ops):
                c_ops[i]()

    s, pa = {}, {}
    for h in range(min(2, nh)):
        s[h], ops = qk_tiles(h)
        run_interleaved(ops, [])
    pa[0] = softmax(0, s[0])
    for h in range(nh):
        if h + 1 < nh:
            pa[h + 1] = softmax(h + 1, s[h + 1])
        a_ops = []
        if h + 2 < nh:
            s[h + 2], a_ops = qk_tiles(h + 2)
        run_interleaved(a_ops, pv_tiles(h, *pa[h]))

    @pl.when(j == nk - 1)
    def _():
        outs = []
        for h in range(N_HEADS_MLA):
            a = acc_ref[h]
            outs.append(a[:V_HEAD] * (1.0 / a[V_HEAD:V_HEAD + 1]))
        o_ref[0] = jnp.concatenate(outs, axis=0).T.astype(o_ref.dtype)


def _attention(qt, k, vt, y_prev, *, tq, tk, nq, nk, q_off, k_off):
    b, t, hw = k.shape
    in_specs = [
        pl.BlockSpec((1, N_HEADS_MLA, HEAD_PAD, tq), lambda bi, i, j: (bi, 0, 0, i + q_off)),
        pl.BlockSpec((1, tk, hw), lambda bi, i, j: (bi, j + k_off, 0)),
        pl.BlockSpec((1, N_HEADS_MLA, VT_ROWS, tk), lambda bi, i, j: (bi, 0, 0, j + k_off)),
    ]
    args = [qt, k, vt]
    aliases = {}
    if y_prev is not None:
        in_specs.append(pl.BlockSpec(memory_space=pl.ANY))
        args.append(y_prev)
        aliases = {3: 0}
    return pl.pallas_call(
        functools.partial(_attn_kernel, nk=nk),
        out_shape=jax.ShapeDtypeStruct((b, t, W_MLA), ACT),
        grid=(b, nq, nk),
        in_specs=in_specs,
        out_specs=pl.BlockSpec((1, tq, W_MLA), lambda bi, i, j: (bi, i + q_off, 0)),
        scratch_shapes=[
            pltpu.VMEM((N_HEADS_MLA, 8, tq), F32),
            pltpu.VMEM((N_HEADS_MLA, VT_ROWS, tq), F32),
        ],
        input_output_aliases=aliases,
        compiler_params=_cparams(("parallel", "parallel", "arbitrary")),
        name="attn_ctx" if y_prev is not None else "attn_x",
    )(*args)


def _half_lane_masks():
    lane = lax.broadcasted_iota(jnp.int32, (CHUNK, LANES), 1)
    first = lane < RWKV_HEAD
    return first, jnp.where(first, 1.0, 0.0).astype(BF16), jnp.where(first, 0.0, 1.0).astype(BF16)


def _bd4(x, lo, hi):
    xb = x.astype(BF16)
    xl, xr = xb[:, :LANES], xb[:, LANES:]
    z = jnp.zeros((2 * CHUNK, LANES), BF16)
    c0 = jnp.concatenate([xl * lo, xl * hi, z], axis=0)
    c1 = jnp.concatenate([z, xr * lo, xr * hi], axis=0)
    return jnp.concatenate([c0, c1], axis=1)


def _diag_blocks(full, first):
    c = CHUNK
    left = jnp.where(first, full[0:c, :LANES], full[c:2 * c, :LANES])
    right = jnp.where(first, full[2 * c:3 * c, LANES:], full[3 * c:4 * c, LANES:])
    return jnp.concatenate([left, right], axis=1)


def _rwkv_prep_kernel(rkv_ref, lora_ref, w0_ref, a0_ref, wup_ref, aup_ref, kk_ref, ka_ref, rk_ref,
                      ones_ref, rh_ref, y0_ref, g_ref, h_ref, bonus_ref):
    fwd = pl.program_id(0) == 0
    c = CHUNK
    w = W_RWKV
    grp = PREP_GROUP
    ones4 = ones_ref[...]
    sgn = jnp.where(fwd, 1, -1)
    ti = lax.broadcasted_iota(jnp.int32, (grp, grp), 0)
    si = lax.broadcasted_iota(jnp.int32, (grp, grp), 1)
    same = jnp.where((ti // c) == (si // c), 1.0, 0.0)
    tri = jnp.where((si - ti) * sgn <= 0, same, 0.0).astype(BF16)
    first, lo, hi = _half_lane_masks()
    tq = lax.broadcasted_iota(jnp.int32, (c, QUAD), 0)
    sq = lax.broadcasted_iota(jnp.int32, (c, QUAD), 1) % c
    before = (sq - tq) * sgn < 0
    upto = (sq - tq) * sgn <= 0
    eye = jnp.where(sq == tq, 1.0, 0.0)

    def head_sum(x):
        xh, xl = _split2(x)
        parts = []
        for qd in range(w // QUAD):
            ls = slice(qd * QUAD, (qd + 1) * QUAD)
            parts.append(_dot(xh[:, ls], ones4) + _dot(xl[:, ls], ones4))
        return jnp.concatenate(parts, axis=1)

    def pm(x, y):
        return _dot(x.astype(BF16), _bd4(y, lo, hi))

    def prologue(g, out):
        gs = slice(g * grp, (g + 1) * grp)
        rkv = rkv_ref[gs, :].astype(F32)
        r, k, v = rkv[:, :w], rkv[:, w:2 * w], rkv[:, 2 * w:]
        lora = lora_ref[gs, :].astype(F32)
        zw = w0_ref[0] + _dot(jnp.tanh(lora[:, :W_LORA]).astype(BF16), wup_ref[0])
        za = a0_ref[0] + _dot(lora[:, W_LORA:].astype(BF16), aup_ref[0])
        yield
        ell = -math.exp(-0.5) * _sigmoid(zw)
        a = _sigmoid(za)
        kkr = k * kk_ref[...]
        k_d = k * (1.0 + (a - 1.0) * ka_ref[...])
        kk_ss = head_sum(kkr * kkr)
        rk_s = head_sum(r * k_d * rk_ref[...])
        yield
        kk = kkr * lax.rsqrt(kk_ss + L2_EPS)
        bonus_ref[0, gs, :] = (rk_s * v).astype(bonus_ref.dtype)
        ell_hi, ell_lo = _split2(ell)
        lc = _dot(tri, ell_hi) + _dot(tri, ell_lo)
        yield
        ltot = jnp.concatenate(
            [jnp.broadcast_to(jnp.where(fwd, lc[ch * c + c - 1:ch * c + c], lc[ch * c:ch * c + 1]), (c, w))
             for ch in range(grp // c)], axis=0)
        e_neg = jnp.exp(-lc)
        e_tail = jnp.exp(ltot - lc)
        kka = kk * a
        out.update(abar=kk * jnp.exp(lc - ell), bbar=kka * e_neg, kbar=k_d * e_neg, rbar=r * jnp.exp(lc),
                   btil=kka * e_tail, ktil=k_d * e_tail, v=v, gam=jnp.exp(ltot))

    def chains(g, q):
        sl = [(slice(ch * c, (ch + 1) * c), slice(qd * QUAD, (qd + 1) * QUAD))
              for ch in range(grp // c) for qd in range(w // QUAD)]
        abar, bbar, kbar, rbar, btil, ktil, v, gam = (
            q[nm] for nm in ("abar", "bbar", "kbar", "rbar", "btil", "ktil", "v", "gam"))
        la = [jnp.concatenate([abar[s], rbar[s]], axis=0).astype(BF16) for s in sl]
        nb = [_dot_nt(la_i, _bd4(bbar[s], lo, hi)) for la_i, s in zip(la, sl)]
        yield
        nk = [_dot_nt(la_i, _bd4(kbar[s], lo, hi)) for la_i, s in zip(la, sl)]
        yield
        n = [jnp.where(before, t[:c], 0.0) for t in nb]
        a_rb = [jnp.where(upto, t[c:], 0.0) for t in nb]
        a_ak = [jnp.where(before, t[:c], 0.0) for t in nk]
        a_rk = [jnp.where(upto, t[c:], 0.0) for t in nk]
        x = [eye - t for t in n]
        p = [pm(t, t) for t in n]
        yield
        py = [pm(jnp.concatenate([u, r_], axis=0), v[s]) for u, r_, s in zip(a_ak, a_rk, sl)]
        yield
        for it in range(5):
            if it < 4:
                xp = [pm(jnp.concatenate([x_i, p_i], axis=0), p_i) for x_i, p_i in zip(x, p)]
                x = [x_i + t[:c] for x_i, t in zip(x, xp)]
                p = [t[c:] for t in xp]
            else:
                x = [x_i + pm(x_i, p_i) for x_i, p_i in zip(x, p)]
            yield
        ah = [pm(x_i, abar[s]) for x_i, s in zip(x, sl)]
        u0 = [pm(x_i, t[:c]) for x_i, t in zip(x, py)]
        yield
        ra = [pm(m_i, t) for m_i, t in zip(a_rb, ah)]
        ru = [pm(m_i, t) for m_i, t in zip(a_rb, u0)]
        gfull = [_dot_tn(btil[s].astype(BF16), t.astype(BF16)) for s, t in zip(sl, ah)]
        hfull = [_dot_tn(jnp.concatenate([ktil[s], -btil[s]], axis=0).astype(BF16),
                         jnp.concatenate([v[s], t], axis=0).astype(BF16)) for s, t in zip(sl, u0)]
        for i, (rs, ls) in enumerate(sl):
            ro = slice(g * grp + rs.start, g * grp + rs.stop)
            rh_ref[0, ro, ls] = (rbar[rs, ls] - ra[i]).astype(BF16)
            y0_ref[0, ro, ls] = py[i][c:] - ru[i]
            g_ref[0, ro, ls] = (eye * gam[rs, ls] - _diag_blocks(gfull[i], first)).astype(g_ref.dtype)
            h_ref[0, ro, ls] = _diag_blocks(hfull[i], first)

    ngroups = rkv_ref.shape[0] // grp
    vals = [dict() for _ in range(ngroups)]
    for _ in prologue(0, vals[0]):
        pass
    for g in range(ngroups):
        nxt = prologue(g + 1, vals[g + 1]) if g + 1 < ngroups else iter(())
        for stage, _ in enumerate(chains(g, vals[g])):
            if stage >= 1 and stage % 2 == 1:
                next(nxt, None)
        for _ in nxt:
            pass


def _rwkv_prep(p2, w0, a0, w_up, a_up, k_k, k_a, r_k, ones4):
    n, _ = p2.shape
    w = W_RWKV
    rows = PREP_ROWS if n % PREP_ROWS == 0 else PREP_GROUP
    const = lambda shape: pl.BlockSpec(shape, lambda d, i: (0,) * len(shape))
    perdir = lambda shape: pl.BlockSpec((1,) + shape, lambda d, i: (d,) + (0,) * len(shape))
    out_spec = pl.BlockSpec((1, rows, w), lambda d, i: (d, i, 0))
    f32_out = jax.ShapeDtypeStruct((2, n, w), F32)
    act_out = jax.ShapeDtypeStruct((2, n, w), ACT)
    return pl.pallas_call(
        _rwkv_prep_kernel,
        out_shape=(act_out, f32_out, act_out, f32_out, act_out),
        grid=(2, n // rows),
        in_specs=[
            pl.BlockSpec((rows, 3 * w), lambda d, i: (i, PB_RKV)),
            pl.BlockSpec((rows, 2 * W_LORA), lambda d, i: (i, PB_LORA128 + d)),
            perdir((1, w)), perdir((1, w)), perdir((W_LORA, w)), perdir((A_LORA, w)),
            const((1, w)), const((1, w)), const((1, w)),
            const((QUAD, QUAD)),
        ],
        out_specs=(out_spec,) * 5,
        compiler_params=_cparams(("parallel", "parallel")),
        name="rwkv_prep",
    )(p2, p2, w0, a0, w_up, a_up, k_k, k_a, r_k, ones4)


def _rwkv_scan_kernel(rhf, y0f, gf, hf, rhb, y0b, gb, hb, yf_ref, yb_ref, st_ref, *, nb):
    @pl.when(pl.program_id(0) == 0)
    def _():
        st_ref[...] = jnp.zeros(st_ref.shape, F32)

    _, lo, hi = _half_lane_masks()
    c = CHUNK
    nsub = rhf.shape[2] // c
    for step in range(nsub):
        for d, sub, (rh, y0, g, hh, y_ref) in ((0, step, (rhf, y0f, gf, hf, yf_ref)),
                                               (1, nsub - 1 - step, (rhb, y0b, gb, hb, yb_ref))):
            rs = slice(sub * c, (sub + 1) * c)
            for bi in range(nb):
                for qd in range(W_RWKV // QUAD):
                    ls = slice(qd * QUAD, (qd + 1) * QUAD)
                    lhs = jnp.concatenate([rh[0, bi, rs, ls], g[0, bi, rs, ls]], axis=0)
                    res = _dot(lhs, _bd4(st_ref[d, bi, qd], lo, hi))
                    y_ref[bi, rs, ls] = (res[:c] + y0[0, bi, rs, ls]).astype(y_ref.dtype)
                    st_ref[d, bi, qd] = res[c:] + hh[0, bi, rs, ls]


def _rwkv_scan(rh, y0, g, h, *, n_x, n_ctx):
    _, b, t, w = rh.shape
    c = ROW_BLK
    n = n_x + n_ctx
    fidx = lambda j: jnp.where(j < n_ctx, n_x + j, j - n_ctx)
    bidx = lambda j: n - 1 - j
    fspec = pl.BlockSpec((1, b, c, w), lambda j: (0, 0, fidx(j), 0))
    bspec = pl.BlockSpec((1, b, c, w), lambda j: (1, 0, bidx(j), 0))
    y_shape = jax.ShapeDtypeStruct((b, t, w), ACT)
    return pl.pallas_call(
        functools.partial(_rwkv_scan_kernel, nb=b),
        out_shape=(y_shape, y_shape),
        grid=(n,),
        in_specs=[fspec] * 4 + [bspec] * 4,
        out_specs=(
            pl.BlockSpec((b, c, w), lambda j: (0, fidx(j), 0)),
            pl.BlockSpec((b, c, w), lambda j: (0, bidx(j), 0)),
        ),
        scratch_shapes=[pltpu.VMEM((2, b, w // QUAD, RWKV_HEAD, QUAD), F32)],
        compiler_params=_cparams(("arbitrary",)),
        name="rwkv_scan",
    )(rh, y0, g, h, rh, y0, g, h)


def _merge_kernel(z_ref, gate_x_ref, gate_c_ref, gl_ref, grw_ref, gml_ref, cin_ref, cb_ref, cc_ref, gcv_ref,
                  cin_p, cc_p, cin_n, cc_n, ymla_ref, yf_ref, yb_ref, bon_ref,
                  wbm_ref, wbc_ref, wbr_ref, wout_ref, gpost_ref, cw_ref, cbias_ref, gng_ref, gnb_ref,
                  avg_ref, o_ref, *, tiles_per_batch, seq, ctx_len):
    tm = z_ref.shape[0]
    d = D_MODEL
    pos = (pl.program_id(0) % tiles_per_batch) * tm + lax.broadcasted_iota(jnp.int32, (tm, 1), 0)
    has_prev = jnp.logical_and(pos != 0, pos != seq)
    has_next = jnp.logical_and(pos != seq - 1, pos != seq + ctx_len - 1)

    def f32(ref, idx=slice(None)):
        return ref[idx].astype(F32)

    u = f32(cc_ref) * f32(cin_ref)
    hl = HALO_ROWS - 1
    u_halo_p = f32(cc_p, slice(hl, hl + 1)) * f32(cin_p, slice(hl, hl + 1))
    u_halo_n = f32(cc_n, slice(0, 1)) * f32(cin_n, slice(0, 1))
    row = lax.broadcasted_iota(jnp.int32, u.shape, 0)
    u_prev = jnp.where(has_prev, jnp.where(row == 0, u_halo_p, pltpu.roll(u, 1, axis=0)), 0.0)
    u_next = jnp.where(has_next, jnp.where(row == tm - 1, u_halo_n, pltpu.roll(u, tm - 1, axis=0)), 0.0)
    cw = cw_ref[...]
    y_conv = f32(cb_ref) * (u_prev * cw[0:1] + u * cw[1:2] + u_next * cw[2:3] + cbias_ref[...])

    avg = avg_ref[...]

    def head_mean(x):
        hi, lo = _split2(x)
        parts = []
        for qd in range(W_RWKV // QUAD):
            ls = slice(qd * QUAD, (qd + 1) * QUAD)
            parts.append(_dot(hi[:, ls], avg) + _dot(lo[:, ls], avg))
        return jnp.concatenate(parts, axis=1)

    yr = f32(yf_ref) + f32(yb_ref)
    mu = head_mean(yr)
    dv = yr - mu
    var = head_mean(dv * dv)
    y_rwkv = dv * lax.rsqrt(var + GN_EPS) * gng_ref[...] + gnb_ref[...] + f32(bon_ref, 0) + f32(bon_ref, 1)

    br_mla = _dot((f32(ymla_ref) * _silu(f32(gml_ref))).astype(BF16), wbm_ref[...])
    br_conv = _dot((y_conv * _silu(f32(gcv_ref))).astype(BF16), wbc_ref[...])
    br_rwkv = _dot((y_rwkv * _silu(f32(grw_ref))).astype(BF16), wbr_ref[...])
    s = _sigmoid(f32(gl_ref))
    merged = s[:, :d] * br_mla + s[:, d:2 * d] * br_conv + s[:, 2 * d:] * br_rwkv
    o = _dot(merged.astype(BF16), wout_ref[...])
    gate = jnp.where(pos >= seq, gate_c_ref[0], gate_x_ref[0])
    o_ref[...] = z_ref[...] + gate * _rms(o, gpost_ref[...])


def _merge(z2, mods, p2, y_mla, yf, yb, bonus, wbm, wbc, wbr, wout, g_post, conv_w, conv_b,
           gn_g, gn_b, avg_bd, *, seq, ctx_len, tm, x_only):
    n, d = z2.shape
    t = seq + ctx_len
    assert t % tm == 0 and tm % HALO_ROWS == 0 and (not x_only or seq % tm == 0)
    hb = tm // HALO_ROWS
    nhb = n // HALO_ROWS
    tiles_per_batch = t // tm
    x_tiles = seq // tm
    n_tiles = (n // t) * x_tiles if x_only else n // tm
    gi = (lambda i: i // x_tiles * tiles_per_batch + i % x_tiles) if x_only else (lambda i: i)
    pcol = lambda blk: pl.BlockSpec((tm, 512), lambda i: (gi(i), blk))
    prev = lambda blk: pl.BlockSpec((HALO_ROWS, 512), lambda i: (jnp.maximum(gi(i) * hb - 1, 0), blk))
    nxt = lambda blk: pl.BlockSpec((HALO_ROWS, 512), lambda i: (jnp.minimum((gi(i) + 1) * hb, nhb - 1), blk))
    const = lambda shape: pl.BlockSpec(shape, lambda i: (0,) * len(shape))
    row512 = pl.BlockSpec((tm, 512), lambda i: (gi(i), 0))
    return pl.pallas_call(
        functools.partial(_merge_kernel, tiles_per_batch=x_tiles if x_only else tiles_per_batch,
                          seq=seq, ctx_len=ctx_len),
        out_shape=jax.ShapeDtypeStruct((n_tiles * tm, d), F32),
        grid=(n_tiles,),
        in_specs=[
            pl.BlockSpec((tm, d), lambda i: (gi(i), 0)),
            pl.BlockSpec((1, 1, d), lambda i: (1 + gi(i) // tiles_per_batch, 0, 2)),
            pl.BlockSpec((1, 1, d), lambda i: (0, 0, 2)),
            pl.BlockSpec((tm, 3 * d), lambda i: (gi(i), PB_GATE)),
            pcol(PB_GRWKV), pcol(PB_GMLA), pcol(PB_CVIN), pcol(PB_CVB), pcol(PB_CVC), pcol(PB_GCONV),
            prev(PB_CVIN), prev(PB_CVC), nxt(PB_CVIN), nxt(PB_CVC),
            row512, row512, row512,
            pl.BlockSpec((2, tm, 512), lambda i: (0, gi(i), 0)),
            const(wbm.shape), const(wbc.shape), const(wbr.shape), const(wout.shape),
            const((1, d)), const(conv_w.shape), const((1, 512)), const((1, 512)), const((1, 512)),
            const(avg_bd.shape),
        ],
        out_specs=pl.BlockSpec((tm, d), lambda i: (i, 0)),
        compiler_params=_cparams(("parallel",)),
        name="merge",
    )(z2, mods, mods, p2, p2, p2, p2, p2, p2, p2, p2, p2, p2, p2, y_mla, yf, yb, bonus,
      wbm, wbc, wbr, wout, g_post, conv_w, conv_b, gn_g, gn_b, avg_bd)


def _pair_swap(w):
    s = w.shape
    return w.reshape(s[:-1] + (s[-1] // 2, 2))[..., ::-1].reshape(s)


def _layout_w_in(w_in):
    sizes = (Q_LORA, KV_LORA, QK_ROPE, W_MLA, CONV_W, CONV_W, CONV_W, CONV_W, W_RWKV, W_RWKV, W_RWKV,
             W_LORA, W_LORA, A_LORA, A_LORA, W_RWKV, 3 * D_MODEL)
    offs = np.concatenate([[0], np.cumsum(sizes)])
    names = ("q_lat", "kv_lat", "kr", "g_mla", "cv_in", "cv_b", "cv_c", "g_conv", "r", "k", "v",
             "wd_f", "wd_b", "ad_f", "ad_b", "g_rwkv", "gl")
    col = {nm: w_in[..., offs[i]:offs[i + 1]] for i, nm in enumerate(names)}
    zeros = jnp.zeros(w_in.shape[:-1] + (512 - Q_LORA - 2 * QK_ROPE,), w_in.dtype)
    parts = [col["gl"], col["r"], col["k"], col["v"], col["g_rwkv"], col["g_mla"], col["cv_in"],
             col["cv_b"], col["cv_c"], col["g_conv"],
             col["q_lat"], col["kr"], _pair_swap(col["kr"]), zeros,
             col["kv_lat"], col["wd_f"], col["ad_f"], col["wd_b"], col["ad_b"]]
    out = jnp.concatenate(parts, axis=-1).astype(BF16)
    assert out.shape[-1] == PCOLS
    return out


def _layout_mla_weights(w_uq, w_ukv):
    depth = w_uq.shape[0]
    nh = N_HEADS_MLA
    wq = w_uq.reshape(depth, Q_LORA, nh, QK_HEAD)
    q_nope, q_rope = wq[..., :QK_NOPE], wq[..., QK_NOPE:]
    zq = jnp.zeros((depth, Q_LORA, nh, HEAD_PAD - QK_HEAD), w_uq.dtype)
    wq_a = jnp.concatenate([q_nope, q_rope, zq], axis=-1)
    wq_b = jnp.concatenate([jnp.zeros_like(q_nope), _pair_swap(q_rope), zq], axis=-1)
    wq_all = jnp.concatenate([wq_a.reshape(depth, Q_LORA, nh * HEAD_PAD),
                              wq_b.reshape(depth, Q_LORA, nh * HEAD_PAD)], axis=-1).astype(BF16)

    wkv = w_ukv.reshape(depth, KV_LORA, nh, QK_NOPE + V_HEAD)
    k_nope, v_w = wkv[..., :QK_NOPE], wkv[..., QK_NOPE:]
    zk = jnp.zeros((depth, KV_LORA, nh, HEAD_PAD - QK_NOPE), w_ukv.dtype)
    wk_top = jnp.concatenate([k_nope, zk], axis=-1).reshape(depth, KV_LORA, nh * HEAD_PAD)
    wv = jnp.concatenate([v_w, zk], axis=-1).reshape(depth, KV_LORA, nh * HEAD_PAD).astype(BF16)
    place = np.zeros((HEAD_PAD, HEAD_PAD), np.float32)
    place[np.arange(QK_ROPE), QK_NOPE + np.arange(QK_ROPE)] = 1.0
    e_a = np.tile(place, (1, nh))
    place_b = np.zeros((HEAD_PAD, HEAD_PAD), np.float32)
    place_b[QK_ROPE + np.arange(QK_ROPE), QK_NOPE + np.arange(QK_ROPE)] = 1.0
    e_b = np.tile(place_b, (1, nh))
    top = jnp.concatenate([wk_top, jnp.zeros_like(wk_top)], axis=-1)
    bot = jnp.broadcast_to(jnp.asarray(np.concatenate([e_a, e_b], axis=1)), (depth, HEAD_PAD, 2 * nh * HEAD_PAD))
    wk_all = jnp.concatenate([top, bot.astype(top.dtype)], axis=1).astype(BF16)
    return wq_all, wk_all, wv


def _rope_tables(seq, ctx_len):
    n_freq = QK_ROPE // 4
    pos = np.arange(seq)
    inv = ROPE_THETA ** (-np.arange(n_freq, dtype=np.float32) / n_freq)
    row = (pos // GRID_W).astype(np.float32)
    colp = (pos % GRID_W).astype(np.float32)
    ang = jnp.concatenate([jnp.asarray(row)[:, None] * jnp.asarray(inv), jnp.asarray(colp)[:, None] * jnp.asarray(inv)], axis=-1)
    cos, sin = jnp.cos(ang), jnp.sin(ang)
    cos2 = jnp.repeat(cos, 2, axis=-1)
    sin2 = jnp.stack([-sin, sin], axis=-1).reshape(seq, QK_ROPE)
    ones = jnp.ones((seq, QK_NOPE), F32)
    pad = jnp.zeros((seq, HEAD_PAD - QK_HEAD), F32)
    cos_x = jnp.concatenate([ones, cos2, pad], axis=-1)
    sin_x = jnp.concatenate([jnp.zeros_like(ones), sin2, pad], axis=-1)
    cos_c = jnp.concatenate([jnp.ones((ctx_len, QK_HEAD), F32), jnp.zeros((ctx_len, HEAD_PAD - QK_HEAD), F32)], axis=-1)
    sin_c = jnp.zeros((ctx_len, HEAD_PAD), F32)
    return jnp.concatenate([cos_x, cos_c], axis=0), jnp.concatenate([sin_x, sin_c], axis=0)


def _block_diag_const(n, blk, value):
    i = np.arange(n)
    return np.where((i[:, None] // blk) == (i[None, :] // blk), value, 0.0).astype(np.float32)


def _pick(n, candidates):
    for cand in candidates:
        if n % cand == 0:
            return cand
    raise ValueError(f"no tile for {n}")


def kernel(x, c, ctx, c_ctx, w_mod, b_mod, g_pre, g_post, w_in, g_q, g_kv, w_uq, w_ukv, conv_w, conv_b,
           w0, w_up, a0, a_up, k_k, k_a, r_k, gn_g, gn_b, w_br_mla, w_br_conv, w_br_rwkv, w_out):
    bsz, seq, d = x.shape
    ctx_len = ctx.shape[1]
    depth = w_mod.shape[0]
    assert d == D_MODEL and ctx_len == ROW_BLK and seq % ROW_BLK == 0
    t = seq + ctx_len
    n = bsz * t
    tiles_per_batch = t // ROW_BLK

    cc = jnp.zeros((8, d), F32).at[0].set(c_ctx).at[1:1 + bsz].set(c)
    mods = _adaln(cc, w_mod, b_mod).reshape(depth, 8, 1, 3 * d)

    w_in_p = _layout_w_in(w_in)
    wq_all, wk_all, wv_all = _layout_mla_weights(w_uq, w_ukv)
    cos_t, sin_t = _rope_tables(seq, ctx_len)
    vone = np.zeros((1, N_HEADS_MLA * HEAD_PAD), np.float32)
    vone[0, V_HEAD::HEAD_PAD] = 1.0
    vone = jnp.asarray(vone)
    ones4 = jnp.asarray(_block_diag_const(QUAD, RWKV_HEAD, 1.0), BF16)
    avg_bd = jnp.asarray(_block_diag_const(QUAD, RWKV_HEAD, 1.0 / RWKV_HEAD), BF16)

    row_tile = _pick(t, (768, 512, 256))
    tq = _pick(seq, (2048, 1024, 512, 256))
    tk = _pick(t, (768, 512, 256))

    z = jnp.concatenate([x, ctx], axis=1)
    for l in range(depth):
        z2 = z.reshape(n, d)
        p2 = _proj_in(z2, mods[l], g_pre[l][None], w_in_p[l], tiles_per_batch=tiles_per_batch)
        p3 = p2.reshape(bsz, t, PCOLS)

        qt, k, vt = _mla_prep(p3, cos_t, sin_t, g_q[l][None], g_kv[l][None], wq_all[l], wk_all[l], wv_all[l], vone,
                              tm=row_tile)
        y_mla = _attention(qt, k, vt, None, tq=tq, tk=tk, nq=seq // tq, nk=t // tk, q_off=0, k_off=0)
        y_mla = _attention(qt, k, vt, y_mla, tq=ctx_len, tk=ctx_len, nq=1, nk=1,
                           q_off=seq // ctx_len, k_off=seq // ctx_len)

        rh, y0, g, h, bonus = _rwkv_prep(
            p2, w0[l][:, None], a0[l][:, None], w_up[l].astype(BF16), a_up[l].astype(BF16),
            k_k[l][None], k_a[l][None], r_k[l].reshape(1, W_RWKV), ones4)
        per_batch = lambda arr: arr.reshape(2, bsz, t, W_RWKV)
        yf, yb = _rwkv_scan(per_batch(rh), per_batch(y0), per_batch(g), per_batch(h),
                            n_x=seq // ROW_BLK, n_ctx=ctx_len // ROW_BLK)

        z2 = _merge(z2, mods[l], p2, y_mla.reshape(n, W_MLA), yf.reshape(n, W_RWKV), yb.reshape(n, W_RWKV),
                    bonus,
                    w_br_mla[l].astype(BF16), w_br_conv[l].astype(BF16), w_br_rwkv[l].astype(BF16),
                    w_out[l].astype(BF16), g_post[l][None], conv_w[l], conv_b[l][None],
                    gn_g[l][None], gn_b[l][None], avg_bd,
                    seq=seq, ctx_len=ctx_len, x_only=l == depth - 1,
                    tm=ROW_BLK if l == depth - 1 else row_tile)
        if l < depth - 1:
            z = z2.reshape(bsz, t, d)
    return z2.reshape(bsz, seq, d)
```

```python
import functools
import math

import numpy as np
import jax
import jax.numpy as jnp
from jax import lax
from jax.experimental import pallas as pl
from jax.experimental.pallas import tpu as pltpu

F32 = jnp.float32
BF16 = jnp.bfloat16
ACT = BF16

D_MODEL = 1024
GRID_W = 64
N_HEADS_MLA = 8
Q_LORA = 384
KV_LORA = 256
QK_NOPE = 64
QK_ROPE = 32
QK_HEAD = QK_NOPE + QK_ROPE
V_HEAD = 64
W_MLA = N_HEADS_MLA * V_HEAD
ROPE_THETA = 10000.0
ATTN_SCALE = QK_HEAD ** -0.5
CONV_W = 512
RWKV_HEADS = 8
RWKV_HEAD = 64
W_RWKV = RWKV_HEADS * RWKV_HEAD
W_LORA = 64
A_LORA = 64
RMS_EPS = 1e-6
GN_EPS = 64e-5
L2_EPS = 1e-12
LOG2E = math.log2(math.e)

LANES = 128
ROW_BLK = 256
CHUNK = 64
QUAD = 4 * RWKV_HEAD
PREP_GROUP = 256
PREP_ROWS = 1024
HEAD_PAD = 128
MXU_TILE = 256
HALO_ROWS = 16
VT_ROWS = 80
VMEM_LIMIT = 48 * 1024 * 1024

PCOLS = 17 * 512
PB_GATE = 0
PB_RKV = 2
PB_GRWKV = 9
PB_GMLA = 10
PB_CVIN = 11
PB_CVB = 12
PB_CVC = 13
PB_GCONV = 14
PB_Q = 15
PB_KV = 16
PB_LORA128 = (16 * 512 + 256) // 128


def _cparams(sem, vmem=VMEM_LIMIT):
    return pltpu.CompilerParams(dimension_semantics=sem, vmem_limit_bytes=vmem)


def _dot(a, b):
    return jnp.dot(a, b, preferred_element_type=F32)


def _dot_nt(a, b):
    return lax.dot_general(a, b, (((1,), (1,)), ((), ())), preferred_element_type=F32)


def _dot_tn(a, b):
    return lax.dot_general(a, b, (((0,), (0,)), ((), ())), preferred_element_type=F32)


def _split2(x):
    hi = x.astype(BF16)
    lo = (x - hi.astype(F32)).astype(BF16)
    return hi, lo


def _dot_hi(a, b):
    ah, al = _split2(a)
    bh, bl = _split2(b)
    return _dot(ah, bh) + _dot(ah, bl) + _dot(al, bh)


def _sigmoid(x):
    return 1.0 / (1.0 + jnp.exp(-x))


def _silu(x):
    return x * _sigmoid(x)


def _rms(x, g):
    return x * lax.rsqrt(jnp.mean(x * x, axis=-1, keepdims=True) + RMS_EPS) * g


def _adaln_kernel(c_ref, w_ref, b_ref, o_ref):
    a = _silu(c_ref[...])
    o_ref[0] = _dot_hi(a, w_ref[0]) + b_ref[0]


def _adaln(cc, w_mod, b_mod):
    depth, d, d3 = w_mod.shape
    tn = 1024
    return pl.pallas_call(
        _adaln_kernel,
        out_shape=jax.ShapeDtypeStruct((depth, 8, d3), F32),
        grid=(depth, d3 // tn),
        in_specs=[
            pl.BlockSpec((8, d), lambda l, j: (0, 0)),
            pl.BlockSpec((1, d, tn), lambda l, j: (l, 0, j)),
            pl.BlockSpec((1, 1, tn), lambda l, j: (l, 0, j)),
        ],
        out_specs=pl.BlockSpec((1, 8, tn), lambda l, j: (l, 0, j)),
        compiler_params=_cparams(("parallel", "parallel")),
        name="adaln",
    )(cc, w_mod, b_mod.reshape(depth, 1, d3))


def _seg_row(blk, tiles_per_batch):
    return jnp.where(blk % tiles_per_batch == tiles_per_batch - 1, 0, 1 + blk // tiles_per_batch)


def _proj_kernel(z_ref, *refs, nsub):
    mod_refs = refs[:2 * nsub]
    g_ref, w_ref, o_ref, h_ref = refs[2 * nsub:]

    @pl.when(pl.program_id(1) == 0)
    def _():
        y = _rms(z_ref[...], g_ref[...])
        for s in range(nsub):
            rows = slice(s * ROW_BLK, (s + 1) * ROW_BLK)
            shift, scale = mod_refs[2 * s][0], mod_refs[2 * s + 1][0]
            h_ref[rows, :] = (y[rows, :] * (1.0 + scale) + shift).astype(BF16)

    o_ref[...] = _dot(h_ref[...], w_ref[0]).astype(o_ref.dtype)


def _proj_in(z2, mods, g_pre, w_all, layer, *, tiles_per_batch):
    n, d = z2.shape
    tm = _pick(n, (1536, 1024, ROW_BLK))
    tn = PCOLS // 4
    nsub = tm // ROW_BLK
    mod_specs = []
    for s in range(nsub):
        for col in (0, 1):
            mod_specs.append(pl.BlockSpec(
                (1, 1, d), lambda i, j, s=s, col=col: (_seg_row(i * nsub + s, tiles_per_batch), 0, col)))
    return pl.pallas_call(
        functools.partial(_proj_kernel, nsub=nsub),
        out_shape=jax.ShapeDtypeStruct((n, PCOLS), ACT),
        grid=(n // tm, PCOLS // tn),
        in_specs=[pl.BlockSpec((tm, d), lambda i, j: (i, 0))] + mod_specs + [
            pl.BlockSpec((1, d), lambda i, j: (0, 0)),
            pl.BlockSpec((1, d, tn), lambda i, j: (layer, 0, j)),
        ],
        out_specs=pl.BlockSpec((tm, tn), lambda i, j: (i, j)),
        scratch_shapes=[pltpu.VMEM((tm, d), BF16)],
        compiler_params=_cparams(("parallel", "arbitrary")),
        name="proj_in",
    )(z2, *([mods] * (2 * nsub)), g_pre, w_all)


def _mla_prep_kernel(qb_ref, kb_ref, cos_ref, sin_ref, gq_ref, gkv_ref, wq_ref, wk_ref, wv_ref,
                     vone_ref, qt_ref, k_ref, vt_ref):
    nh = N_HEADS_MLA
    hw = nh * HEAD_PAD
    qb = qb_ref[0].astype(F32)
    kb = kb_ref[0].astype(F32)
    cos8 = jnp.tile(cos_ref[...], (1, nh))
    sin8 = jnp.tile(sin_ref[...], (1, nh))
    qn = _rms(qb[:, :Q_LORA], gq_ref[...]).astype(BF16)
    qq = _dot(qn, wq_ref[...])
    q = (qq[:, :hw] * cos8 + qq[:, hw:] * sin8) * (ATTN_SCALE * LOG2E)
    kvn = _rms(kb[:, :KV_LORA], gkv_ref[...]).astype(BF16)
    kin = jnp.concatenate([kvn, qb[:, Q_LORA:].astype(BF16)], axis=1)
    kk = _dot(kin, wk_ref[...])
    k_ref[0] = (kk[:, :hw] * cos8 + kk[:, hw:] * sin8).astype(BF16)
    v = _dot(kvn, wv_ref[...]) + vone_ref[...]
    for h in range(nh):
        lanes = slice(h * HEAD_PAD, (h + 1) * HEAD_PAD)
        qt_ref[0, h] = q[:, lanes].T.astype(BF16)
        vt_ref[0, h] = v[:, lanes].T[:VT_ROWS].astype(BF16)


def _mla_prep(p3, cos_t, sin_t, g_q, g_kv, wq, wk, wv, vone, *, tm):
    b, t, _ = p3.shape
    hw = N_HEADS_MLA * HEAD_PAD
    const = lambda shape: pl.BlockSpec(shape, lambda bi, i: (0,) * len(shape))
    return pl.pallas_call(
        _mla_prep_kernel,
        out_shape=(
            jax.ShapeDtypeStruct((b, N_HEADS_MLA, HEAD_PAD, t), BF16),
            jax.ShapeDtypeStruct((b, t, hw), BF16),
            jax.ShapeDtypeStruct((b, N_HEADS_MLA, VT_ROWS, t), BF16),
        ),
        grid=(b, t // tm),
        in_specs=[
            pl.BlockSpec((1, tm, 512), lambda bi, i: (bi, i, PB_Q)),
            pl.BlockSpec((1, tm, 512), lambda bi, i: (bi, i, PB_KV)),
            pl.BlockSpec((tm, HEAD_PAD), lambda bi, i: (i, 0)),
            pl.BlockSpec((tm, HEAD_PAD), lambda bi, i: (i, 0)),
            const((1, Q_LORA)),
            const((1, KV_LORA)),
            const(wq.shape),
            const(wk.shape),
            const(wv.shape),
            const((1, hw)),
        ],
        out_specs=(
            pl.BlockSpec((1, N_HEADS_MLA, HEAD_PAD, tm), lambda bi, i: (bi, 0, 0, i)),
            pl.BlockSpec((1, tm, hw), lambda bi, i: (bi, i, 0)),
            pl.BlockSpec((1, N_HEADS_MLA, VT_ROWS, tm), lambda bi, i: (bi, 0, 0, i)),
        ),
        compiler_params=_cparams(("parallel", "parallel")),
        name="mla_prep",
    )(p3, p3, cos_t, sin_t, g_q, g_kv, wq, wk, wv, vone)


def _attn_kernel(qt_ref, k_ref, vt_ref, *rest, nk):
    o_ref, m_ref, acc_ref = rest[-3:]
    j = pl.program_id(2)

    @pl.when(j == 0)
    def _():
        m_ref[...] = jnp.full(m_ref.shape, -jnp.inf, F32)
        acc_ref[...] = jnp.zeros(acc_ref.shape, F32)

    nh = N_HEADS_MLA
    mt = MXU_TILE
    nqb = qt_ref.shape[3] // mt

    def qk_tiles(h):
        out = {}
        k_h = k_ref[0, :, h * HEAD_PAD:(h + 1) * HEAD_PAD]

        def make(n):
            def run():
                out[n] = _dot(k_h, qt_ref[0, h, :, n * mt:(n + 1) * mt])
            return run
        return out, [make(n) for n in range(nqb)]

    def softmax(h, s):
        p, alpha = {}, {}
        for n in range(nqb):
            lanes = slice(n * mt, (n + 1) * mt)
            m_prev = m_ref[h, :, lanes]
            m_new = jnp.maximum(m_prev, jnp.max(s[n], axis=0, keepdims=True))
            alpha[n] = jnp.exp2(m_prev - m_new)[0:1, :]
            p[n] = jnp.exp2(s[n] - m_new[0:1, :]).astype(BF16)
            m_ref[h, :, lanes] = m_new
        return p, alpha

    def pv_tiles(h, p, alpha):
        vt_h = vt_ref[0, h]

        def make(n):
            def run():
                lanes = slice(n * mt, (n + 1) * mt)
                acc_ref[h, :, lanes] = alpha[n] * acc_ref[h, :, lanes] + _dot(vt_h, p[n])
            return run
        return [make(n) for n in range(nqb)]

    def run_interleaved(a_ops, c_ops):
        for i in range(max(len(a_ops), len(c_ops))):
            if i < len(a_ops):
                a_ops[i]()
            if i < len(c_ops):
                c_ops[i]()

    s, pa = {}, {}
    for h in range(min(2, nh)):
        s[h], ops = qk_tiles(h)
        run_interleaved(ops, [])
    pa[0] = softmax(0, s[0])
    for h in range(nh):
        if h + 1 < nh:
            pa[h + 1] = softmax(h + 1, s[h + 1])
        a_ops = []
        if h + 2 < nh:
            s[h + 2], a_ops = qk_tiles(h + 2)
        run_interleaved(a_ops, pv_tiles(h, *pa[h]))

    @pl.when(j == nk - 1)
    def _():
        outs = []
        for h in range(N_HEADS_MLA):
            a = acc_ref[h]
            outs.append(a[:V_HEAD] * (1.0 / a[V_HEAD:V_HEAD + 1]))
        o_ref[0] = jnp.concatenate(outs, axis=0).T.astype(o_ref.dtype)


def _attention(qt, k, vt, y_prev, *, tq, tk, nq, nk, q_off, k_off):
    b, t, hw = k.shape
    in_specs = [
        pl.BlockSpec((1, N_HEADS_MLA, HEAD_PAD, tq), lambda bi, i, j: (bi, 0, 0, i + q_off)),
        pl.BlockSpec((1, tk, hw), lambda bi, i, j: (bi, j + k_off, 0)),
        pl.BlockSpec((1, N_HEADS_MLA, VT_ROWS, tk), lambda bi, i, j: (bi, 0, 0, j + k_off)),
    ]
    args = [qt, k, vt]
    aliases = {}
    if y_prev is not None:
        in_specs.append(pl.BlockSpec(memory_space=pl.ANY))
        args.append(y_prev)
        aliases = {3: 0}
    return pl.pallas_call(
        functools.partial(_attn_kernel, nk=nk),
        out_shape=jax.ShapeDtypeStruct((b, t, W_MLA), ACT),
        grid=(b, nq, nk),
        in_specs=in_specs,
        out_specs=pl.BlockSpec((1, tq, W_MLA), lambda bi, i, j: (bi, i + q_off, 0)),
        scratch_shapes=[
            pltpu.VMEM((N_HEADS_MLA, 8, tq), F32),
            pltpu.VMEM((N_HEADS_MLA, VT_ROWS, tq), F32),
        ],
        input_output_aliases=aliases,
        compiler_params=_cparams(("parallel", "parallel", "arbitrary")),
        name="attn_ctx" if y_prev is not None else "attn_x",
    )(*args)


def _half_lane_masks():
    lane = lax.broadcasted_iota(jnp.int32, (CHUNK, LANES), 1)
    first = lane < RWKV_HEAD
    return first, jnp.where(first, 1.0, 0.0).astype(BF16), jnp.where(first, 0.0, 1.0).astype(BF16)


def _bd4(x, lo, hi):
    xb = x.astype(BF16)
    xl, xr = xb[:, :LANES], xb[:, LANES:]
    z = jnp.zeros((2 * CHUNK, LANES), BF16)
    c0 = jnp.concatenate([xl * lo, xl * hi, z], axis=0)
    c1 = jnp.concatenate([z, xr * lo, xr * hi], axis=0)
    return jnp.concatenate([c0, c1], axis=1)


def _diag_blocks(full, first):
    c = CHUNK
    left = jnp.where(first, full[0:c, :LANES], full[c:2 * c, :LANES])
    right = jnp.where(first, full[2 * c:3 * c, LANES:], full[3 * c:4 * c, LANES:])
    return jnp.concatenate([left, right], axis=1)


def _rwkv_prep_kernel(rkv_ref, lora_ref, w0_ref, a0_ref, wup_ref, aup_ref, kk_ref, ka_ref, rk_ref,
                      ones_ref, rh_ref, y0_ref, g_ref, h_ref, bonus_ref):
    fwd = pl.program_id(0) == 0
    c = CHUNK
    w = W_RWKV
    grp = PREP_GROUP
    ones4 = ones_ref[...]
    sgn = jnp.where(fwd, 1, -1)
    ti = lax.broadcasted_iota(jnp.int32, (grp, grp), 0)
    si = lax.broadcasted_iota(jnp.int32, (grp, grp), 1)
    same = jnp.where((ti // c) == (si // c), 1.0, 0.0)
    tri = jnp.where((si - ti) * sgn <= 0, same, 0.0).astype(BF16)
    first, lo, hi = _half_lane_masks()
    tq = lax.broadcasted_iota(jnp.int32, (c, QUAD), 0)
    sq = lax.broadcasted_iota(jnp.int32, (c, QUAD), 1) % c
    before = (sq - tq) * sgn < 0
    upto = (sq - tq) * sgn <= 0
    eye = jnp.where(sq == tq, 1.0, 0.0)

    def head_sum(x):
        xh, xl = _split2(x)
        parts = []
        for qd in range(w // QUAD):
            ls = slice(qd * QUAD, (qd + 1) * QUAD)
            parts.append(_dot(xh[:, ls], ones4) + _dot(xl[:, ls], ones4))
        return jnp.concatenate(parts, axis=1)

    def pm(x, y):
        return _dot(x.astype(BF16), _bd4(y, lo, hi))

    def prologue(g, out):
        gs = slice(g * grp, (g + 1) * grp)
        rkv = rkv_ref[gs, :].astype(F32)
        r, k, v = rkv[:, :w], rkv[:, w:2 * w], rkv[:, 2 * w:]
        lora = lora_ref[gs, :].astype(F32)
        zw = w0_ref[0] + _dot(jnp.tanh(lora[:, :W_LORA]).astype(BF16), wup_ref[0])
        za = a0_ref[0] + _dot(lora[:, W_LORA:].astype(BF16), aup_ref[0])
        yield
        ell = -math.exp(-0.5) * _sigmoid(zw)
        a = _sigmoid(za)
        kkr = k * kk_ref[...]
        k_d = k * (1.0 + (a - 1.0) * ka_ref[...])
        kk_ss = head_sum(kkr * kkr)
        rk_s = head_sum(r * k_d * rk_ref[...])
        yield
        kk = kkr * lax.rsqrt(kk_ss + L2_EPS)
        bonus_ref[0, gs, :] = (rk_s * v).astype(bonus_ref.dtype)
        ell_hi, ell_lo = _split2(ell)
        lc = _dot(tri, ell_hi) + _dot(tri, ell_lo)
        yield
        ltot = jnp.concatenate(
            [jnp.broadcast_to(jnp.where(fwd, lc[ch * c + c - 1:ch * c + c], lc[ch * c:ch * c + 1]), (c, w))
             for ch in range(grp // c)], axis=0)
        e_neg = jnp.exp(-lc)
        e_tail = jnp.exp(ltot - lc)
        kka = kk * a
        out.update(abar=kk * jnp.exp(lc - ell), bbar=kka * e_neg, kbar=k_d * e_neg, rbar=r * jnp.exp(lc),
                   btil=kka * e_tail, ktil=k_d * e_tail, v=v, gam=jnp.exp(ltot))

    def chains(g, q):
        sl = [(slice(ch * c, (ch + 1) * c), slice(qd * QUAD, (qd + 1) * QUAD))
              for ch in range(grp // c) for qd in range(w // QUAD)]
        abar, bbar, kbar, rbar, btil, ktil, v, gam = (
            q[nm] for nm in ("abar", "bbar", "kbar", "rbar", "btil", "ktil", "v", "gam"))
        la = [jnp.concatenate([abar[s], rbar[s]], axis=0).astype(BF16) for s in sl]
        nb = [_dot_nt(la_i, _bd4(bbar[s], lo, hi)) for la_i, s in zip(la, sl)]
        yield
        nk = [_dot_nt(la_i, _bd4(kbar[s], lo, hi)) for la_i, s in zip(la, sl)]
        yield
        n = [jnp.where(before, t[:c], 0.0) for t in nb]
        a_rb = [jnp.where(upto, t[c:], 0.0) for t in nb]
        a_ak = [jnp.where(before, t[:c], 0.0) for t in nk]
        a_rk = [jnp.where(upto, t[c:], 0.0) for t in nk]
        x = [eye - t for t in n]
        p = [pm(t, t) for t in n]
        yield
        py = [pm(jnp.concatenate([u, r_], axis=0), v[s]) for u, r_, s in zip(a_ak, a_rk, sl)]
        yield
        for it in range(5):
            if it < 4:
                xp = [pm(jnp.concatenate([x_i, p_i], axis=0), p_i) for x_i, p_i in zip(x, p)]
                x = [x_i + t[:c] for x_i, t in zip(x, xp)]
                p = [t[c:] for t in xp]
            else:
                x = [x_i + pm(x_i, p_i) for x_i, p_i in zip(x, p)]
            yield
        ah = [pm(x_i, abar[s]) for x_i, s in zip(x, sl)]
        u0 = [pm(x_i, t[:c]) for x_i, t in zip(x, py)]
        yield
        ra = [pm(m_i, t) for m_i, t in zip(a_rb, ah)]
        ru = [pm(m_i, t) for m_i, t in zip(a_rb, u0)]
        gfull = [_dot_tn(btil[s].astype(BF16), t.astype(BF16)) for s, t in zip(sl, ah)]
        hfull = [_dot_tn(jnp.concatenate([ktil[s], -btil[s]], axis=0).astype(BF16),
                         jnp.concatenate([v[s], t], axis=0).astype(BF16)) for s, t in zip(sl, u0)]
        for i, (rs, ls) in enumerate(sl):
            ro = slice(g * grp + rs.start, g * grp + rs.stop)
            rh_ref[0, ro, ls] = (rbar[rs, ls] - ra[i]).astype(BF16)
            y0_ref[0, ro, ls] = py[i][c:] - ru[i]
            g_ref[0, ro, ls] = (eye * gam[rs, ls] - _diag_blocks(gfull[i], first)).astype(g_ref.dtype)
            h_ref[0, ro, ls] = _diag_blocks(hfull[i], first)

    ngroups = rkv_ref.shape[0] // grp
    vals = [dict() for _ in range(ngroups)]
    for _ in prologue(0, vals[0]):
        pass
    for g in range(ngroups):
        nxt = prologue(g + 1, vals[g + 1]) if g + 1 < ngroups else iter(())
        for stage, _ in enumerate(chains(g, vals[g])):
            if stage >= 1 and stage % 2 == 1:
                next(nxt, None)
        for _ in nxt:
            pass


def _rwkv_prep(p2, w0, a0, w_up, a_up, k_k, k_a, r_k, ones4):
    n, _ = p2.shape
    w = W_RWKV
    rows = PREP_ROWS if n % PREP_ROWS == 0 else PREP_GROUP
    const = lambda shape: pl.BlockSpec(shape, lambda d, i: (0,) * len(shape))
    perdir = lambda shape: pl.BlockSpec((1,) + shape, lambda d, i: (d,) + (0,) * len(shape))
    out_spec = pl.BlockSpec((1, rows, w), lambda d, i: (d, i, 0))
    f32_out = jax.ShapeDtypeStruct((2, n, w), F32)
    act_out = jax.ShapeDtypeStruct((2, n, w), ACT)
    return pl.pallas_call(
        _rwkv_prep_kernel,
        out_shape=(act_out, f32_out, act_out, f32_out, act_out),
        grid=(2, n // rows),
        in_specs=[
            pl.BlockSpec((rows, 3 * w), lambda d, i: (i, PB_RKV)),
            pl.BlockSpec((rows, 2 * W_LORA), lambda d, i: (i, PB_LORA128 + d)),
            perdir((1, w)), perdir((1, w)), perdir((W_LORA, w)), perdir((A_LORA, w)),
            const((1, w)), const((1, w)), const((1, w)),
            const((QUAD, QUAD)),
        ],
        out_specs=(out_spec,) * 5,
        compiler_params=_cparams(("parallel", "parallel")),
        name="rwkv_prep",
    )(p2, p2, w0, a0, w_up, a_up, k_k, k_a, r_k, ones4)


def _rwkv_scan_kernel(rhf, y0f, gf, hf, rhb, y0b, gb, hb, yf_ref, yb_ref, st_ref, *, nb):
    @pl.when(pl.program_id(0) == 0)
    def _():
        st_ref[...] = jnp.zeros(st_ref.shape, F32)

    _, lo, hi = _half_lane_masks()
    c = CHUNK
    nsub = rhf.shape[2] // c
    for step in range(nsub):
        for d, sub, (rh, y0, g, hh, y_ref) in ((0, step, (rhf, y0f, gf, hf, yf_ref)),
                                               (1, nsub - 1 - step, (rhb, y0b, gb, hb, yb_ref))):
            rs = slice(sub * c, (sub + 1) * c)
            for bi in range(nb):
                for qd in range(W_RWKV // QUAD):
                    ls = slice(qd * QUAD, (qd + 1) * QUAD)
                    lhs = jnp.concatenate([rh[0, bi, rs, ls], g[0, bi, rs, ls]], axis=0)
                    res = _dot(lhs, _bd4(st_ref[d, bi, qd], lo, hi))
                    y_ref[bi, rs, ls] = (res[:c] + y0[0, bi, rs, ls]).astype(y_ref.dtype)
                    st_ref[d, bi, qd] = res[c:] + hh[0, bi, rs, ls]


def _rwkv_scan(rh, y0, g, h, *, n_x, n_ctx):
    _, b, t, w = rh.shape
    c = ROW_BLK
    n = n_x + n_ctx
    fidx = lambda j: jnp.where(j < n_ctx, n_x + j, j - n_ctx)
    bidx = lambda j: n - 1 - j
    fspec = pl.BlockSpec((1, b, c, w), lambda j: (0, 0, fidx(j), 0))
    bspec = pl.BlockSpec((1, b, c, w), lambda j: (1, 0, bidx(j), 0))
    y_shape = jax.ShapeDtypeStruct((b, t, w), ACT)
    return pl.pallas_call(
        functools.partial(_rwkv_scan_kernel, nb=b),
        out_shape=(y_shape, y_shape),
        grid=(n,),
        in_specs=[fspec] * 4 + [bspec] * 4,
        out_specs=(
            pl.BlockSpec((b, c, w), lambda j: (0, fidx(j), 0)),
            pl.BlockSpec((b, c, w), lambda j: (0, bidx(j), 0)),
        ),
        scratch_shapes=[pltpu.VMEM((2, b, w // QUAD, RWKV_HEAD, QUAD), F32)],
        compiler_params=_cparams(("arbitrary",)),
        name="rwkv_scan",
    )(rh, y0, g, h, rh, y0, g, h)


def _merge_kernel(z_ref, gate_x_ref, gate_c_ref, gl_ref, grw_ref, gml_ref, cin_ref, cb_ref, cc_ref, gcv_ref,
                  cin_p, cc_p, cin_n, cc_n, ymla_ref, yf_ref, yb_ref, bon_ref,
                  wbm_ref, wbc_ref, wbr_ref, wout_ref, gpost_ref, cw_ref, cbias_ref, gng_ref, gnb_ref,
                  avg_ref, o_ref, *, tiles_per_batch, seq, ctx_len):
    tm = z_ref.shape[0]
    d = D_MODEL
    pos = (pl.program_id(0) % tiles_per_batch) * tm + lax.broadcasted_iota(jnp.int32, (tm, 1), 0)
    has_prev = jnp.logical_and(pos != 0, pos != seq)
    has_next = jnp.logical_and(pos != seq - 1, pos != seq + ctx_len - 1)

    def f32(ref, idx=slice(None)):
        return ref[idx].astype(F32)

    u = f32(cc_ref) * f32(cin_ref)
    hl = HALO_ROWS - 1
    u_halo_p = f32(cc_p, slice(hl, hl + 1)) * f32(cin_p, slice(hl, hl + 1))
    u_halo_n = f32(cc_n, slice(0, 1)) * f32(cin_n, slice(0, 1))
    row = lax.broadcasted_iota(jnp.int32, u.shape, 0)
    u_prev = jnp.where(has_prev, jnp.where(row == 0, u_halo_p, pltpu.roll(u, 1, axis=0)), 0.0)
    u_next = jnp.where(has_next, jnp.where(row == tm - 1, u_halo_n, pltpu.roll(u, tm - 1, axis=0)), 0.0)
    cw = cw_ref[...]
    y_conv = f32(cb_ref) * (u_prev * cw[0:1] + u * cw[1:2] + u_next * cw[2:3] + cbias_ref[...])

    avg = avg_ref[...]

    def head_mean(x):
        hi, lo = _split2(x)
        parts = []
        for qd in range(W_RWKV // QUAD):
            ls = slice(qd * QUAD, (qd + 1) * QUAD)
            parts.append(_dot(hi[:, ls], avg) + _dot(lo[:, ls], avg))
        return jnp.concatenate(parts, axis=1)

    yr = f32(yf_ref) + f32(yb_ref)
    mu = head_mean(yr)
    dv = yr - mu
    var = head_mean(dv * dv)
    y_rwkv = dv * lax.rsqrt(var + GN_EPS) * gng_ref[...] + gnb_ref[...] + f32(bon_ref, 0) + f32(bon_ref, 1)

    br_mla = _dot((f32(ymla_ref) * _silu(f32(gml_ref))).astype(BF16), wbm_ref[...])
    br_conv = _dot((y_conv * _silu(f32(gcv_ref))).astype(BF16), wbc_ref[...])
    br_rwkv = _dot((y_rwkv * _silu(f32(grw_ref))).astype(BF16), wbr_ref[...])
    s = _sigmoid(f32(gl_ref))
    merged = s[:, :d] * br_mla + s[:, d:2 * d] * br_conv + s[:, 2 * d:] * br_rwkv
    o = _dot(merged.astype(BF16), wout_ref[...])
    gate = jnp.where(pos >= seq, gate_c_ref[0], gate_x_ref[0])
    o_ref[...] = z_ref[...] + gate * _rms(o, gpost_ref[...])


def _merge(z2, mods, p2, y_mla, yf, yb, bonus, wbm, wbc, wbr, wout, g_post, conv_w, conv_b,
           gn_g, gn_b, avg_bd, *, seq, ctx_len, tm, x_only):
    n, d = z2.shape
    t = seq + ctx_len
    assert t % tm == 0 and tm % HALO_ROWS == 0 and (not x_only or seq % tm == 0)
    hb = tm // HALO_ROWS
    nhb = n // HALO_ROWS
    tiles_per_batch = t // tm
    x_tiles = seq // tm
    n_tiles = (n // t) * x_tiles if x_only else n // tm
    gi = (lambda i: i // x_tiles * tiles_per_batch + i % x_tiles) if x_only else (lambda i: i)
    pcol = lambda blk: pl.BlockSpec((tm, 512), lambda i: (gi(i), blk))
    prev = lambda blk: pl.BlockSpec((HALO_ROWS, 512), lambda i: (jnp.maximum(gi(i) * hb - 1, 0), blk))
    nxt = lambda blk: pl.BlockSpec((HALO_ROWS, 512), lambda i: (jnp.minimum((gi(i) + 1) * hb, nhb - 1), blk))
    const = lambda shape: pl.BlockSpec(shape, lambda i: (0,) * len(shape))
    row512 = pl.BlockSpec((tm, 512), lambda i: (gi(i), 0))
    return pl.pallas_call(
        functools.partial(_merge_kernel, tiles_per_batch=x_tiles if x_only else tiles_per_batch,
                          seq=seq, ctx_len=ctx_len),
        out_shape=jax.ShapeDtypeStruct((n_tiles * tm, d), F32),
        grid=(n_tiles,),
        in_specs=[
            pl.BlockSpec((tm, d), lambda i: (gi(i), 0)),
            pl.BlockSpec((1, 1, d), lambda i: (1 + gi(i) // tiles_per_batch, 0, 2)),
            pl.BlockSpec((1, 1, d), lambda i: (0, 0, 2)),
            pl.BlockSpec((tm, 3 * d), lambda i: (gi(i), PB_GATE)),
            pcol(PB_GRWKV), pcol(PB_GMLA), pcol(PB_CVIN), pcol(PB_CVB), pcol(PB_CVC), pcol(PB_GCONV),
            prev(PB_CVIN), prev(PB_CVC), nxt(PB_CVIN), nxt(PB_CVC),
            row512, row512, row512,
            pl.BlockSpec((2, tm, 512), lambda i: (0, gi(i), 0)),
            const(wbm.shape), const(wbc.shape), const(wbr.shape), const(wout.shape),
            const((1, d)), const(conv_w.shape), const((1, 512)), const((1, 512)), const((1, 512)),
            const(avg_bd.shape),
        ],
        out_specs=pl.BlockSpec((tm, d), lambda i: (i, 0)),
        compiler_params=_cparams(("parallel",)),
        name="merge",
    )(z2, mods, mods, p2, p2, p2, p2, p2, p2, p2, p2, p2, p2, p2, y_mla, yf, yb, bonus,
      wbm, wbc, wbr, wout, g_post, conv_w, conv_b, gn_g, gn_b, avg_bd)


def _pair_swap(w):
    s = w.shape
    return w.reshape(s[:-1] + (s[-1] // 2, 2))[..., ::-1].reshape(s)


def _layout_w_in(w_in):
    sizes = (Q_LORA, KV_LORA, QK_ROPE, W_MLA, CONV_W, CONV_W, CONV_W, CONV_W, W_RWKV, W_RWKV, W_RWKV,
             W_LORA, W_LORA, A_LORA, A_LORA, W_RWKV, 3 * D_MODEL)
    offs = np.concatenate([[0], np.cumsum(sizes)])
    names = ("q_lat", "kv_lat", "kr", "g_mla", "cv_in", "cv_b", "cv_c", "g_conv", "r", "k", "v",
             "wd_f", "wd_b", "ad_f", "ad_b", "g_rwkv", "gl")
    col = {nm: w_in[..., offs[i]:offs[i + 1]] for i, nm in enumerate(names)}
    zeros = jnp.zeros(w_in.shape[:-1] + (512 - Q_LORA - 2 * QK_ROPE,), w_in.dtype)
    parts = [col["gl"], col["r"], col["k"], col["v"], col["g_rwkv"], col["g_mla"], col["cv_in"],
             col["cv_b"], col["cv_c"], col["g_conv"],
             col["q_lat"], col["kr"], _pair_swap(col["kr"]), zeros,
             col["kv_lat"], col["wd_f"], col["ad_f"], col["wd_b"], col["ad_b"]]
    out = jnp.concatenate(parts, axis=-1).astype(BF16)
    assert out.shape[-1] == PCOLS
    return out


def _layout_mla_weights(w_uq, w_ukv):
    depth = w_uq.shape[0]
    nh = N_HEADS_MLA
    wq = w_uq.reshape(depth, Q_LORA, nh, QK_HEAD)
    q_nope, q_rope = wq[..., :QK_NOPE], wq[..., QK_NOPE:]
    zq = jnp.zeros((depth, Q_LORA, nh, HEAD_PAD - QK_HEAD), w_uq.dtype)
    wq_a = jnp.concatenate([q_nope, q_rope, zq], axis=-1)
    wq_b = jnp.concatenate([jnp.zeros_like(q_nope), _pair_swap(q_rope), zq], axis=-1)
    wq_all = jnp.concatenate([wq_a.reshape(depth, Q_LORA, nh * HEAD_PAD),
                              wq_b.reshape(depth, Q_LORA, nh * HEAD_PAD)], axis=-1).astype(BF16)

    wkv = w_ukv.reshape(depth, KV_LORA, nh, QK_NOPE + V_HEAD)
    k_nope, v_w = wkv[..., :QK_NOPE], wkv[..., QK_NOPE:]
    zk = jnp.zeros((depth, KV_LORA, nh, HEAD_PAD - QK_NOPE), w_ukv.dtype)
    wk_top = jnp.concatenate([k_nope, zk], axis=-1).reshape(depth, KV_LORA, nh * HEAD_PAD)
    wv = jnp.concatenate([v_w, zk], axis=-1).reshape(depth, KV_LORA, nh * HEAD_PAD).astype(BF16)
    place = np.zeros((HEAD_PAD, HEAD_PAD), np.float32)
    place[np.arange(QK_ROPE), QK_NOPE + np.arange(QK_ROPE)] = 1.0
    e_a = np.tile(place, (1, nh))
    place_b = np.zeros((HEAD_PAD, HEAD_PAD), np.float32)
    place_b[QK_ROPE + np.arange(QK_ROPE), QK_NOPE + np.arange(QK_ROPE)] = 1.0
    e_b = np.tile(place_b, (1, nh))
    top = jnp.concatenate([wk_top, jnp.zeros_like(wk_top)], axis=-1)
    bot = jnp.broadcast_to(jnp.asarray(np.concatenate([e_a, e_b], axis=1)), (depth, HEAD_PAD, 2 * nh * HEAD_PAD))
    wk_all = jnp.concatenate([top, bot.astype(top.dtype)], axis=1).astype(BF16)
    return wq_all, wk_all, wv


def _rope_tables(seq, ctx_len):
    n_freq = QK_ROPE // 4
    pos = np.arange(seq)
    inv = ROPE_THETA ** (-np.arange(n_freq, dtype=np.float32) / n_freq)
    row = (pos // GRID_W).astype(np.float32)
    colp = (pos % GRID_W).astype(np.float32)
    ang = jnp.concatenate([jnp.asarray(row)[:, None] * jnp.asarray(inv), jnp.asarray(colp)[:, None] * jnp.asarray(inv)], axis=-1)
    cos, sin = jnp.cos(ang), jnp.sin(ang)
    cos2 = jnp.repeat(cos, 2, axis=-1)
    sin2 = jnp.stack([-sin, sin], axis=-1).reshape(seq, QK_ROPE)
    ones = jnp.ones((seq, QK_NOPE), F32)
    pad = jnp.zeros((seq, HEAD_PAD - QK_HEAD), F32)
    cos_x = jnp.concatenate([ones, cos2, pad], axis=-1)
    sin_x = jnp.concatenate([jnp.zeros_like(ones), sin2, pad], axis=-1)
    cos_c = jnp.concatenate([jnp.ones((ctx_len, QK_HEAD), F32), jnp.zeros((ctx_len, HEAD_PAD - QK_HEAD), F32)], axis=-1)
    sin_c = jnp.zeros((ctx_len, HEAD_PAD), F32)
    return jnp.concatenate([cos_x, cos_c], axis=0), jnp.concatenate([sin_x, sin_c], axis=0)


def _block_diag_const(n, blk, value):
    i = np.arange(n)
    return np.where((i[:, None] // blk) == (i[None, :] // blk), value, 0.0).astype(np.float32)


def _pick(n, candidates):
    for cand in candidates:
        if n % cand == 0:
            return cand
    raise ValueError(f"no tile for {n}")


def kernel(x, c, ctx, c_ctx, w_mod, b_mod, g_pre, g_post, w_in, g_q, g_kv, w_uq, w_ukv, conv_w, conv_b,
           w0, w_up, a0, a_up, k_k, k_a, r_k, gn_g, gn_b, w_br_mla, w_br_conv, w_br_rwkv, w_out):
    bsz, seq, d = x.shape
    ctx_len = ctx.shape[1]
    depth = w_mod.shape[0]
    assert d == D_MODEL and ctx_len == ROW_BLK and seq % ROW_BLK == 0
    t = seq + ctx_len
    n = bsz * t
    tiles_per_batch = t // ROW_BLK

    cc = jnp.zeros((8, d), F32).at[0].set(c_ctx).at[1:1 + bsz].set(c)
    mods = _adaln(cc, w_mod, b_mod).reshape(depth, 8, 1, 3 * d)

    w_in_p = _layout_w_in(w_in)
    wq_all, wk_all, wv_all = _layout_mla_weights(w_uq, w_ukv)
    cos_t, sin_t = _rope_tables(seq, ctx_len)
    vone = np.zeros((1, N_HEADS_MLA * HEAD_PAD), np.float32)
    vone[0, V_HEAD::HEAD_PAD] = 1.0
    vone = jnp.asarray(vone)
    ones4 = jnp.asarray(_block_diag_const(QUAD, RWKV_HEAD, 1.0), BF16)
    avg_bd = jnp.asarray(_block_diag_const(QUAD, RWKV_HEAD, 1.0 / RWKV_HEAD), BF16)

    row_tile = _pick(t, (768, 512, 256))
    tq = _pick(seq, (2048, 1024, 512, 256))
    tk = _pick(t, (768, 512, 256))

    z = jnp.concatenate([x, ctx], axis=1)
    for l in range(depth):
        z2 = z.reshape(n, d)
        p2 = _proj_in(z2, mods[l], g_pre[l][None], w_in_p, l, tiles_per_batch=tiles_per_batch)
        p3 = p2.reshape(bsz, t, PCOLS)

        qt, k, vt = _mla_prep(p3, cos_t, sin_t, g_q[l][None], g_kv[l][None], wq_all[l], wk_all[l], wv_all[l], vone,
                              tm=row_tile)
        y_mla = _attention(qt, k, vt, None, tq=tq, tk=tk, nq=seq // tq, nk=t // tk, q_off=0, k_off=0)
        y_mla = _attention(qt, k, vt, y_mla, tq=ctx_len, tk=ctx_len, nq=1, nk=1,
                           q_off=seq // ctx_len, k_off=seq // ctx_len)

        rh, y0, g, h, bonus = _rwkv_prep(
            p2, w0[l][:, None], a0[l][:, None], w_up[l].astype(BF16), a_up[l].astype(BF16),
            k_k[l][None], k_a[l][None], r_k[l].reshape(1, W_RWKV), ones4)
        per_batch = lambda arr: arr.reshape(2, bsz, t, W_RWKV)
        yf, yb = _rwkv_scan(per_batch(rh), per_batch(y0), per_batch(g), per_batch(h),
                            n_x=seq // ROW_BLK, n_ctx=ctx_len // ROW_BLK)

        z2 = _merge(z2, mods[l], p2, y_mla.reshape(n, W_MLA), yf.reshape(n, W_RWKV), yb.reshape(n, W_RWKV),
                    bonus,
                    w_br_mla[l].astype(BF16), w_br_conv[l].astype(BF16), w_br_rwkv[l].astype(BF16),
                    w_out[l].astype(BF16), g_post[l][None], conv_w[l], conv_b[l][None],
                    gn_g[l][None], gn_b[l][None], avg_bd,
                    seq=seq, ctx_len=ctx_len, x_only=l == depth - 1,
                    tm=ROW_BLK if l == depth - 1 else row_tile)
        if l < depth - 1:
            z = z2.reshape(bsz, t, d)
    return z2.reshape(bsz, seq, d)
```

```python
import functools
import math

import numpy as np
import jax
import jax.numpy as jnp
from jax import lax
from jax.experimental import pallas as pl
from jax.experimental.pallas import tpu as pltpu

F32 = jnp.float32
BF16 = jnp.bfloat16
ACT = BF16

D_MODEL = 1024
GRID_W = 64
N_HEADS_MLA = 8
Q_LORA = 384
KV_LORA = 256
QK_NOPE = 64
QK_ROPE = 32
QK_HEAD = QK_NOPE + QK_ROPE
V_HEAD = 64
W_MLA = N_HEADS_MLA * V_HEAD
ROPE_THETA = 10000.0
ATTN_SCALE = QK_HEAD ** -0.5
CONV_W = 512
RWKV_HEADS = 8
RWKV_HEAD = 64
W_RWKV = RWKV_HEADS * RWKV_HEAD
W_LORA = 64
A_LORA = 64
RMS_EPS = 1e-6
GN_EPS = 64e-5
L2_EPS = 1e-12
LOG2E = math.log2(math.e)

LANES = 128
ROW_BLK = 256
CHUNK = 64
QUAD = 4 * RWKV_HEAD
PREP_GROUP = 256
PREP_ROWS = 1024
HEAD_PAD = 128
MXU_TILE = 256
HALO_ROWS = 16
VT_ROWS = 80
VMEM_LIMIT = 48 * 1024 * 1024

PCOLS = 17 * 512
PB_GATE = 0
PB_RKV = 2
PB_GRWKV = 9
PB_GMLA = 10
PB_CVIN = 11
PB_CVB = 12
PB_CVC = 13
PB_GCONV = 14
PB_Q = 15
PB_KV = 16
PB_LORA128 = (16 * 512 + 256) // 128


def _cparams(sem, vmem=VMEM_LIMIT):
    return pltpu.CompilerParams(dimension_semantics=sem, vmem_limit_bytes=vmem)


def _dot(a, b):
    return jnp.dot(a, b, preferred_element_type=F32)


def _dot_nt(a, b):
    return lax.dot_general(a, b, (((1,), (1,)), ((), ())), preferred_element_type=F32)


def _dot_tn(a, b):
    return lax.dot_general(a, b, (((0,), (0,)), ((), ())), preferred_element_type=F32)


def _split2(x):
    hi = x.astype(BF16)
    lo = (x - hi.astype(F32)).astype(BF16)
    return hi, lo


def _dot_hi(a, b):
    ah, al = _split2(a)
    bh, bl = _split2(b)
    return _dot(ah, bh) + _dot(ah, bl) + _dot(al, bh)


def _sigmoid(x):
    return 0.5 * jnp.tanh(0.5 * x) + 0.5


def _silu(x):
    return x * _sigmoid(x)


def _rms(x, g):
    return x * lax.rsqrt(jnp.mean(x * x, axis=-1, keepdims=True) + RMS_EPS) * g


def _adaln_kernel(c_ref, w_ref, b_ref, o_ref):
    a = _silu(c_ref[...])
    o_ref[0] = _dot_hi(a, w_ref[0]) + b_ref[0]


def _adaln(cc, w_mod, b_mod):
    depth, d, d3 = w_mod.shape
    tn = 1024
    return pl.pallas_call(
        _adaln_kernel,
        out_shape=jax.ShapeDtypeStruct((depth, 8, d3), F32),
        grid=(depth, d3 // tn),
        in_specs=[
            pl.BlockSpec((8, d), lambda l, j: (0, 0)),
            pl.BlockSpec((1, d, tn), lambda l, j: (l, 0, j)),
            pl.BlockSpec((1, 1, tn), lambda l, j: (l, 0, j)),
        ],
        out_specs=pl.BlockSpec((1, 8, tn), lambda l, j: (l, 0, j)),
        compiler_params=_cparams(("parallel", "parallel")),
        name="adaln",
    )(cc, w_mod, b_mod.reshape(depth, 1, d3))


def _seg_row(blk, tiles_per_batch):
    return jnp.where(blk % tiles_per_batch == tiles_per_batch - 1, 0, 1 + blk // tiles_per_batch)


def _proj_kernel(z_ref, *refs, nsub):
    mod_refs = refs[:2 * nsub]
    g_ref, w_ref, o_ref, h_ref = refs[2 * nsub:]

    @pl.when(pl.program_id(1) == 0)
    def _():
        y = _rms(z_ref[...], g_ref[...])
        for s in range(nsub):
            rows = slice(s * ROW_BLK, (s + 1) * ROW_BLK)
            shift, scale = mod_refs[2 * s][0], mod_refs[2 * s + 1][0]
            h_ref[rows, :] = (y[rows, :] * (1.0 + scale) + shift).astype(BF16)

    o_ref[...] = _dot(h_ref[...], w_ref[0]).astype(o_ref.dtype)


def _proj_in(z2, mods, g_pre, w_all, layer, *, tiles_per_batch):
    n, d = z2.shape
    tm = _pick(n, (1536, 1024, ROW_BLK))
    tn = PCOLS // 4
    nsub = tm // ROW_BLK
    mod_specs = []
    for s in range(nsub):
        for col in (0, 1):
            mod_specs.append(pl.BlockSpec(
                (1, 1, d), lambda i, j, s=s, col=col: (_seg_row(i * nsub + s, tiles_per_batch), 0, col)))
    return pl.pallas_call(
        functools.partial(_proj_kernel, nsub=nsub),
        out_shape=jax.ShapeDtypeStruct((n, PCOLS), ACT),
        grid=(n // tm, PCOLS // tn),
        in_specs=[pl.BlockSpec((tm, d), lambda i, j: (i, 0))] + mod_specs + [
            pl.BlockSpec((1, d), lambda i, j: (0, 0)),
            pl.BlockSpec((1, d, tn), lambda i, j: (layer, 0, j)),
        ],
        out_specs=pl.BlockSpec((tm, tn), lambda i, j: (i, j)),
        scratch_shapes=[pltpu.VMEM((tm, d), BF16)],
        compiler_params=_cparams(("parallel", "arbitrary")),
        name="proj_in",
    )(z2, *([mods] * (2 * nsub)), g_pre, w_all)


def _mla_prep_kernel(qb_ref, kb_ref, cos_ref, sin_ref, gq_ref, gkv_ref, wq_ref, wk_ref, wv_ref,
                     vone_ref, qt_ref, k_ref, vt_ref):
    nh = N_HEADS_MLA
    hw = nh * HEAD_PAD
    qb = qb_ref[0].astype(F32)
    kb = kb_ref[0].astype(F32)
    cos8 = jnp.tile(cos_ref[...], (1, nh))
    sin8 = jnp.tile(sin_ref[...], (1, nh))
    qn = _rms(qb[:, :Q_LORA], gq_ref[...]).astype(BF16)
    qq = _dot(qn, wq_ref[...])
    q = (qq[:, :hw] * cos8 + qq[:, hw:] * sin8) * (ATTN_SCALE * LOG2E)
    kvn = _rms(kb[:, :KV_LORA], gkv_ref[...]).astype(BF16)
    kin = jnp.concatenate([kvn, qb[:, Q_LORA:].astype(BF16)], axis=1)
    kk = _dot(kin, wk_ref[...])
    k_ref[0] = (kk[:, :hw] * cos8 + kk[:, hw:] * sin8).astype(BF16)
    v = _dot(kvn, wv_ref[...]) + vone_ref[...]
    for h in range(nh):
        lanes = slice(h * HEAD_PAD, (h + 1) * HEAD_PAD)
        qt_ref[0, h] = q[:, lanes].T.astype(BF16)
        vt_ref[0, h] = v[:, lanes].T[:VT_ROWS].astype(BF16)


def _mla_prep(p3, cos_t, sin_t, g_q, g_kv, wq, wk, wv, vone, *, tm):
    b, t, _ = p3.shape
    hw = N_HEADS_MLA * HEAD_PAD
    const = lambda shape: pl.BlockSpec(shape, lambda bi, i: (0,) * len(shape))
    return pl.pallas_call(
        _mla_prep_kernel,
        out_shape=(
            jax.ShapeDtypeStruct((b, N_HEADS_MLA, HEAD_PAD, t), BF16),
            jax.ShapeDtypeStruct((b, t, hw), BF16),
            jax.ShapeDtypeStruct((b, N_HEADS_MLA, VT_ROWS, t), BF16),
        ),
        grid=(b, t // tm),
        in_specs=[
            pl.BlockSpec((1, tm, 512), lambda bi, i: (bi, i, PB_Q)),
            pl.BlockSpec((1, tm, 512), lambda bi, i: (bi, i, PB_KV)),
            pl.BlockSpec((tm, HEAD_PAD), lambda bi, i: (i, 0)),
            pl.BlockSpec((tm, HEAD_PAD), lambda bi, i: (i, 0)),
            const((1, Q_LORA)),
            const((1, KV_LORA)),
            const(wq.shape),
            const(wk.shape),
            const(wv.shape),
            const((1, hw)),
        ],
        out_specs=(
            pl.BlockSpec((1, N_HEADS_MLA, HEAD_PAD, tm), lambda bi, i: (bi, 0, 0, i)),
            pl.BlockSpec((1, tm, hw), lambda bi, i: (bi, i, 0)),
            pl.BlockSpec((1, N_HEADS_MLA, VT_ROWS, tm), lambda bi, i: (bi, 0, 0, i)),
        ),
        compiler_params=_cparams(("parallel", "parallel")),
        name="mla_prep",
    )(p3, p3, cos_t, sin_t, g_q, g_kv, wq, wk, wv, vone)


def _attn_kernel(qt_ref, k_ref, vt_ref, *rest, nk):
    o_ref, m_ref, acc_ref = rest[-3:]
    j = pl.program_id(2)

    @pl.when(j == 0)
    def _():
        m_ref[...] = jnp.full(m_ref.shape, -jnp.inf, F32)
        acc_ref[...] = jnp.zeros(acc_ref.shape, F32)

    nh = N_HEADS_MLA
    mt = MXU_TILE
    nqb = qt_ref.shape[3] // mt

    def qk_tiles(h):
        out = {}
        k_h = k_ref[0, :, h * HEAD_PAD:(h + 1) * HEAD_PAD]

        def make(n):
            def run():
                out[n] = _dot(k_h, qt_ref[0, h, :, n * mt:(n + 1) * mt])
            return run
        return out, [make(n) for n in range(nqb)]

    def softmax(h, s):
        p, alpha = {}, {}
        for n in range(nqb):
            lanes = slice(n * mt, (n + 1) * mt)
            m_prev = m_ref[h, :, lanes]
            m_new = jnp.maximum(m_prev, jnp.max(s[n], axis=0, keepdims=True))
            alpha[n] = jnp.exp2(m_prev - m_new)[0:1, :]
            p[n] = jnp.exp2(s[n] - m_new[0:1, :]).astype(BF16)
            m_ref[h, :, lanes] = m_new
        return p, alpha

    def pv_tiles(h, p, alpha):
        vt_h = vt_ref[0, h]

        def make(n):
            def run():
                lanes = slice(n * mt, (n + 1) * mt)
                acc_ref[h, :, lanes] = alpha[n] * acc_ref[h, :, lanes] + _dot(vt_h, p[n])
            return run
        return [make(n) for n in range(nqb)]

    def run_interleaved(a_ops, c_ops):
        for i in range(max(len(a_ops), len(c_ops))):
            if i < len(a_ops):
                a_ops[i]()
            if i < len(c_ops):
                c_ops[i]()

    s, pa = {}, {}
    for h in range(min(2, nh)):
        s[h], ops = qk_tiles(h)
        run_interleaved(ops, [])
    pa[0] = softmax(0, s[0])
    for h in range(nh):
        if h + 1 < nh:
            pa[h + 1] = softmax(h + 1, s[h + 1])
        a_ops = []
        if h + 2 < nh:
            s[h + 2], a_ops = qk_tiles(h + 2)
        run_interleaved(a_ops, pv_tiles(h, *pa[h]))

    @pl.when(j == nk - 1)
    def _():
        outs = []
        for h in range(N_HEADS_MLA):
            a = acc_ref[h]
            outs.append(a[:V_HEAD] * (1.0 / a[V_HEAD:V_HEAD + 1]))
        o_ref[0] = jnp.concatenate(outs, axis=0).T.astype(o_ref.dtype)


def _attention(qt, k, vt, y_prev, *, tq, tk, nq, nk, q_off, k_off):
    b, t, hw = k.shape
    in_specs = [
        pl.BlockSpec((1, N_HEADS_MLA, HEAD_PAD, tq), lambda bi, i, j: (bi, 0, 0, i + q_off)),
        pl.BlockSpec((1, tk, hw), lambda bi, i, j: (bi, j + k_off, 0)),
        pl.BlockSpec((1, N_HEADS_MLA, VT_ROWS, tk), lambda bi, i, j: (bi, 0, 0, j + k_off)),
    ]
    args = [qt, k, vt]
    aliases = {}
    if y_prev is not None:
        in_specs.append(pl.BlockSpec(memory_space=pl.ANY))
        args.append(y_prev)
        aliases = {3: 0}
    return pl.pallas_call(
        functools.partial(_attn_kernel, nk=nk),
        out_shape=jax.ShapeDtypeStruct((b, t, W_MLA), ACT),
        grid=(b, nq, nk),
        in_specs=in_specs,
        out_specs=pl.BlockSpec((1, tq, W_MLA), lambda bi, i, j: (bi, i + q_off, 0)),
        scratch_shapes=[
            pltpu.VMEM((N_HEADS_MLA, 8, tq), F32),
            pltpu.VMEM((N_HEADS_MLA, VT_ROWS, tq), F32),
        ],
        input_output_aliases=aliases,
        compiler_params=_cparams(("parallel", "parallel", "arbitrary")),
        name="attn_ctx" if y_prev is not None else "attn_x",
    )(*args)


def _half_lane_masks():
    lane = lax.broadcasted_iota(jnp.int32, (CHUNK, LANES), 1)
    first = lane < RWKV_HEAD
    return first, jnp.where(first, 1.0, 0.0).astype(BF16), jnp.where(first, 0.0, 1.0).astype(BF16)


def _bd4(x, lo, hi):
    xb = x.astype(BF16)
    xl, xr = xb[:, :LANES], xb[:, LANES:]
    z = jnp.zeros((2 * CHUNK, LANES), BF16)
    c0 = jnp.concatenate([xl * lo, xl * hi, z], axis=0)
    c1 = jnp.concatenate([z, xr * lo, xr * hi], axis=0)
    return jnp.concatenate([c0, c1], axis=1)


def _diag_blocks(full, first):
    c = CHUNK
    left = jnp.where(first, full[0:c, :LANES], full[c:2 * c, :LANES])
    right = jnp.where(first, full[2 * c:3 * c, LANES:], full[3 * c:4 * c, LANES:])
    return jnp.concatenate([left, right], axis=1)


def _rwkv_prep_kernel(rkv_ref, lora_ref, w0_ref, a0_ref, wup_ref, aup_ref, kk_ref, ka_ref, rk_ref,
                      ones_ref, rh_ref, y0_ref, g_ref, h_ref, bonus_ref):
    fwd = pl.program_id(0) == 0
    c = CHUNK
    w = W_RWKV
    grp = PREP_GROUP
    ones4 = ones_ref[...]
    sgn = jnp.where(fwd, 1, -1)
    ti = lax.broadcasted_iota(jnp.int32, (grp, grp), 0)
    si = lax.broadcasted_iota(jnp.int32, (grp, grp), 1)
    same = jnp.where((ti // c) == (si // c), 1.0, 0.0)
    tri = jnp.where((si - ti) * sgn <= 0, same, 0.0).astype(BF16)
    first, lo, hi = _half_lane_masks()
    tq = lax.broadcasted_iota(jnp.int32, (c, QUAD), 0)
    sq = lax.broadcasted_iota(jnp.int32, (c, QUAD), 1) % c
    before = (sq - tq) * sgn < 0
    upto = (sq - tq) * sgn <= 0
    eye = jnp.where(sq == tq, 1.0, 0.0)

    def head_sum(x):
        xh, xl = _split2(x)
        parts = []
        for qd in range(w // QUAD):
            ls = slice(qd * QUAD, (qd + 1) * QUAD)
            parts.append(_dot(xh[:, ls], ones4) + _dot(xl[:, ls], ones4))
        return jnp.concatenate(parts, axis=1)

    def pm(x, y):
        return _dot(x.astype(BF16), _bd4(y, lo, hi))

    def prologue(g, out):
        gs = slice(g * grp, (g + 1) * grp)
        rkv = rkv_ref[gs, :].astype(F32)
        r, k, v = rkv[:, :w], rkv[:, w:2 * w], rkv[:, 2 * w:]
        lora = lora_ref[gs, :].astype(F32)
        zw = w0_ref[0] + _dot(jnp.tanh(lora[:, :W_LORA]).astype(BF16), wup_ref[0])
        za = a0_ref[0] + _dot(lora[:, W_LORA:].astype(BF16), aup_ref[0])
        yield
        ell = -math.exp(-0.5) * _sigmoid(zw)
        a = _sigmoid(za)
        kkr = k * kk_ref[...]
        k_d = k * (1.0 + (a - 1.0) * ka_ref[...])
        kk_ss = head_sum(kkr * kkr)
        rk_s = head_sum(r * k_d * rk_ref[...])
        yield
        kk = kkr * lax.rsqrt(kk_ss + L2_EPS)
        bonus_ref[0, gs, :] = (rk_s * v).astype(bonus_ref.dtype)
        ell_hi, ell_lo = _split2(ell)
        lc = _dot(tri, ell_hi) + _dot(tri, ell_lo)
        yield
        ltot = jnp.concatenate(
            [jnp.broadcast_to(jnp.where(fwd, lc[ch * c + c - 1:ch * c + c], lc[ch * c:ch * c + 1]), (c, w))
             for ch in range(grp // c)], axis=0)
        e_neg = jnp.exp(-lc)
        e_tail = jnp.exp(ltot - lc)
        kka = kk * a
        out.update(abar=kk * jnp.exp(lc - ell), bbar=kka * e_neg, kbar=k_d * e_neg, rbar=r * jnp.exp(lc),
                   btil=kka * e_tail, ktil=k_d * e_tail, v=v, gam=jnp.exp(ltot))

    def chains(g, q):
        sl = [(slice(ch * c, (ch + 1) * c), slice(qd * QUAD, (qd + 1) * QUAD))
              for ch in range(grp // c) for qd in range(w // QUAD)]
        abar, bbar, kbar, rbar, btil, ktil, v, gam = (
            q[nm] for nm in ("abar", "bbar", "kbar", "rbar", "btil", "ktil", "v", "gam"))
        la = [jnp.concatenate([abar[s], rbar[s]], axis=0).astype(BF16) for s in sl]
        nb = [_dot_nt(la_i, _bd4(bbar[s], lo, hi)) for la_i, s in zip(la, sl)]
        yield
        nk = [_dot_nt(la_i, _bd4(kbar[s], lo, hi)) for la_i, s in zip(la, sl)]
        yield
        n = [jnp.where(before, t[:c], 0.0) for t in nb]
        a_rb = [jnp.where(upto, t[c:], 0.0) for t in nb]
        a_ak = [jnp.where(before, t[:c], 0.0) for t in nk]
        a_rk = [jnp.where(upto, t[c:], 0.0) for t in nk]
        x = [eye - t for t in n]
        p = [pm(t, t) for t in n]
        yield
        py = [pm(jnp.concatenate([u, r_], axis=0), v[s]) for u, r_, s in zip(a_ak, a_rk, sl)]
        yield
        for it in range(5):
            if it < 4:
                xp = [pm(jnp.concatenate([x_i, p_i], axis=0), p_i) for x_i, p_i in zip(x, p)]
                x = [x_i + t[:c] for x_i, t in zip(x, xp)]
                p = [t[c:] for t in xp]
            else:
                x = [x_i + pm(x_i, p_i) for x_i, p_i in zip(x, p)]
            yield
        ah = [pm(x_i, abar[s]) for x_i, s in zip(x, sl)]
        u0 = [pm(x_i, t[:c]) for x_i, t in zip(x, py)]
        yield
        ra = [pm(m_i, t) for m_i, t in zip(a_rb, ah)]
        ru = [pm(m_i, t) for m_i, t in zip(a_rb, u0)]
        gfull = [_dot_tn(btil[s].astype(BF16), t.astype(BF16)) for s, t in zip(sl, ah)]
        hfull = [_dot_tn(jnp.concatenate([ktil[s], -btil[s]], axis=0).astype(BF16),
                         jnp.concatenate([v[s], t], axis=0).astype(BF16)) for s, t in zip(sl, u0)]
        for i, (rs, ls) in enumerate(sl):
            ro = slice(g * grp + rs.start, g * grp + rs.stop)
            rh_ref[0, ro, ls] = (rbar[rs, ls] - ra[i]).astype(BF16)
            y0_ref[0, ro, ls] = py[i][c:] - ru[i]
            g_ref[0, ro, ls] = (eye * gam[rs, ls] - _diag_blocks(gfull[i], first)).astype(g_ref.dtype)
            h_ref[0, ro, ls] = _diag_blocks(hfull[i], first)

    ngroups = rkv_ref.shape[0] // grp
    vals = [dict() for _ in range(ngroups)]
    for _ in prologue(0, vals[0]):
        pass
    for g in range(ngroups):
        nxt = prologue(g + 1, vals[g + 1]) if g + 1 < ngroups else iter(())
        for stage, _ in enumerate(chains(g, vals[g])):
            if stage >= 1 and stage % 2 == 1:
                next(nxt, None)
        for _ in nxt:
            pass


def _rwkv_prep(p2, w0, a0, w_up, a_up, k_k, k_a, r_k, ones4):
    n, _ = p2.shape
    w = W_RWKV
    rows = PREP_ROWS if n % PREP_ROWS == 0 else PREP_GROUP
    const = lambda shape: pl.BlockSpec(shape, lambda d, i: (0,) * len(shape))
    perdir = lambda shape: pl.BlockSpec((1,) + shape, lambda d, i: (d,) + (0,) * len(shape))
    out_spec = pl.BlockSpec((1, rows, w), lambda d, i: (d, i, 0))
    f32_out = jax.ShapeDtypeStruct((2, n, w), F32)
    act_out = jax.ShapeDtypeStruct((2, n, w), ACT)
    return pl.pallas_call(
        _rwkv_prep_kernel,
        out_shape=(act_out, f32_out, act_out, f32_out, act_out),
        grid=(2, n // rows),
        in_specs=[
            pl.BlockSpec((rows, 3 * w), lambda d, i: (i, PB_RKV)),
            pl.BlockSpec((rows, 2 * W_LORA), lambda d, i: (i, PB_LORA128 + d)),
            perdir((1, w)), perdir((1, w)), perdir((W_LORA, w)), perdir((A_LORA, w)),
            const((1, w)), const((1, w)), const((1, w)),
            const((QUAD, QUAD)),
        ],
        out_specs=(out_spec,) * 5,
        compiler_params=_cparams(("parallel", "parallel")),
        name="rwkv_prep",
    )(p2, p2, w0, a0, w_up, a_up, k_k, k_a, r_k, ones4)


def _rwkv_scan_kernel(rhf, y0f, gf, hf, rhb, y0b, gb, hb, yf_ref, yb_ref, st_ref, *, nb):
    @pl.when(pl.program_id(0) == 0)
    def _():
        st_ref[...] = jnp.zeros(st_ref.shape, F32)

    _, lo, hi = _half_lane_masks()
    c = CHUNK
    nsub = rhf.shape[2] // c
    for step in range(nsub):
        for d, sub, (rh, y0, g, hh, y_ref) in ((0, step, (rhf, y0f, gf, hf, yf_ref)),
                                               (1, nsub - 1 - step, (rhb, y0b, gb, hb, yb_ref))):
            rs = slice(sub * c, (sub + 1) * c)
            for bi in range(nb):
                for qd in range(W_RWKV // QUAD):
                    ls = slice(qd * QUAD, (qd + 1) * QUAD)
                    lhs = jnp.concatenate([rh[0, bi, rs, ls], g[0, bi, rs, ls]], axis=0)
                    res = _dot(lhs, _bd4(st_ref[d, bi, qd], lo, hi))
                    y_ref[bi, rs, ls] = (res[:c] + y0[0, bi, rs, ls]).astype(y_ref.dtype)
                    st_ref[d, bi, qd] = res[c:] + hh[0, bi, rs, ls]


def _rwkv_scan(rh, y0, g, h, *, n_x, n_ctx):
    _, b, t, w = rh.shape
    c = ROW_BLK
    n = n_x + n_ctx
    fidx = lambda j: jnp.where(j < n_ctx, n_x + j, j - n_ctx)
    bidx = lambda j: n - 1 - j
    fspec = pl.BlockSpec((1, b, c, w), lambda j: (0, 0, fidx(j), 0))
    bspec = pl.BlockSpec((1, b, c, w), lambda j: (1, 0, bidx(j), 0))
    y_shape = jax.ShapeDtypeStruct((b, t, w), ACT)
    return pl.pallas_call(
        functools.partial(_rwkv_scan_kernel, nb=b),
        out_shape=(y_shape, y_shape),
        grid=(n,),
        in_specs=[fspec] * 4 + [bspec] * 4,
        out_specs=(
            pl.BlockSpec((b, c, w), lambda j: (0, fidx(j), 0)),
            pl.BlockSpec((b, c, w), lambda j: (0, bidx(j), 0)),
        ),
        scratch_shapes=[pltpu.VMEM((2, b, w // QUAD, RWKV_HEAD, QUAD), F32)],
        compiler_params=_cparams(("arbitrary",)),
        name="rwkv_scan",
    )(rh, y0, g, h, rh, y0, g, h)


def _merge_kernel(z_ref, gate_x_ref, gate_c_ref, gl_ref, grw_ref, gml_ref, cin_ref, cb_ref, cc_ref, gcv_ref,
                  cin_p, cc_p, cin_n, cc_n, ymla_ref, yf_ref, yb_ref, bon_ref,
                  wbm_ref, wbc_ref, wbr_ref, wout_ref, gpost_ref, cw_ref, cbias_ref, gng_ref, gnb_ref,
                  avg_ref, o_ref, *, tiles_per_batch, seq, ctx_len):
    tm = z_ref.shape[0]
    d = D_MODEL
    pos = (pl.program_id(0) % tiles_per_batch) * tm + lax.broadcasted_iota(jnp.int32, (tm, 1), 0)
    has_prev = jnp.logical_and(pos != 0, pos != seq)
    has_next = jnp.logical_and(pos != seq - 1, pos != seq + ctx_len - 1)

    def f32(ref, idx=slice(None)):
        return ref[idx].astype(F32)

    u = f32(cc_ref) * f32(cin_ref)
    hl = HALO_ROWS - 1
    u_halo_p = f32(cc_p, slice(hl, hl + 1)) * f32(cin_p, slice(hl, hl + 1))
    u_halo_n = f32(cc_n, slice(0, 1)) * f32(cin_n, slice(0, 1))
    row = lax.broadcasted_iota(jnp.int32, u.shape, 0)
    u_prev = jnp.where(has_prev, jnp.where(row == 0, u_halo_p, pltpu.roll(u, 1, axis=0)), 0.0)
    u_next = jnp.where(has_next, jnp.where(row == tm - 1, u_halo_n, pltpu.roll(u, tm - 1, axis=0)), 0.0)
    cw = cw_ref[...]
    y_conv = f32(cb_ref) * (u_prev * cw[0:1] + u * cw[1:2] + u_next * cw[2:3] + cbias_ref[...])

    avg = avg_ref[...]

    def head_mean(x):
        hi, lo = _split2(x)
        parts = []
        for qd in range(W_RWKV // QUAD):
            ls = slice(qd * QUAD, (qd + 1) * QUAD)
            parts.append(_dot(hi[:, ls], avg) + _dot(lo[:, ls], avg))
        return jnp.concatenate(parts, axis=1)

    yr = f32(yf_ref) + f32(yb_ref)
    mu = head_mean(yr)
    dv = yr - mu
    var = head_mean(dv * dv)
    y_rwkv = dv * lax.rsqrt(var + GN_EPS) * gng_ref[...] + gnb_ref[...] + f32(bon_ref, 0) + f32(bon_ref, 1)

    br_mla = _dot((f32(ymla_ref) * _silu(f32(gml_ref))).astype(BF16), wbm_ref[...])
    br_conv = _dot((y_conv * _silu(f32(gcv_ref))).astype(BF16), wbc_ref[...])
    br_rwkv = _dot((y_rwkv * _silu(f32(grw_ref))).astype(BF16), wbr_ref[...])
    s = _sigmoid(f32(gl_ref))
    merged = s[:, :d] * br_mla + s[:, d:2 * d] * br_conv + s[:, 2 * d:] * br_rwkv
    o = _dot(merged.astype(BF16), wout_ref[...])
    gate = jnp.where(pos >= seq, gate_c_ref[0], gate_x_ref[0])
    o_ref[...] = z_ref[...] + gate * _rms(o, gpost_ref[...])


def _merge(z2, mods, p2, y_mla, yf, yb, bonus, wbm, wbc, wbr, wout, g_post, conv_w, conv_b,
           gn_g, gn_b, avg_bd, *, seq, ctx_len, tm, x_only):
    n, d = z2.shape
    t = seq + ctx_len
    assert t % tm == 0 and tm % HALO_ROWS == 0 and (not x_only or seq % tm == 0)
    hb = tm // HALO_ROWS
    nhb = n // HALO_ROWS
    tiles_per_batch = t // tm
    x_tiles = seq // tm
    n_tiles = (n // t) * x_tiles if x_only else n // tm
    gi = (lambda i: i // x_tiles * tiles_per_batch + i % x_tiles) if x_only else (lambda i: i)
    pcol = lambda blk: pl.BlockSpec((tm, 512), lambda i: (gi(i), blk))
    prev = lambda blk: pl.BlockSpec((HALO_ROWS, 512), lambda i: (jnp.maximum(gi(i) * hb - 1, 0), blk))
    nxt = lambda blk: pl.BlockSpec((HALO_ROWS, 512), lambda i: (jnp.minimum((gi(i) + 1) * hb, nhb - 1), blk))
    const = lambda shape: pl.BlockSpec(shape, lambda i: (0,) * len(shape))
    row512 = pl.BlockSpec((tm, 512), lambda i: (gi(i), 0))
    return pl.pallas_call(
        functools.partial(_merge_kernel, tiles_per_batch=x_tiles if x_only else tiles_per_batch,
                          seq=seq, ctx_len=ctx_len),
        out_shape=jax.ShapeDtypeStruct((n_tiles * tm, d), F32),
        grid=(n_tiles,),
        in_specs=[
            pl.BlockSpec((tm, d), lambda i: (gi(i), 0)),
            pl.BlockSpec((1, 1, d), lambda i: (1 + gi(i) // tiles_per_batch, 0, 2)),
            pl.BlockSpec((1, 1, d), lambda i: (0, 0, 2)),
            pl.BlockSpec((tm, 3 * d), lambda i: (gi(i), PB_GATE)),
            pcol(PB_GRWKV), pcol(PB_GMLA), pcol(PB_CVIN), pcol(PB_CVB), pcol(PB_CVC), pcol(PB_GCONV),
            prev(PB_CVIN), prev(PB_CVC), nxt(PB_CVIN), nxt(PB_CVC),
            row512, row512, row512,
            pl.BlockSpec((2, tm, 512), lambda i: (0, gi(i), 0)),
            const(wbm.shape), const(wbc.shape), const(wbr.shape), const(wout.shape),
            const((1, d)), const(conv_w.shape), const((1, 512)), const((1, 512)), const((1, 512)),
            const(avg_bd.shape),
        ],
        out_specs=pl.BlockSpec((tm, d), lambda i: (i, 0)),
        compiler_params=_cparams(("parallel",)),
        name="merge",
    )(z2, mods, mods, p2, p2, p2, p2, p2, p2, p2, p2, p2, p2, p2, y_mla, yf, yb, bonus,
      wbm, wbc, wbr, wout, g_post, conv_w, conv_b, gn_g, gn_b, avg_bd)


def _pair_swap(w):
    s = w.shape
    return w.reshape(s[:-1] + (s[-1] // 2, 2))[..., ::-1].reshape(s)


def _layout_w_in(w_in):
    sizes = (Q_LORA, KV_LORA, QK_ROPE, W_MLA, CONV_W, CONV_W, CONV_W, CONV_W, W_RWKV, W_RWKV, W_RWKV,
             W_LORA, W_LORA, A_LORA, A_LORA, W_RWKV, 3 * D_MODEL)
    offs = np.concatenate([[0], np.cumsum(sizes)])
    names = ("q_lat", "kv_lat", "kr", "g_mla", "cv_in", "cv_b", "cv_c", "g_conv", "r", "k", "v",
             "wd_f", "wd_b", "ad_f", "ad_b", "g_rwkv", "gl")
    col = {nm: w_in[..., offs[i]:offs[i + 1]] for i, nm in enumerate(names)}
    zeros = jnp.zeros(w_in.shape[:-1] + (512 - Q_LORA - 2 * QK_ROPE,), w_in.dtype)
    parts = [col["gl"], col["r"], col["k"], col["v"], col["g_rwkv"], col["g_mla"], col["cv_in"],
             col["cv_b"], col["cv_c"], col["g_conv"],
             col["q_lat"], col["kr"], _pair_swap(col["kr"]), zeros,
             col["kv_lat"], col["wd_f"], col["ad_f"], col["wd_b"], col["ad_b"]]
    out = jnp.concatenate(parts, axis=-1).astype(BF16)
    assert out.shape[-1] == PCOLS
    return out


def _layout_mla_weights(w_uq, w_ukv):
    depth = w_uq.shape[0]
    nh = N_HEADS_MLA
    wq = w_uq.reshape(depth, Q_LORA, nh, QK_HEAD)
    q_nope, q_rope = wq[..., :QK_NOPE], wq[..., QK_NOPE:]
    zq = jnp.zeros((depth, Q_LORA, nh, HEAD_PAD - QK_HEAD), w_uq.dtype)
    wq_a = jnp.concatenate([q_nope, q_rope, zq], axis=-1)
    wq_b = jnp.concatenate([jnp.zeros_like(q_nope), _pair_swap(q_rope), zq], axis=-1)
    wq_all = jnp.concatenate([wq_a.reshape(depth, Q_LORA, nh * HEAD_PAD),
                              wq_b.reshape(depth, Q_LORA, nh * HEAD_PAD)], axis=-1).astype(BF16)

    wkv = w_ukv.reshape(depth, KV_LORA, nh, QK_NOPE + V_HEAD)
    k_nope, v_w = wkv[..., :QK_NOPE], wkv[..., QK_NOPE:]
    zk = jnp.zeros((depth, KV_LORA, nh, HEAD_PAD - QK_NOPE), w_ukv.dtype)
    wk_top = jnp.concatenate([k_nope, zk], axis=-1).reshape(depth, KV_LORA, nh * HEAD_PAD)
    wv = jnp.concatenate([v_w, zk], axis=-1).reshape(depth, KV_LORA, nh * HEAD_PAD).astype(BF16)
    place = np.zeros((HEAD_PAD, HEAD_PAD), np.float32)
    place[np.arange(QK_ROPE), QK_NOPE + np.arange(QK_ROPE)] = 1.0
    e_a = np.tile(place, (1, nh))
    place_b = np.zeros((HEAD_PAD, HEAD_PAD), np.float32)
    place_b[QK_ROPE + np.arange(QK_ROPE), QK_NOPE + np.arange(QK_ROPE)] = 1.0
    e_b = np.tile(place_b, (1, nh))
    top = jnp.concatenate([wk_top, jnp.zeros_like(wk_top)], axis=-1)
    bot = jnp.broadcast_to(jnp.asarray(np.concatenate([e_a, e_b], axis=1)), (depth, HEAD_PAD, 2 * nh * HEAD_PAD))
    wk_all = jnp.concatenate([top, bot.astype(top.dtype)], axis=1).astype(BF16)
    return wq_all, wk_all, wv


def _rope_tables(seq, ctx_len):
    n_freq = QK_ROPE // 4
    pos = np.arange(seq)
    inv = ROPE_THETA ** (-np.arange(n_freq, dtype=np.float32) / n_freq)
    row = (pos // GRID_W).astype(np.float32)
    colp = (pos % GRID_W).astype(np.float32)
    ang = jnp.concatenate([jnp.asarray(row)[:, None] * jnp.asarray(inv), jnp.asarray(colp)[:, None] * jnp.asarray(inv)], axis=-1)
    cos, sin = jnp.cos(ang), jnp.sin(ang)
    cos2 = jnp.repeat(cos, 2, axis=-1)
    sin2 = jnp.stack([-sin, sin], axis=-1).reshape(seq, QK_ROPE)
    ones = jnp.ones((seq, QK_NOPE), F32)
    pad = jnp.zeros((seq, HEAD_PAD - QK_HEAD), F32)
    cos_x = jnp.concatenate([ones, cos2, pad], axis=-1)
    sin_x = jnp.concatenate([jnp.zeros_like(ones), sin2, pad], axis=-1)
    cos_c = jnp.concatenate([jnp.ones((ctx_len, QK_HEAD), F32), jnp.zeros((ctx_len, HEAD_PAD - QK_HEAD), F32)], axis=-1)
    sin_c = jnp.zeros((ctx_len, HEAD_PAD), F32)
    return jnp.concatenate([cos_x, cos_c], axis=0), jnp.concatenate([sin_x, sin_c], axis=0)


def _block_diag_const(n, blk, value):
    i = np.arange(n)
    return np.where((i[:, None] // blk) == (i[None, :] // blk), value, 0.0).astype(np.float32)


def _pick(n, candidates):
    for cand in candidates:
        if n % cand == 0:
            return cand
    raise ValueError(f"no tile for {n}")


def kernel(x, c, ctx, c_ctx, w_mod, b_mod, g_pre, g_post, w_in, g_q, g_kv, w_uq, w_ukv, conv_w, conv_b,
           w0, w_up, a0, a_up, k_k, k_a, r_k, gn_g, gn_b, w_br_mla, w_br_conv, w_br_rwkv, w_out):
    bsz, seq, d = x.shape
    ctx_len = ctx.shape[1]
    depth = w_mod.shape[0]
    assert d == D_MODEL and ctx_len == ROW_BLK and seq % ROW_BLK == 0
    t = seq + ctx_len
    n = bsz * t
    tiles_per_batch = t // ROW_BLK

    cc = jnp.zeros((8, d), F32).at[0].set(c_ctx).at[1:1 + bsz].set(c)
    mods = _adaln(cc, w_mod, b_mod).reshape(depth, 8, 1, 3 * d)

    w_in_p = _layout_w_in(w_in)
    wq_all, wk_all, wv_all = _layout_mla_weights(w_uq, w_ukv)
    cos_t, sin_t = _rope_tables(seq, ctx_len)
    vone = np.zeros((1, N_HEADS_MLA * HEAD_PAD), np.float32)
    vone[0, V_HEAD::HEAD_PAD] = 1.0
    vone = jnp.asarray(vone)
    ones4 = jnp.asarray(_block_diag_const(QUAD, RWKV_HEAD, 1.0), BF16)
    avg_bd = jnp.asarray(_block_diag_const(QUAD, RWKV_HEAD, 1.0 / RWKV_HEAD), BF16)

    row_tile = _pick(t, (768, 512, 256))
    tq = _pick(seq, (2048, 1024, 512, 256))
    tk = _pick(t, (768, 512, 256))

    z = jnp.concatenate([x, ctx], axis=1)
    for l in range(depth):
        z2 = z.reshape(n, d)
        p2 = _proj_in(z2, mods[l], g_pre[l][None], w_in_p, l, tiles_per_batch=tiles_per_batch)
        p3 = p2.reshape(bsz, t, PCOLS)

        qt, k, vt = _mla_prep(p3, cos_t, sin_t, g_q[l][None], g_kv[l][None], wq_all[l], wk_all[l], wv_all[l], vone,
                              tm=row_tile)
        y_mla = _attention(qt, k, vt, None, tq=tq, tk=tk, nq=seq // tq, nk=t // tk, q_off=0, k_off=0)
        y_mla = _attention(qt, k, vt, y_mla, tq=ctx_len, tk=ctx_len, nq=1, nk=1,
                           q_off=seq // ctx_len, k_off=seq // ctx_len)

        rh, y0, g, h, bonus = _rwkv_prep(
            p2, w0[l][:, None], a0[l][:, None], w_up[l].astype(BF16), a_up[l].astype(BF16),
            k_k[l][None], k_a[l][None], r_k[l].reshape(1, W_RWKV), ones4)
        per_batch = lambda arr: arr.reshape(2, bsz, t, W_RWKV)
        yf, yb = _rwkv_scan(per_batch(rh), per_batch(y0), per_batch(g), per_batch(h),
                            n_x=seq // ROW_BLK, n_ctx=ctx_len // ROW_BLK)

        z2 = _merge(z2, mods[l], p2, y_mla.reshape(n, W_MLA), yf.reshape(n, W_RWKV), yb.reshape(n, W_RWKV),
                    bonus,
                    w_br_mla[l].astype(BF16), w_br_conv[l].astype(BF16), w_br_rwkv[l].astype(BF16),
                    w_out[l].astype(BF16), g_post[l][None], conv_w[l], conv_b[l][None],
                    gn_g[l][None], gn_b[l][None], avg_bd,
                    seq=seq, ctx_len=ctx_len, x_only=l == depth - 1,
                    tm=ROW_BLK if l == depth - 1 else row_tile)
        if l < depth - 1:
            z = z2.reshape(bsz, t, d)
    return z2.reshape(bsz, seq, d)
```
